```python
import jax
import jax.numpy as jnp
from jax import lax
import numpy as np

D_MODEL = 1024
BATCH = 8
SEQ = 2048
DEPTH = 4
DEC_BATCH = 128
DEC_SEQ = 8
PAST_LEN = 2048
PAGE_SIZE = 128

HEAD_DIM = 64
ROT_DIM = HEAD_DIM // 4
ROPE_THETA = 500000.0
N_MIX_HEADS = D_MODEL // HEAD_DIM
NSA_HEADS = N_MIX_HEADS // 2
NSA_KV_HEADS = 2
NSA_GROUP = NSA_HEADS // NSA_KV_HEADS
CMP_BLOCK = 32
SLC_BLOCK = 64
SLC_TOPN = 16
WINDOW = 512
FOX_HEADS = N_MIX_HEADS - NSA_HEADS
DSA_HEADS = N_MIX_HEADS
DSA_KV_HEADS = 4
DSA_GROUP = DSA_HEADS // DSA_KV_HEADS
IDX_HEADS = 8
IDX_DIM = 64
IDX_TOPK = 256
Q_BLOCK = 128
D_FF = 2816
N_EXPERTS = 8
MOE_TOPK = 2
D_FF_EXPERT = 3584
N_EVEN = (DEPTH + 1) // 2
N_ODD = DEPTH // 2
DN_ALPHA = (2 * DEPTH) ** 0.25
DN_BETA = (8 * DEPTH) ** -0.25
LN_EPS = 1e-5

NSA_Q_W = NSA_HEADS * HEAD_DIM
NSA_KV_W = NSA_KV_HEADS * HEAD_DIM
FOX_W = FOX_HEADS * HEAD_DIM
EVEN_SPLITS = (NSA_Q_W, NSA_KV_W, NSA_KV_W, NSA_KV_W, NSA_KV_W, NSA_KV_W, NSA_KV_W,
               3 * NSA_HEADS, FOX_W, FOX_W, FOX_W, FOX_HEADS)
EVEN_IN = sum(EVEN_SPLITS)
ODD_SPLITS = (DSA_HEADS * HEAD_DIM, DSA_KV_HEADS * HEAD_DIM, DSA_KV_HEADS * HEAD_DIM,
              IDX_HEADS * IDX_DIM, IDX_DIM, IDX_HEADS)
ODD_IN = sum(ODD_SPLITS)

kernel_name = 'nsa_fox_dsa_hybrid_step'


def _split(h, sizes):
    offs = [int(o) for o in np.cumsum(sizes)[:-1]]
    return jnp.split(h, offs, axis=-1)


def _layer_norm(x, g, b):
    xf = x.astype(jnp.float32)
    mu = jnp.mean(xf, axis=-1, keepdims=True)
    var = jnp.mean(jnp.square(xf - mu), axis=-1, keepdims=True)
    y = (xf - mu) * lax.rsqrt(var + LN_EPS) * g.astype(jnp.float32) + b.astype(jnp.float32)
    return y.astype(x.dtype)


def _rope(x, pos):
    half = ROT_DIM // 2
    inv = ROPE_THETA ** (-2.0 * jnp.arange(half, dtype=jnp.float32) / ROT_DIM)
    ang = pos.astype(jnp.float32)[:, None] * inv[None, :]
    shape = (1, pos.shape[0]) + (1,) * (x.ndim - 3) + (half,)
    cos, sin = jnp.cos(ang).reshape(shape), jnp.sin(ang).reshape(shape)
    x1 = x[..., :half].astype(jnp.float32)
    x2 = x[..., half:ROT_DIM].astype(jnp.float32)
    rot = jnp.concatenate([x1 * cos - x2 * sin, x2 * cos + x1 * sin], axis=-1)
    return jnp.concatenate([rot.astype(x.dtype), x[..., ROT_DIM:]], axis=-1)


def _masked_softmax(s, mask):
    s = jnp.where(mask, s.astype(jnp.float32), -jnp.inf)
    m = jnp.max(s, axis=-1, keepdims=True)
    m = jnp.where(jnp.isfinite(m), m, 0.0)
    e = jnp.exp(s - m)
    return e / jnp.maximum(jnp.sum(e, axis=-1, keepdims=True), 1e-30)


def _gather_pages(pool, page_table, layer):
    g = pool[page_table, layer]
    return g.reshape((g.shape[0], g.shape[1] * g.shape[2]) + g.shape[3:])


def _sweep(fn, n_tokens):
    out = lax.map(fn, jnp.arange(n_tokens // Q_BLOCK))
    nb, b, qb, f = out.shape
    return jnp.transpose(out, (1, 0, 2, 3)).reshape(b, nb * qb, f)


def _nsa_compressed(k_rows, v_rows, pos_emb, phi):
    b, l, g, d = k_rows.shape
    nc = l // CMP_BLOCK

    def comp(rows, pe, w):
        blk = rows[:, :nc * CMP_BLOCK].reshape(b, nc, CMP_BLOCK, g, d) + pe[None, None, :, None, :]
        return jnp.einsum('bnlgd,lde->bnge', blk, w)

    c_end = (jnp.arange(nc) + 1) * CMP_BLOCK - 1
    ck = _rope(comp(k_rows, pos_emb[0], phi[0]), c_end)
    cv = comp(v_rows, pos_emb[1], phi[1])
    return ck, cv, c_end


def _slc_blocks(rows):
    b, l, g, d = rows.shape
    ns = -(-l // SLC_BLOCK)
    rows = jnp.pad(rows, ((0, 0), (0, ns * SLC_BLOCK - l), (0, 0), (0, 0)))
    return jnp.transpose(rows.reshape(b, ns, SLC_BLOCK, g, d), (0, 3, 1, 2, 4))


def _nsa_attend(q, qpos, ck, cv, c_end, sk, sv, wk, wv, wpos, gates):
    b, t, g = q.shape[:3]
    scale = HEAD_DIM ** -0.5
    s = jnp.einsum('btgrd,bngd->bgrtn', q, ck) * scale
    p_cmp = _masked_softmax(s, c_end[None, :] <= qpos[:, None])
    o_cmp = jnp.einsum('bgrtn,bngd->btgrd', p_cmp.astype(cv.dtype), cv)
    ns, nc = sk.shape[2], ck.shape[1]
    ratio = SLC_BLOCK // CMP_BLOCK
    imp = jnp.pad(jnp.sum(p_cmp, axis=2), ((0, 0), (0, 0), (0, 0), (0, ratio * ns - nc)))
    imp = jnp.sum(imp.reshape(b, g, t, ns, ratio), axis=-1)
    blk_id = jnp.arange(ns)
    valid = blk_id[None, :] * SLC_BLOCK <= qpos[:, None]
    forced = (blk_id[None, :] == 0) | (blk_id[None, :] == (qpos // SLC_BLOCK)[:, None])
    score = jnp.where(forced, jnp.inf, jnp.where(valid, imp, -jnp.inf))
    n_sel = min(SLC_TOPN, ns)
    _, idx = lax.top_k(score, n_sel)
    bi = jnp.arange(b)[:, None, None, None]
    gi = jnp.arange(g)[None, :, None, None]
    kg, vg = sk[bi, gi, idx], sv[bi, gi, idx]
    kpos = idx[..., None] * SLC_BLOCK + jnp.arange(SLC_BLOCK)
    kmask = (kpos <= qpos[None, None, :, None, None]).reshape(b, g, 1, t, n_sel * SLC_BLOCK)
    s = jnp.einsum('btgrd,bgtnkd->bgrtnk', q, kg).reshape(b, g, NSA_GROUP, t, n_sel * SLC_BLOCK) * scale
    p = _masked_softmax(s, kmask)
    o_slc = jnp.einsum('bgrtm,bgtmd->btgrd', p.astype(vg.dtype), vg.reshape(b, g, t, n_sel * SLC_BLOCK, HEAD_DIM))
    s = jnp.einsum('btgrd,bsgd->bgrts', q, wk) * scale
    wmask = (wpos[None, :] <= qpos[:, None]) & (wpos[None, :] > qpos[:, None] - WINDOW) & (wpos[None, :] >= 0)
    p = _masked_softmax(s, wmask)
    o_win = jnp.einsum('bgrts,bsgd->btgrd', p.astype(wv.dtype), wv)
    o = gates[..., 0:1] * o_cmp + gates[..., 1:2] * o_slc + gates[..., 2:3] * o_win
    return o.reshape(b, t, NSA_HEADS * HEAD_DIM)


def _fox_attend(q, qpos, k, v, cum_q, cum_k):
    b, t = q.shape[:2]
    l = k.shape[1]
    s = jnp.einsum('bthd,bshd->bhts', q, k).astype(jnp.float32) * (HEAD_DIM ** -0.5)
    s = s + jnp.transpose(cum_q, (0, 2, 1))[:, :, :, None] - jnp.transpose(cum_k, (0, 2, 1))[:, :, None, :]
    p = _masked_softmax(s, jnp.arange(l)[None, :] <= qpos[:, None])
    o = jnp.einsum('bhts,bshd->bthd', p.astype(v.dtype), v)
    return o.reshape(b, t, FOX_HEADS * HEAD_DIM)


def _dsa_attend(q, qi, wi, qpos, k, v, ki, n_keep):
    b, t = q.shape[:2]
    l = k.shape[1]
    logits = jnp.einsum('bthe,bse->bths', qi, ki).astype(jnp.float32) * (IDX_DIM ** -0.5)
    score = jnp.einsum('bth,bths->bts', wi.astype(jnp.float32), jax.nn.relu(logits))
    causal = jnp.arange(l)[None, None, :] <= qpos[None, :, None]
    _, idx = lax.top_k(jnp.where(causal, score, -jnp.inf), n_keep)
    bi = jnp.arange(b)[:, None, None]
    kg, vg = k[bi, idx], v[bi, idx]
    qg = q.reshape(b, t, DSA_KV_HEADS, DSA_GROUP, HEAD_DIM)
    s = jnp.einsum('btgrd,btngd->bgrtn', qg, kg) * (HEAD_DIM ** -0.5)
    p = _masked_softmax(s, (idx <= qpos[None, :, None])[:, None, None])
    o = jnp.einsum('bgrtn,btngd->btgrd', p.astype(vg.dtype), vg)
    return o.reshape(b, t, DSA_HEADS * HEAD_DIM)


def _even_project(x, pos, w_in, f_bias):
    b, t, _ = x.shape
    nq, kc, vc, ks, vs, kw, vw, g, fq, fk, fv, ff = _split(x @ w_in, EVEN_SPLITS)
    kvh = lambda a: a.reshape(b, t, NSA_KV_HEADS, HEAD_DIM)
    fh = lambda a: a.reshape(b, t, FOX_HEADS, HEAD_DIM)
    nq = _rope(nq.reshape(b, t, NSA_HEADS, HEAD_DIM), pos).reshape(b, t, NSA_KV_HEADS, NSA_GROUP, HEAD_DIM)
    gates = jax.nn.sigmoid(g.reshape(b, t, NSA_KV_HEADS, NSA_GROUP, 3))
    logf = jax.nn.log_sigmoid(ff.astype(jnp.float32) + f_bias.astype(jnp.float32))
    return (nq, kvh(kc), kvh(vc), _rope(kvh(ks), pos), kvh(vs), _rope(kvh(kw), pos), kvh(vw),
            gates, fh(fq), fh(fk), fh(fv), logf)


def _even_prompt(x, w_in, w_out, f_bias, cmp_pos, cmp_phi):
    b, t, _ = x.shape
    pos = jnp.arange(t)
    nq, kc, vc, ks, vs, kw, vw, gates, fq, fk, fv, logf = _even_project(x, pos, w_in, f_bias)
    ck, cv, c_end = _nsa_compressed(kc, vc, cmp_pos, cmp_phi)
    sk, sv = _slc_blocks(ks), _slc_blocks(vs)
    pad = ((0, 0), (WINDOW, 0), (0, 0), (0, 0))
    kw_pad, vw_pad = jnp.pad(kw, pad), jnp.pad(vw, pad)
    cum = jnp.cumsum(logf, axis=1)

    def nsa_fn(i):
        qs = i * Q_BLOCK
        qpos = qs + jnp.arange(Q_BLOCK)
        wpos = qs - WINDOW + jnp.arange(WINDOW + Q_BLOCK)
        blk = lambda a: lax.dynamic_slice_in_dim(a, qs, Q_BLOCK, axis=1)
        band = lambda a: lax.dynamic_slice_in_dim(a, qs, WINDOW + Q_BLOCK, axis=1)
        return _nsa_attend(blk(nq), qpos, ck, cv, c_end, sk, sv, band(kw_pad), band(vw_pad), wpos, blk(gates))

    def fox_fn(i):
        qs = i * Q_BLOCK
        blk = lambda a: lax.dynamic_slice_in_dim(a, qs, Q_BLOCK, axis=1)
        return _fox_attend(blk(fq), qs + jnp.arange(Q_BLOCK), fk, fv, blk(cum), cum)

    o = jnp.concatenate([_sweep(nsa_fn, t), _sweep(fox_fn, t)], axis=-1)
    y = o.astype(x.dtype) @ w_out
    keep = min(WINDOW, t)
    win = jnp.stack([kw, vw], axis=2)[:, t - keep:]
    return y, (jnp.stack([kc, vc], axis=2), jnp.stack([ks, vs], axis=2), win,
               jnp.stack([fk, fv], axis=2), logf.astype(x.dtype))


def _even_sample(x, page_table, cache_cmp, cache_slc, win_buf, cache_fkv, cache_flogf, layer,
                 w_in, w_out, f_bias, cmp_pos, cmp_phi):
    b, t, _ = x.shape
    past_c = _gather_pages(cache_cmp, page_table, layer)
    past_s = _gather_pages(cache_slc, page_table, layer)
    p_len = past_c.shape[1]
    pos = p_len + jnp.arange(t)
    nq, kc, vc, ks, vs, kw, vw, gates, fq, fk, fv, logf = _even_project(x, pos, w_in, f_bias)
    ck, cv, c_end = _nsa_compressed(jnp.concatenate([past_c[:, :, 0], kc], axis=1),
                                    jnp.concatenate([past_c[:, :, 1], vc], axis=1), cmp_pos, cmp_phi)
    sk = _slc_blocks(jnp.concatenate([past_s[:, :, 0], ks], axis=1))
    sv = _slc_blocks(jnp.concatenate([past_s[:, :, 1], vs], axis=1))
    wb = win_buf.shape[1]
    win_all = jnp.concatenate([win_buf, jnp.stack([kw, vw], axis=2).astype(win_buf.dtype)], axis=1)
    wpos = p_len - wb + jnp.arange(wb + t)
    o_nsa = _nsa_attend(nq, pos, ck, cv, c_end, sk, sv, win_all[:, :, 0], win_all[:, :, 1], wpos, gates)
    past_f = _gather_pages(cache_fkv, page_table, layer)
    past_lf = _gather_pages(cache_flogf, page_table, layer)
    cum = jnp.cumsum(jnp.concatenate([past_lf.astype(jnp.float32), logf], axis=1), axis=1)
    o_fox = _fox_attend(fq, pos, jnp.concatenate([past_f[:, :, 0], fk], axis=1),
                        jnp.concatenate([past_f[:, :, 1], fv], axis=1), cum[:, p_len:], cum)
    y = jnp.concatenate([o_nsa, o_fox], axis=-1).astype(x.dtype) @ w_out
    keep = min(WINDOW, wb + t)
    return y, (jnp.stack([kc, vc], axis=2), jnp.stack([ks, vs], axis=2), win_all[:, wb + t - keep:],
               jnp.stack([fk, fv], axis=2), logf.astype(x.dtype))


def _odd_project(x, pos, w_in):
    b, t, _ = x.shape
    q, k, v, qi, ki, wi = _split(x @ w_in, ODD_SPLITS)
    q = _rope(q.reshape(b, t, DSA_HEADS, HEAD_DIM), pos)
    k = _rope(k.reshape(b, t, DSA_KV_HEADS, HEAD_DIM), pos)
    v = v.reshape(b, t, DSA_KV_HEADS, HEAD_DIM)
    qi = _rope(qi.reshape(b, t, IDX_HEADS, IDX_DIM), pos)
    ki = _rope(ki, pos)
    return q, k, v, qi, ki, wi * (IDX_HEADS ** -0.5)


def _odd_prompt(x, w_in, w_out):
    b, t, _ = x.shape
    q, k, v, qi, ki, wi = _odd_project(x, jnp.arange(t), w_in)
    n_keep = min(IDX_TOPK, t // 4)

    def fn(i):
        qs = i * Q_BLOCK
        blk = lambda a: lax.dynamic_slice_in_dim(a, qs, Q_BLOCK, axis=1)
        return _dsa_attend(blk(q), blk(qi), blk(wi), qs + jnp.arange(Q_BLOCK), k, v, ki, n_keep)

    y = _sweep(fn, t).astype(x.dtype) @ w_out
    return y, (jnp.stack([k, v], axis=2), ki)


def _odd_sample(x, page_table, cache_kv, cache_idx, layer, w_in, w_out):
    b, t, _ = x.shape
    past_kv = _gather_pages(cache_kv, page_table, layer)
    past_ki = _gather_pages(cache_idx, page_table, layer)
    p_len = past_kv.shape[1]
    pos = p_len + jnp.arange(t)
    q, k, v, qi, ki, wi = _odd_project(x, pos, w_in)
    n_keep = min(IDX_TOPK, (p_len + t) // 4)
    o = _dsa_attend(q, qi, wi, pos, jnp.concatenate([past_kv[:, :, 0], k], axis=1),
                    jnp.concatenate([past_kv[:, :, 1], v], axis=1),
                    jnp.concatenate([past_ki, ki], axis=1), n_keep)
    y = o.astype(x.dtype) @ w_out
    return y, (jnp.stack([k, v], axis=2), ki)


def _swiglu(x, w_gu, w_down):
    a, g = jnp.split(x @ w_gu, 2, axis=-1)
    return (jax.nn.silu(a) * g) @ w_down


def _moe(x, w_router, w_gu, w_down):
    logits = (x @ w_router).astype(jnp.float32)
    top_v, top_i = lax.top_k(logits, MOE_TOPK)
    top_w = jax.nn.softmax(top_v, axis=-1)
    comb = jnp.sum(jax.nn.one_hot(top_i, N_EXPERTS, dtype=jnp.float32) * top_w[..., None], axis=-2)
    y = jnp.zeros_like(x)
    for e in range(N_EXPERTS):
        y = y + comb[..., e:e + 1].astype(x.dtype) * _swiglu(x, w_gu[e], w_down[e])
    return y


def setup_inputs(seed: int = 0) -> dict:
    keys = jax.random.split(jax.random.key(seed), 32)

    def nrm(i, shape, scale=1.0):
        return jax.random.normal(keys[i], shape, jnp.float32) * scale

    n_pages = PAST_LEN // PAGE_SIZE
    n_used = DEC_BATCH * n_pages
    n_pool = n_used + (n_used + 3) // 4
    page_table = jax.random.permutation(keys[0], n_pool)[:n_used].reshape(DEC_BATCH, n_pages).astype(jnp.int32)
    win_len = min(WINDOW, PAST_LEN)
    hd = HEAD_DIM
    return {
        'x_prompt': nrm(1, (BATCH, SEQ, D_MODEL)),
        'x_sample': nrm(2, (DEC_BATCH, DEC_SEQ, D_MODEL)),
        'cache_nsa_cmp_kv': nrm(3, (n_pool, N_EVEN, PAGE_SIZE, 2, NSA_KV_HEADS, hd)),
        'cache_nsa_slc_kv': nrm(4, (n_pool, N_EVEN, PAGE_SIZE, 2, NSA_KV_HEADS, hd)),
        'state_nsa_win_kv': nrm(5, (DEC_BATCH, N_EVEN, win_len, 2, NSA_KV_HEADS, hd)),
        'cache_fox_kv': nrm(6, (n_pool, N_EVEN, PAGE_SIZE, 2, FOX_HEADS, hd)),
        'cache_fox_logf': jax.nn.log_sigmoid(4.0 + nrm(7, (n_pool, N_EVEN, PAGE_SIZE, FOX_HEADS))),
        'cache_dsa_kv': nrm(8, (n_pool, N_ODD, PAGE_SIZE, 2, DSA_KV_HEADS, hd)),
        'cache_dsa_idx_k': nrm(9, (n_pool, N_ODD, PAGE_SIZE, IDX_DIM)),
        'page_table': page_table,
        'ln_g': 1.0 + nrm(10, (DEPTH, 2, D_MODEL), 0.02),
        'ln_b': nrm(11, (DEPTH, 2, D_MODEL), 0.02),
        'w_in_even': nrm(12, (N_EVEN, D_MODEL, EVEN_IN), D_MODEL ** -0.5),
        'w_out_even': nrm(13, (N_EVEN, NSA_Q_W + FOX_W, D_MODEL), DN_BETA * (NSA_Q_W + FOX_W) ** -0.5),
        'fox_f_bias': 4.0 + nrm(14, (N_EVEN, FOX_HEADS), 0.5),
        'nsa_cmp_pos': nrm(15, (N_EVEN, 2, CMP_BLOCK, hd), 0.1),
        'nsa_cmp_phi': nrm(16, (N_EVEN, 2, CMP_BLOCK, hd, hd), (CMP_BLOCK * hd) ** -0.5),
        'w_in_odd': nrm(17, (N_ODD, D_MODEL, ODD_IN), D_MODEL ** -0.5),
        'w_out_odd': nrm(18, (N_ODD, DSA_HEADS * hd, D_MODEL), DN_BETA * (DSA_HEADS * hd) ** -0.5),
        'ffn_gu': nrm(19, (N_EVEN, D_MODEL, 2 * D_FF), D_MODEL ** -0.5),
        'ffn_down': nrm(20, (N_EVEN, D_FF, D_MODEL), DN_BETA * D_FF ** -0.5),
        'moe_router': nrm(21, (N_ODD, D_MODEL, N_EXPERTS), D_MODEL ** -0.5),
        'moe_gu': nrm(22, (N_ODD, N_EXPERTS, D_MODEL, 2 * D_FF_EXPERT), D_MODEL ** -0.5),
        'moe_down': nrm(23, (N_ODD, N_EXPERTS, D_FF_EXPERT, D_MODEL), DN_BETA * D_FF_EXPERT ** -0.5),
    }


def reference(x_prompt, x_sample, cache_nsa_cmp_kv, cache_nsa_slc_kv, state_nsa_win_kv,
              cache_fox_kv, cache_fox_logf, cache_dsa_kv, cache_dsa_idx_k, page_table,
              ln_g, ln_b, w_in_even, w_out_even, fox_f_bias, nsa_cmp_pos, nsa_cmp_phi,
              w_in_odd, w_out_odd, ffn_gu, ffn_down, moe_router, moe_gu, moe_down):
    xp, xs = x_prompt, x_sample
    ev_p, ev_s, od_p, od_s = [], [], [], []
    for layer in range(DEPTH):
        j = layer // 2
        if layer % 2 == 0:
            yp, st_p = _even_prompt(xp, w_in_even[j], w_out_even[j], fox_f_bias[j], nsa_cmp_pos[j], nsa_cmp_phi[j])
            ys, st_s = _even_sample(xs, page_table, cache_nsa_cmp_kv, cache_nsa_slc_kv, state_nsa_win_kv[:, j],
                                    cache_fox_kv, cache_fox_logf, j, w_in_even[j], w_out_even[j],
                                    fox_f_bias[j], nsa_cmp_pos[j], nsa_cmp_phi[j])
            ev_p.append(st_p)
            ev_s.append(st_s)
        else:
            yp, st_p = _odd_prompt(xp, w_in_odd[j], w_out_odd[j])
            ys, st_s = _odd_sample(xs, page_table, cache_dsa_kv, cache_dsa_idx_k, j, w_in_odd[j], w_out_odd[j])
            od_p.append(st_p)
            od_s.append(st_s)
        xp = _layer_norm(DN_ALPHA * xp + yp, ln_g[layer, 0], ln_b[layer, 0])
        xs = _layer_norm(DN_ALPHA * xs + ys, ln_g[layer, 0], ln_b[layer, 0])
        if layer % 2 == 0:
            fp = _swiglu(xp, ffn_gu[j], ffn_down[j])
            fs = _swiglu(xs, ffn_gu[j], ffn_down[j])
        else:
            fp = _moe(xp, moe_router[j], moe_gu[j], moe_down[j])
            fs = _moe(xs, moe_router[j], moe_gu[j], moe_down[j])
        xp = _layer_norm(DN_ALPHA * xp + fp, ln_g[layer, 1], ln_b[layer, 1])
        xs = _layer_norm(DN_ALPHA * xs + fs, ln_g[layer, 1], ln_b[layer, 1])

    def stk(lst, idx):
        return jnp.stack([s[idx] for s in lst], axis=1)

    return (xp, xs,
            stk(ev_p, 0), stk(ev_s, 0), stk(ev_p, 1), stk(ev_s, 1), stk(ev_p, 2), stk(ev_s, 2),
            stk(ev_p, 3), stk(ev_s, 3), stk(ev_p, 4), stk(ev_s, 4),
            stk(od_p, 0), stk(od_s, 0), stk(od_p, 1), stk(od_s, 1))
```

```python
import functools
import math

import jax
import jax.numpy as jnp
from jax import lax
from jax.experimental import pallas as pl
from jax.experimental.pallas import tpu as pltpu

F32 = jnp.float32
BF16 = jnp.bfloat16
NEG_INF = float("-inf")

HEAD_DIM = 64
ROT_DIM = HEAD_DIM // 4
ROPE_THETA = 500000.0
NSA_HEADS = 8
NSA_KV_HEADS = 2
NSA_GROUP = NSA_HEADS // NSA_KV_HEADS
CMP_BLOCK = 32
SLC_BLOCK = 64
SLC_TOPN = 16
WINDOW = 512
FOX_HEADS = 8
DSA_HEADS = 16
DSA_KV_HEADS = 4
DSA_GROUP = DSA_HEADS // DSA_KV_HEADS
IDX_HEADS = 8
IDX_DIM = 64
IDX_TOPK = 256
N_EXPERTS = 8
LN_EPS = 1e-5
PAGE_SIZE = 128
LANES = 128

E_NQ, E_FQ, E_FK, E_FV, E_CMP, E_SLC, E_WIN, E_MISC, E_W = 0, 512, 1024, 1536, 2048, 2304, 2560, 2816, 2944
MISC_GATES = 0
MISC_FF = 24
EVEN_ROPE = (0, 1, 2, 3, E_SLC // LANES, E_WIN // LANES)
O_Q, O_KV, O_QI, O_KI, O_W = 0, 1024, 1536, 2048, 2176
ODD_ROPE = tuple(range(0, 10)) + tuple(range(12, 16))
ODD_ROPE_LO = (O_KI // LANES,)

VMEM_LIMIT = 48 * 1024 * 1024


def _dot(a, b):
    return jnp.dot(a, b, preferred_element_type=F32)


def _dot_nt(a, b):
    return lax.dot_general(a, b, (((1,), (1,)), ((), ())), preferred_element_type=F32)


def _iota(shape, dim):
    return lax.broadcasted_iota(jnp.int32, shape, dim)


def _rope128(v, c, s1, s2):
    return v * c + pltpu.roll(v, 8, 1) * s1 + pltpu.roll(v, LANES - 8, 1) * s2


def _split3(x):
    hi = x.astype(BF16).astype(F32)
    r = x - hi
    mid = r.astype(BF16).astype(F32)
    lo = (r - mid).astype(BF16).astype(F32)
    return hi, mid, lo


def _softmax_init(rows):
    return (jnp.full((rows, 1), NEG_INF, F32), jnp.zeros((rows, 1), F32), jnp.zeros((rows, HEAD_DIM), F32))


def _softmax_update(state, s, v):
    m_prev, l_prev, acc_prev = state
    m_new = jnp.maximum(m_prev, jnp.max(s, axis=-1, keepdims=True))
    m_safe = jnp.where(m_new == NEG_INF, 0.0, m_new)
    alpha = jnp.exp(m_prev - m_safe)
    p = jnp.exp(s - m_safe)
    l_new = alpha * l_prev + jnp.sum(p, axis=-1, keepdims=True)
    acc_new = alpha * acc_prev + _dot(p.astype(BF16), v)
    return m_new, l_new, acc_new


def _softmax_finish(state):
    _, l, acc = state
    return acc / jnp.maximum(l, 1e-30)


def _pad_rows(a, rows):
    if a.shape[0] == rows:
        return a
    return jnp.concatenate([a, jnp.zeros((rows - a.shape[0], a.shape[1]), a.dtype)], axis=0)


def _round_up(n, m):
    return -(-n // m) * m


def _layer_norm(z, g, b):
    mu = jnp.mean(z, axis=-1, keepdims=True)
    d = z - mu
    var = jnp.mean(d * d, axis=-1, keepdims=True)
    return d * lax.rsqrt(var + LN_EPS) * g + b


def _proj_kernel(x_ref, w_ref, rt_ref, o_ref, *, width, rope_full, rope_lo, cw):
    xb = x_ref[...].astype(BF16)
    c, s1, s2 = rt_ref[:, 0:128], rt_ref[:, 128:256], rt_ref[:, 256:384]
    lo = _iota((1, LANES), 1) < HEAD_DIM
    for c0 in range(0, width, cw):
        c1 = min(c0 + cw, width)
        acc = _dot(xb, w_ref[:, c0:c1])
        for blk in range(c0 // LANES, c1 // LANES):
            v = acc[:, blk * LANES - c0:(blk + 1) * LANES - c0]
            if blk in rope_full:
                v = _rope128(v, c, s1, s2)
            elif blk in rope_lo:
                v = _rope128(v, jnp.where(lo, c, 1.0), jnp.where(lo, s1, 0.0), jnp.where(lo, s2, 0.0))
            o_ref[:, blk * LANES:(blk + 1) * LANES] = v


def _proj(x, w, rt, *, width, rope_full, rope_lo, tm):
    n, d = x.shape
    return pl.pallas_call(
        functools.partial(_proj_kernel, width=width, rope_full=rope_full, rope_lo=rope_lo, cw=512),
        grid=(n // tm,),
        in_specs=[pl.BlockSpec((tm, d), lambda i: (i, 0)),
                  pl.BlockSpec((d, width), lambda i: (0, 0)),
                  pl.BlockSpec((tm, 3 * LANES), lambda i: (i, 0))],
        out_specs=pl.BlockSpec((tm, width), lambda i: (i, 0)),
        out_shape=jax.ShapeDtypeStruct((n, width), F32),
        compiler_params=pltpu.CompilerParams(dimension_semantics=("arbitrary",), vmem_limit_bytes=VMEM_LIMIT),
        name="proj",
    )(x, w, rt)


def _out_ln_kernel(*refs, n_parts, n_prompt_tiles, alpha):
    op = refs[0:n_parts]
    os_ = refs[n_parts:2 * n_parts]
    ws = refs[2 * n_parts:3 * n_parts]
    x_ref, g_ref, b_ref, o_ref = refs[3 * n_parts:]
    i = pl.program_id(0)

    def compute(parts):
        y = None
        for o, w in zip(parts, ws):
            t = _dot(o[...].astype(BF16), w[...])
            y = t if y is None else y + t
        o_ref[...] = _layer_norm(alpha * x_ref[...] + y, g_ref[...], b_ref[...])

    @pl.when(i < n_prompt_tiles)
    def _():
        compute(op)

    @pl.when(i >= n_prompt_tiles)
    def _():
        compute(os_)


def _out_ln(o_prompt, o_sample, w_parts, x, g, b, *, alpha, tm):
    n, d = x.shape
    npt = o_prompt[0].shape[0] // tm
    nst = o_sample[0].shape[0] // tm
    k = len(w_parts)
    in_specs = []
    for o in o_prompt:
        in_specs.append(pl.BlockSpec((tm, o.shape[1]), lambda i: (jnp.minimum(i, npt - 1), 0)))
    for o in o_sample:
        in_specs.append(pl.BlockSpec((tm, o.shape[1]), lambda i: (jnp.maximum(i - npt, 0), 0)))
    for w in w_parts:
        in_specs.append(pl.BlockSpec(w.shape, lambda i: (0, 0)))
    in_specs += [pl.BlockSpec((tm, d), lambda i: (i, 0)),
                 pl.BlockSpec((1, d), lambda i: (0, 0)),
                 pl.BlockSpec((1, d), lambda i: (0, 0))]
    assert npt + nst == n // tm
    return pl.pallas_call(
        functools.partial(_out_ln_kernel, n_parts=k, n_prompt_tiles=npt, alpha=alpha),
        grid=(n // tm,),
        in_specs=in_specs,
        out_specs=pl.BlockSpec((tm, d), lambda i: (i, 0)),
        out_shape=jax.ShapeDtypeStruct((n, d), F32),
        compiler_params=pltpu.CompilerParams(dimension_semantics=("arbitrary",), vmem_limit_bytes=VMEM_LIMIT),
        name="out_ln",
    )(*o_prompt, *o_sample, *w_parts, x, g, b)


def _ffn_ln_kernel(x_ref, wa_ref, wg_ref, wd_ref, g_ref, b_ref, o_ref, acc_ref, *, alpha, n_chunks):
    c = pl.program_id(1)

    @pl.when(c == 0)
    def _():
        acc_ref[...] = jnp.zeros_like(acc_ref)

    xb = x_ref[...].astype(BF16)
    a = _dot(xb, wa_ref[...])
    gg = _dot(xb, wg_ref[...])
    hmid = (jax.nn.silu(a) * gg).astype(BF16)
    acc_ref[...] += _dot(hmid, wd_ref[...])

    @pl.when(c == n_chunks - 1)
    def _():
        o_ref[...] = _layer_norm(alpha * x_ref[...] + acc_ref[...], g_ref[...], b_ref[...])


def _ffn_ln(x, w_gu, w_down, g, b, *, alpha, tm, fc):
    n, d = x.shape
    dff = w_down.shape[0]
    nch = dff // fc
    return pl.pallas_call(
        functools.partial(_ffn_ln_kernel, alpha=alpha, n_chunks=nch),
        grid=(n // tm, nch),
        in_specs=[pl.BlockSpec((tm, d), lambda i, c: (i, 0)),
                  pl.BlockSpec((d, fc), lambda i, c: (0, c)),
                  pl.BlockSpec((d, fc), lambda i, c: (0, c + nch)),
                  pl.BlockSpec((fc, d), lambda i, c: (c, 0)),
                  pl.BlockSpec((1, d), lambda i, c: (0, 0)),
                  pl.BlockSpec((1, d), lambda i, c: (0, 0))],
        out_specs=pl.BlockSpec((tm, d), lambda i, c: (i, 0)),
        out_shape=jax.ShapeDtypeStruct((n, d), F32),
        scratch_shapes=[pltpu.VMEM((tm, d), F32)],
        compiler_params=pltpu.CompilerParams(dimension_semantics=("arbitrary", "arbitrary"),
                                             vmem_limit_bytes=VMEM_LIMIT),
        name="ffn_ln",
    )(x, w_gu, w_gu, w_down, g, b)


def _moe_ln_kernel(x_ref, wrh_ref, wrl_ref, wa_ref, wg_ref, wd_ref, g_ref, b_ref, o_ref, acc_ref, comb_ref,
                   *, alpha, n_chunks):
    e = pl.program_id(1)
    c = pl.program_id(2)
    lane = _iota((1, LANES), 1).astype(F32)

    @pl.when((e == 0) & (c == 0))
    def _():
        acc_ref[...] = jnp.zeros_like(acc_ref)
        x = x_ref[...]
        xh = x.astype(BF16)
        xl = (x - xh.astype(F32)).astype(BF16)
        logits = _dot(xh, wrh_ref[...]) + _dot(xl, wrh_ref[...]) + _dot(xh, wrl_ref[...])
        lg = jnp.where(lane < N_EXPERTS, logits, NEG_INF)
        m1 = jnp.max(lg, axis=-1, keepdims=True)
        i1 = jnp.min(jnp.where(lg == m1, lane, float(LANES)), axis=-1, keepdims=True)
        lg2 = jnp.where(lane == i1, NEG_INF, lg)
        m2 = jnp.max(lg2, axis=-1, keepdims=True)
        i2 = jnp.min(jnp.where(lg2 == m2, lane, float(LANES)), axis=-1, keepdims=True)
        e2 = jnp.exp(m2 - m1)
        den = 1.0 + e2
        comb_ref[...] = jnp.where(lane == i1, 1.0 / den, 0.0) + jnp.where(lane == i2, e2 / den, 0.0)

    xb = x_ref[...].astype(BF16)
    a = _dot(xb, wa_ref[...])
    gg = _dot(xb, wg_ref[...])
    hmid = (jax.nn.silu(a) * gg).astype(BF16)
    ce = jnp.sum(jnp.where(lane == e.astype(F32), comb_ref[...], 0.0), axis=-1, keepdims=True)
    acc_ref[...] += ce * _dot(hmid, wd_ref[...])

    @pl.when((e == N_EXPERTS - 1) & (c == n_chunks - 1))
    def _():
        o_ref[...] = _layer_norm(alpha * x_ref[...] + acc_ref[...], g_ref[...], b_ref[...])


def _moe_ln(x, wr_hi, wr_lo, w_gu, w_down, g, b, *, alpha, tm, fc):
    n, d = x.shape
    dff = w_down.shape[1]
    nch = dff // fc
    return pl.pallas_call(
        functools.partial(_moe_ln_kernel, alpha=alpha, n_chunks=nch),
        grid=(n // tm, N_EXPERTS, nch),
        in_specs=[pl.BlockSpec((tm, d), lambda i, e, c: (i, 0)),
                  pl.BlockSpec((d, LANES), lambda i, e, c: (0, 0)),
                  pl.BlockSpec((d, LANES), lambda i, e, c: (0, 0)),
                  pl.BlockSpec((None, d, fc), lambda i, e, c: (e, 0, c)),
                  pl.BlockSpec((None, d, fc), lambda i, e, c: (e, 0, c + nch)),
                  pl.BlockSpec((None, fc, d), lambda i, e, c: (e, c, 0)),
                  pl.BlockSpec((1, d), lambda i, e, c: (0, 0)),
                  pl.BlockSpec((1, d), lambda i, e, c: (0, 0))],
        out_specs=pl.BlockSpec((tm, d), lambda i, e, c: (i, 0)),
        out_shape=jax.ShapeDtypeStruct((n, d), F32),
        scratch_shapes=[pltpu.VMEM((tm, d), F32), pltpu.VMEM((tm, LANES), F32)],
        compiler_params=pltpu.CompilerParams(dimension_semantics=("arbitrary", "arbitrary", "arbitrary"),
                                             vmem_limit_bytes=VMEM_LIMIT),
        name="moe_ln",
    )(x, wr_hi, wr_lo, w_gu, w_gu, w_down, g, b)


def _flatten_kernel(*refs, dims):
    in_ref, o_ref = refs[-2], refs[-1]
    if len(dims) == 1:
        o_ref[...] = in_ref[...].astype(o_ref.dtype)
        return
    a, b, c = dims
    for i in range(a):
        for j in range(b):
            k = i * b + j
            o_ref[:, k * c:(k + 1) * c] = in_ref[:, i, j, :].astype(o_ref.dtype)


def _gather_pages(pool, pt_flat, layer, n_seq, n_pages, out_dtype):
    dims = pool.shape[3:]
    width = math.prod(dims)
    zeros = (0,) * len(dims)
    return pl.pallas_call(
        functools.partial(_flatten_kernel, dims=dims),
        grid_spec=pltpu.PrefetchScalarGridSpec(
            num_scalar_prefetch=1, grid=(n_seq, n_pages),
            in_specs=[pl.BlockSpec((None, None, PAGE_SIZE) + dims,
                                   lambda s, p, pt: (pt[s * n_pages + p], layer, 0) + zeros)],
            out_specs=pl.BlockSpec((None, PAGE_SIZE, width), lambda s, p, pt: (s, p, 0))),
        out_shape=jax.ShapeDtypeStruct((n_seq, n_pages * PAGE_SIZE, width), out_dtype),
        compiler_params=pltpu.CompilerParams(dimension_semantics=("arbitrary", "arbitrary")),
        name="gather_pages",
    )(pt_flat, pool)


def _flatten_state(state, layer, out_dtype):
    nb, _, rows = state.shape[:3]
    dims = state.shape[3:]
    width = math.prod(dims)
    zeros = (0,) * len(dims)
    return pl.pallas_call(
        functools.partial(_flatten_kernel, dims=dims),
        grid=(nb,),
        in_specs=[pl.BlockSpec((None, None, rows) + dims, lambda s: (s, layer, 0) + zeros)],
        out_specs=pl.BlockSpec((None, rows, width), lambda s: (s, 0, 0)),
        out_shape=jax.ShapeDtypeStruct((nb, rows, width), out_dtype),
        compiler_params=pltpu.CompilerParams(dimension_semantics=("arbitrary",)),
        name="flatten_state",
    )(state)


def _cum_tri(lf, carry):
    rows = lf.shape[0]
    tri = jnp.where(_iota((rows, rows), 0) >= _iota((rows, rows), 1), 1.0, 0.0).astype(BF16)
    hi, mid, lo = _split3(lf)
    return _dot(tri, hi.astype(BF16)) + _dot(tri, mid.astype(BF16)) + _dot(tri, lo.astype(BF16)) + carry


def _fox_aug_cols(cum3, hh):
    chi, cmid, clo = [part[:, MISC_FF + hh:MISC_FF + hh + 1] for part in cum3]
    l64 = _iota((1, HEAD_DIM), 1)
    qx = jnp.where(l64 == 0, chi, jnp.where(l64 == 1, cmid, jnp.where(l64 == 2, clo, jnp.where(l64 < 6, 1.0, 0.0))))
    kx = jnp.where(l64 < 3, 1.0,
                   jnp.where(l64 == 3, -chi, jnp.where(l64 == 4, -cmid, jnp.where(l64 == 5, -clo, 0.0))))
    return qx, kx


def _fox_gather_kernel(pt_ref, kv_ref, lf_ref, ka_ref, v_ref, tot_ref, carry_ref, pad_ref):
    p = pl.program_id(1)

    @pl.when(p == 0)
    def _():
        carry_ref[...] = jnp.zeros_like(carry_ref)

    pad_ref[...] = jnp.zeros_like(pad_ref)
    pad_ref[:, MISC_FF:MISC_FF + FOX_HEADS] = lf_ref[...]
    cum = _cum_tri(pad_ref[...], carry_ref[...])
    last = cum[PAGE_SIZE - 1:PAGE_SIZE, :]
    carry_ref[...] = last
    tot_ref[...] = last
    cum3 = _split3(cum)
    for hh in range(FOX_HEADS):
        _, kx = _fox_aug_cols(cum3, hh)
        ka_ref[:, hh * 128:hh * 128 + 64] = kv_ref[:, 0, hh, :].astype(BF16)
        ka_ref[:, hh * 128 + 64:(hh + 1) * 128] = kx.astype(BF16)
        v_ref[:, hh * 64:(hh + 1) * 64] = kv_ref[:, 1, hh, :].astype(BF16)


def _fox_gather(cache_kv, cache_lf, pt_flat, layer, n_seq, n_pages):
    lp = n_pages * PAGE_SIZE
    return pl.pallas_call(
        _fox_gather_kernel,
        grid_spec=pltpu.PrefetchScalarGridSpec(
            num_scalar_prefetch=1, grid=(n_seq, n_pages),
            in_specs=[pl.BlockSpec((None, None, PAGE_SIZE, 2, FOX_HEADS, HEAD_DIM),
                                   lambda s, p, pt: (pt[s * n_pages + p], layer, 0, 0, 0, 0)),
                      pl.BlockSpec((None, None, PAGE_SIZE, FOX_HEADS),
                                   lambda s, p, pt: (pt[s * n_pages + p], layer, 0, 0))],
            out_specs=[pl.BlockSpec((None, PAGE_SIZE, FOX_HEADS * 128), lambda s, p, pt: (s, p, 0)),
                       pl.BlockSpec((None, PAGE_SIZE, FOX_HEADS * 64), lambda s, p, pt: (s, p, 0)),
                       pl.BlockSpec((None, 1, LANES), lambda s, p, pt: (s, 0, 0))],
            scratch_shapes=[pltpu.VMEM((1, LANES), F32), pltpu.VMEM((PAGE_SIZE, LANES), F32)]),
        out_shape=[jax.ShapeDtypeStruct((n_seq, lp, FOX_HEADS * 128), BF16),
                   jax.ShapeDtypeStruct((n_seq, lp, FOX_HEADS * 64), BF16),
                   jax.ShapeDtypeStruct((n_seq, 1, LANES), F32)],
        compiler_params=pltpu.CompilerParams(dimension_semantics=("arbitrary", "arbitrary")),
        name="fox_gather",
    )(pt_flat, cache_kv, cache_lf)


def _fox_prep_kernel(fq_ref, fk_ref, misc_ref, fb_ref, cin_ref, qa_ref, ka_ref, lf_ref, carry_ref, *, rows):
    j = pl.program_id(1)

    @pl.when(j == 0)
    def _():
        carry_ref[...] = cin_ref[...]

    lane = _iota((1, LANES), 1)
    z = misc_ref[...] + fb_ref[...]
    lf = jnp.minimum(z, 0.0) - jnp.log1p(jnp.exp(-jnp.abs(z)))
    lf = jnp.where((lane >= MISC_FF) & (lane < MISC_FF + FOX_HEADS), lf, 0.0)
    lf_ref[...] = lf
    if rows >= 128:
        cum = _cum_tri(lf, carry_ref[...])
    else:
        ri = _iota((rows, LANES), 0)
        cum = jnp.zeros((rows, LANES), F32) + carry_ref[...]
        for t in range(rows):
            cum = cum + jnp.where(ri >= t, lf[t:t + 1, :], 0.0)
    carry_ref[...] = cum[rows - 1:rows, :]
    cum3 = _split3(cum)
    for hh in range(FOX_HEADS):
        qx, kx = _fox_aug_cols(cum3, hh)
        qa_ref[:, hh * 128:hh * 128 + 64] = (fq_ref[:, hh * 64:(hh + 1) * 64] * 0.125).astype(qa_ref.dtype)
        qa_ref[:, hh * 128 + 64:(hh + 1) * 128] = qx.astype(qa_ref.dtype)
        ka_ref[:, hh * 128:hh * 128 + 64] = fk_ref[:, hh * 64:(hh + 1) * 64].astype(ka_ref.dtype)
        ka_ref[:, hh * 128 + 64:(hh + 1) * 128] = kx.astype(ka_ref.dtype)


def _fox_prep(h, fb, cin, *, row0, n_seq, t, rows, out_dtype):
    nr = t // rows
    rb0 = row0 // rows
    n = n_seq * t

    def rowblk(s, j):
        return rb0 + s * nr + j

    return pl.pallas_call(
        functools.partial(_fox_prep_kernel, rows=rows),
        grid=(n_seq, nr),
        in_specs=[pl.BlockSpec((rows, 512), lambda s, j: (rowblk(s, j), E_FQ // 512)),
                  pl.BlockSpec((rows, 512), lambda s, j: (rowblk(s, j), E_FK // 512)),
                  pl.BlockSpec((rows, LANES), lambda s, j: (rowblk(s, j), E_MISC // LANES)),
                  pl.BlockSpec((1, LANES), lambda s, j: (0, 0)),
                  pl.BlockSpec((None, 1, LANES), lambda s, j: (s, 0, 0))],
        out_specs=[pl.BlockSpec((rows, FOX_HEADS * 128), lambda s, j: (s * nr + j, 0)),
                   pl.BlockSpec((rows, FOX_HEADS * 128), lambda s, j: (s * nr + j, 0)),
                   pl.BlockSpec((rows, LANES), lambda s, j: (s * nr + j, 0))],
        out_shape=[jax.ShapeDtypeStruct((n, FOX_HEADS * 128), out_dtype),
                   jax.ShapeDtypeStruct((n, FOX_HEADS * 128), out_dtype),
                   jax.ShapeDtypeStruct((n, LANES), F32)],
        scratch_shapes=[pltpu.VMEM((1, LANES), F32)],
        compiler_params=pltpu.CompilerParams(dimension_semantics=("arbitrary", "arbitrary")),
        name="fox_prep",
    )(h, h, h, fb, cin)


def _key_chunks(total, ch):
    return [(r0, min(ch, total - r0)) for r0 in range(0, total, ch)]


def _fox_attn_kernel(*refs, tq, t_new, q_off, l_past, ch):
    if l_past:
        qa_ref, ka_ref, v_ref, kap_ref, vp_ref, o_ref = refs
    else:
        qa_ref, ka_ref, v_ref, o_ref = refs
    i = pl.program_id(1)
    qpos = q_off + i * tq + _iota((tq, 1), 0)
    segs = []
    if l_past:
        segs += [(kap_ref, vp_ref, r0, n, r0) for r0, n in _key_chunks(l_past, ch)]
    segs += [(ka_ref, v_ref, r0, n, l_past + r0) for r0, n in _key_chunks(t_new, ch)]
    for hh in range(2):
        q = qa_ref[:, hh * 128:(hh + 1) * 128].astype(BF16)
        st = _softmax_init(tq)
        for kref, vref, r0, n, pos0 in segs:
            npad = _round_up(n, 128)
            k = _pad_rows(kref[r0:r0 + n, hh * 128:(hh + 1) * 128].astype(BF16), npad)
            v = _pad_rows(vref[r0:r0 + n, hh * 64:(hh + 1) * 64].astype(BF16), npad)
            kpos = pos0 + _iota((1, npad), 1)
            ok = (kpos <= qpos) & (kpos < pos0 + n)
            s = jnp.where(ok, _dot_nt(q, k), NEG_INF)
            st = _softmax_update(st, s, v)
        o_ref[:, hh * 64:(hh + 1) * 64] = _softmax_finish(st)


def _fox_attn(qa, ka, h, past, *, row0, n_seq, t, tq, q_off, l_past):
    nq = t // tq
    n = n_seq * t
    hp = FOX_HEADS // 2
    in_specs = [pl.BlockSpec((tq, 256), lambda s, i, p: (s * nq + i, p)),
                pl.BlockSpec((t, 256), lambda s, i, p: (s, p)),
                pl.BlockSpec((t, LANES), lambda s, i, p: (row0 // t + s, E_FV // LANES + p))]
    args = [qa, ka, h]
    if l_past:
        in_specs += [pl.BlockSpec((None, l_past, 256), lambda s, i, p: (s, 0, p)),
                     pl.BlockSpec((None, l_past, LANES), lambda s, i, p: (s, 0, p))]
        args += list(past)
    return pl.pallas_call(
        functools.partial(_fox_attn_kernel, tq=tq, t_new=t, q_off=q_off, l_past=l_past, ch=512),
        grid=(n_seq, nq, hp),
        in_specs=in_specs,
        out_specs=pl.BlockSpec((tq, LANES), lambda s, i, p: (s * nq + i, p)),
        out_shape=jax.ShapeDtypeStruct((n, FOX_HEADS * HEAD_DIM), F32),
        compiler_params=pltpu.CompilerParams(dimension_semantics=("arbitrary", "arbitrary", "arbitrary"),
                                             vmem_limit_bytes=VMEM_LIMIT),
        name="fox_attn",
    )(*args)


SEL_W = 64


def _nsa_kernel(*refs, tq, t_new, q_off, l_past, w_past, nc, ns, topn, ch):
    if l_past:
        (hq_ref, misc_ref, slc_ref, win_ref, kc_ref, vc_ref, slcp_ref, winp_ref,
         pe_ref, phi_ref, crt_ref, o_ref, ck_ref, cv_ref) = refs
    else:
        (hq_ref, misc_ref, kc_ref, vc_ref, slc_ref, win_ref,
         pe_ref, phi_ref, crt_ref, o_ref, ck_ref, cv_ref) = refs
    i = pl.program_id(1)

    @pl.when(i == 0)
    def _():
        acc_k = jnp.zeros((nc, LANES), F32)
        acc_v = jnp.zeros((nc, LANES), F32)
        for l in range(CMP_BLOCK):
            rk = kc_ref[pl.ds(l, nc, stride=CMP_BLOCK), :] + pe_ref[0, l:l + 1, :]
            acc_k = acc_k + _dot(rk.astype(BF16), phi_ref[0, l])
            rv = vc_ref[pl.ds(l, nc, stride=CMP_BLOCK), :] + pe_ref[1, l:l + 1, :]
            acc_v = acc_v + _dot(rv.astype(BF16), phi_ref[1, l])
        ck_ref[...] = jnp.zeros_like(ck_ref)
        cv_ref[...] = jnp.zeros_like(cv_ref)
        ck_ref[0:nc, :] = _rope128(acc_k, crt_ref[:, 0:128], crt_ref[:, 128:256], crt_ref[:, 256:384])
        cv_ref[0:nc, :] = acc_v

    qs = i * tq
    qpos = q_off + qs + _iota((tq, 1), 0)
    qpos4 = jnp.concatenate([qpos] * NSA_GROUP, axis=0)
    gates = jax.nn.sigmoid(misc_ref[...])
    blk = _iota((1, SEL_W), 1)
    n_even, n_odd = (nc + 1) // 2, nc // 2

    slc_segs, win_segs = [], []
    if l_past:
        slc_segs += [(slcp_ref, r0, n, r0) for r0, n in _key_chunks(l_past, ch)]
        win_segs += [(winp_ref, 0, w_past, l_past - w_past)]
        slc_segs += [(slc_ref, 0, t_new, l_past)]
        win_segs += [(win_ref, 0, t_new, l_past)]
    else:
        slc_segs += [(slc_ref, r0, n, r0) for r0, n in _key_chunks(t_new, ch)]
        n_win = min(WINDOW + tq, t_new)
        start = pl.multiple_of(jnp.maximum(qs + tq - n_win, 0), 8)
        win_segs += [(win_ref, start, n_win, start)]

    for g in range(NSA_KV_HEADS):
        kcol = slice(g * 64, (g + 1) * 64)
        vcol = slice(128 + g * 64, 128 + (g + 1) * 64)
        q4 = jnp.concatenate([hq_ref[:, (NSA_GROUP * g + r) * 64:(NSA_GROUP * g + r + 1) * 64]
                              for r in range(NSA_GROUP)], axis=0)
        q4 = (q4 * 0.125).astype(BF16)

        ck_e = ck_ref[pl.ds(0, SEL_W, stride=2), :][:, kcol].astype(BF16)
        ck_o = ck_ref[pl.ds(1, SEL_W, stride=2), :][:, kcol].astype(BF16)
        cv_e = cv_ref[pl.ds(0, SEL_W, stride=2), :][:, kcol].astype(BF16)
        cv_o = cv_ref[pl.ds(1, SEL_W, stride=2), :][:, kcol].astype(BF16)
        ok_e = ((2 * blk + 1) * CMP_BLOCK - 1 <= qpos4) & (blk < n_even)
        ok_o = ((2 * blk + 2) * CMP_BLOCK - 1 <= qpos4) & (blk < n_odd)
        s_e = jnp.where(ok_e, _dot_nt(q4, ck_e), NEG_INF)
        s_o = jnp.where(ok_o, _dot_nt(q4, ck_o), NEG_INF)
        m = jnp.maximum(jnp.max(s_e, axis=-1, keepdims=True), jnp.max(s_o, axis=-1, keepdims=True))
        m = jnp.where(m == NEG_INF, 0.0, m)
        p_e = jnp.exp(s_e - m)
        p_o = jnp.exp(s_o - m)
        den = jnp.sum(p_e, axis=-1, keepdims=True) + jnp.sum(p_o, axis=-1, keepdims=True)
        inv = 1.0 / jnp.maximum(den, 1e-30)
        p_e = p_e * inv
        p_o = p_o * inv
        o_cmp = _dot(p_e.astype(BF16), cv_e) + _dot(p_o.astype(BF16), cv_o)
        pp = p_e + p_o
        imp = pp[0:tq]
        for r in range(1, NSA_GROUP):
            imp = imp + pp[r * tq:(r + 1) * tq]

        valid = blk * SLC_BLOCK <= qpos
        forced = (blk == 0) | (blk == (qpos >> 6))
        score = jnp.where(forced, jnp.inf, jnp.where(valid, imp, NEG_INF))
        score = jnp.where(blk < ns, score, NEG_INF)
        rank = jnp.zeros((tq, SEL_W), F32)
        for b2 in range(ns):
            col = score[:, b2:b2 + 1]
            beats = (col > score) | ((col == score) & (blk > b2))
            rank = rank + jnp.where(beats, 1.0, 0.0)
        sel = jnp.where(rank < topn, 1.0, 0.0).astype(BF16)

        st = _softmax_init(NSA_GROUP * tq)
        for ref, r0, n, pos0 in slc_segs:
            npad = _round_up(n, 128)
            k = _pad_rows(ref[r0:r0 + n, kcol].astype(BF16), npad)
            v = _pad_rows(ref[r0:r0 + n, vcol].astype(BF16), npad)
            kpos = pos0 + _iota((1, npad), 1)
            expand = jnp.where(_iota((SEL_W, npad), 0) == ((pos0 + _iota((SEL_W, npad), 1)) >> 6), 1.0, 0.0)
            selx = _dot(sel, expand.astype(BF16))
            ok = (selx > 0.5) & (kpos <= qpos) & (kpos < pos0 + n)
            bias = jnp.where(ok, 0.0, NEG_INF)
            s = _dot_nt(q4, k) + jnp.concatenate([bias] * NSA_GROUP, axis=0)
            st = _softmax_update(st, s, v)
        o_slc = _softmax_finish(st)

        st = _softmax_init(NSA_GROUP * tq)
        for ref, r0, n, pos0 in win_segs:
            npad = _round_up(n, 128)
            k = _pad_rows(ref[pl.ds(r0, n), kcol].astype(BF16), npad)
            v = _pad_rows(ref[pl.ds(r0, n), vcol].astype(BF16), npad)
            kpos = pos0 + _iota((1, npad), 1)
            ok = (kpos <= qpos) & (kpos > qpos - WINDOW) & (kpos < pos0 + n)
            bias = jnp.where(ok, 0.0, NEG_INF)
            s = _dot_nt(q4, k) + jnp.concatenate([bias] * NSA_GROUP, axis=0)
            st = _softmax_update(st, s, v)
        o_win = _softmax_finish(st)

        for r in range(NSA_GROUP):
            hh = NSA_GROUP * g + r
            rs = slice(r * tq, (r + 1) * tq)
            gc = gates[:, MISC_GATES + 3 * hh:MISC_GATES + 3 * hh + 1]
            gs = gates[:, MISC_GATES + 3 * hh + 1:MISC_GATES + 3 * hh + 2]
            gw = gates[:, MISC_GATES + 3 * hh + 2:MISC_GATES + 3 * hh + 3]
            o_ref[:, hh * 64:(hh + 1) * 64] = gc * o_cmp[rs] + gs * o_slc[rs] + gw * o_win[rs]


def _nsa_attn(h, past, pe, phi, crt, *, row0, n_seq, t, tq, q_off, l_past, w_past, topn):
    nq = t // tq
    n = n_seq * t
    l_tot = l_past + t
    nc = l_tot // CMP_BLOCK
    ns = -(-l_tot // SLC_BLOCK)
    assert ns <= SEL_W and nc <= 2 * SEL_W
    assert (l_past % CMP_BLOCK == 0 and t < CMP_BLOCK) if l_past else True
    rq0 = row0 // tq
    rb0 = row0 // t

    def qblk(col):
        return lambda s, i: (rq0 + s * nq + i, col)

    def kblk(col):
        return lambda s, i: (rb0 + s, col)

    in_specs = [pl.BlockSpec((tq, 512), qblk(E_NQ // 512)),
                pl.BlockSpec((tq, LANES), qblk(E_MISC // LANES))]
    args = [h, h]
    if not l_past:
        in_specs += [pl.BlockSpec((t, LANES), kblk(E_CMP // LANES)),
                     pl.BlockSpec((t, LANES), kblk(E_CMP // LANES + 1))]
        args += [h, h]
    in_specs += [pl.BlockSpec((t, 256), kblk(E_SLC // 256)),
                 pl.BlockSpec((t, 256), kblk(E_WIN // 256))]
    args += [h, h]
    if l_past:
        cmp_past, slc_past, win_past = past
        in_specs += [pl.BlockSpec((None, l_past, LANES), lambda s, i: (s, 0, 0)),
                     pl.BlockSpec((None, l_past, LANES), lambda s, i: (s, 0, 1)),
                     pl.BlockSpec((None, l_past, 256), lambda s, i: (s, 0, 0)),
                     pl.BlockSpec((None, w_past, 256), lambda s, i: (s, 0, 0))]
        args += [cmp_past, cmp_past, slc_past, win_past]
    in_specs += [pl.BlockSpec(pe.shape, lambda s, i: (0, 0, 0)),
                 pl.BlockSpec(phi.shape, lambda s, i: (0, 0, 0, 0)),
                 pl.BlockSpec(crt.shape, lambda s, i: (0, 0))]
    args += [pe, phi, crt]
    return pl.pallas_call(
        functools.partial(_nsa_kernel, tq=tq, t_new=t, q_off=q_off, l_past=l_past, w_past=w_past,
                          nc=nc, ns=ns, topn=min(topn, ns), ch=512),
        grid=(n_seq, nq),
        in_specs=in_specs,
        out_specs=pl.BlockSpec((tq, NSA_HEADS * HEAD_DIM), lambda s, i: (s * nq + i, 0)),
        out_shape=jax.ShapeDtypeStruct((n, NSA_HEADS * HEAD_DIM), F32),
        scratch_shapes=[pltpu.VMEM((2 * SEL_W, LANES), F32), pltpu.VMEM((2 * SEL_W, LANES), F32)],
        compiler_params=pltpu.CompilerParams(dimension_semantics=("arbitrary", "arbitrary"),
                                             vmem_limit_bytes=VMEM_LIMIT),
        name="nsa_attn",
    )(*args)


def _dsa_kernel(*refs, tq, t_new, q_off, l_past, n_keep, ch):
    if l_past:
        q_ref, qi_ref, kw_ref, kv_ref, ki_ref, kvp_ref, kip_ref, o_ref, sc_ref, key_ref = refs
    else:
        q_ref, qi_ref, kw_ref, kv_ref, ki_ref, o_ref, sc_ref, key_ref = refs
    i = pl.program_id(1)
    qpos = q_off + i * tq + _iota((tq, 1), 0)
    l_pad = sc_ref.shape[1]

    segs = []
    off = 0
    if l_past:
        for r0, n in _key_chunks(l_past, ch):
            segs.append((kvp_ref, kip_ref, r0, n, r0, off))
            off += _round_up(n, 128)
    for r0, n in _key_chunks(t_new, ch):
        segs.append((kv_ref, ki_ref, r0, n, l_past + r0, off))
        off += _round_up(n, 128)
    assert off == l_pad

    wi = kw_ref[:, IDX_DIM:IDX_DIM + IDX_HEADS] * (IDX_HEADS ** -0.5)
    qi = [(qi_ref[:, hh * IDX_DIM:(hh + 1) * IDX_DIM] * (IDX_DIM ** -0.5)).astype(BF16) for hh in range(IDX_HEADS)]
    for _, kiref, r0, n, pos0, o0 in segs:
        npad = _round_up(n, 128)
        ki = _pad_rows(kiref[r0:r0 + n, 0:IDX_DIM].astype(BF16), npad)
        sc = jnp.zeros((tq, npad), F32)
        for hh in range(IDX_HEADS):
            sc = sc + wi[:, hh:hh + 1] * jnp.maximum(_dot_nt(qi[hh], ki), 0.0)
        kpos = pos0 + _iota((1, npad), 1)
        ok = (kpos <= qpos) & (kpos < pos0 + n)
        sc_ref[:, o0:o0 + npad] = jnp.where(ok, sc + 0.0, NEG_INF)

    bits = pltpu.bitcast(sc_ref[...], jnp.int32)
    key_ref[...] = bits ^ ((bits >> 31) & jnp.int32(0x7FFFFFFF))
    kk = float(n_keep)

    def count(mask):
        return jnp.sum(jnp.where(mask, 1.0, 0.0), axis=-1, keepdims=True)

    thr = jnp.where(count(key_ref[...] >= 0) >= kk, jnp.int32(0), jnp.int32(-2147483648))
    thr = thr + jnp.zeros((tq, 1), jnp.int32)

    def descend(b, thr):
        cand = thr + (jnp.int32(1) << (30 - b))
        return jnp.where(count(key_ref[...] >= cand) >= kk, cand, thr)

    thr = lax.fori_loop(0, 31, descend, thr)
    ge = key_ref[...] >= thr
    has_ties = jnp.max(count(ge)) > kk

    @pl.when(jnp.logical_not(has_ties))
    def _():
        sc_ref[...] = jnp.where(ge, 0.0, NEG_INF)

    @pl.when(has_ties)
    def _():
        keys = key_ref[...]
        gt = keys > thr
        eq = keys == thr
        need = kk - count(gt)
        idx = _iota((1, l_pad), 1)

        def widen(b, bound):
            cand = bound + (jnp.int32(1) << (15 - b))
            return jnp.where(count(eq & (idx < cand)) <= need, cand, bound)

        bound = lax.fori_loop(0, 16, widen, jnp.zeros((tq, 1), jnp.int32))
        sc_ref[...] = jnp.where(gt | (eq & (idx < bound)), 0.0, NEG_INF)

    q4s = []
    for kh in range(DSA_KV_HEADS):
        q4 = jnp.concatenate([q_ref[:, (DSA_GROUP * kh + r) * 64:(DSA_GROUP * kh + r + 1) * 64]
                              for r in range(DSA_GROUP)], axis=0)
        q4s.append((q4 * 0.125).astype(BF16))
    states = [_softmax_init(DSA_GROUP * tq) for _ in range(DSA_KV_HEADS)]
    for kvref, _, r0, n, pos0, o0 in segs:
        npad = _round_up(n, 128)
        kpos = pos0 + _iota((1, npad), 1)
        ok = (kpos <= qpos) & (kpos < pos0 + n)
        bias = jnp.where(ok, sc_ref[:, o0:o0 + npad], NEG_INF)
        bias4 = jnp.concatenate([bias] * DSA_GROUP, axis=0)
        for kh in range(DSA_KV_HEADS):
            k = _pad_rows(kvref[r0:r0 + n, kh * 64:(kh + 1) * 64].astype(BF16), npad)
            v = _pad_rows(kvref[r0:r0 + n, 256 + kh * 64:256 + (kh + 1) * 64].astype(BF16), npad)
            states[kh] = _softmax_update(states[kh], _dot_nt(q4s[kh], k) + bias4, v)
    for kh in range(DSA_KV_HEADS):
        o = _softmax_finish(states[kh])
        for r in range(DSA_GROUP):
            hh = DSA_GROUP * kh + r
            o_ref[:, hh * 64:(hh + 1) * 64] = o[r * tq:(r + 1) * tq]


def _dsa_attn(h, past, *, row0, n_seq, t, tq, q_off, l_past, n_keep):
    nq = t // tq
    n = n_seq * t
    ch = 512
    l_pad = sum(_round_up(c, 128) for _, c in _key_chunks(l_past, ch)) if l_past else 0
    l_pad += sum(_round_up(c, 128) for _, c in _key_chunks(t, ch))
    assert l_pad <= 1 << 16
    rq0 = row0 // tq
    rb0 = row0 // t

    def qblk(w, col):
        return pl.BlockSpec((tq, w), lambda s, i: (rq0 + s * nq + i, col))

    in_specs = [qblk(1024, O_Q // 1024), qblk(512, O_QI // 512), qblk(LANES, O_KI // LANES),
                pl.BlockSpec((t, 512), lambda s, i: (rb0 + s, O_KV // 512)),
                pl.BlockSpec((t, LANES), lambda s, i: (rb0 + s, O_KI // LANES))]
    args = [h, h, h, h, h]
    if l_past:
        kv_past, ki_past = past
        in_specs += [pl.BlockSpec((None, l_past, 512), lambda s, i: (s, 0, 0)),
                     pl.BlockSpec((None, l_past, IDX_DIM), lambda s, i: (s, 0, 0))]
        args += [kv_past, ki_past]
    return pl.pallas_call(
        functools.partial(_dsa_kernel, tq=tq, t_new=t, q_off=q_off, l_past=l_past, n_keep=n_keep, ch=ch),
        grid=(n_seq, nq),
        in_specs=in_specs,
        out_specs=pl.BlockSpec((tq, DSA_HEADS * HEAD_DIM), lambda s, i: (s * nq + i, 0)),
        out_shape=jax.ShapeDtypeStruct((n, DSA_HEADS * HEAD_DIM), F32),
        scratch_shapes=[pltpu.VMEM((tq, l_pad), F32), pltpu.VMEM((tq, l_pad), jnp.int32)],
        compiler_params=pltpu.CompilerParams(dimension_semantics=("arbitrary", "arbitrary"),
                                             vmem_limit_bytes=VMEM_LIMIT),
        name="dsa_attn",
    )(*args)


def _rope_table(pos):
    half = ROT_DIM // 2
    inv = ROPE_THETA ** (-2.0 * jnp.arange(half, dtype=F32) / ROT_DIM)
    ang = pos.astype(F32)[:, None] * inv[None, :]
    cos, sin = jnp.cos(ang), jnp.sin(ang)
    n = pos.shape[0]
    one = jnp.ones((n, HEAD_DIM - ROT_DIM), F32)
    zero = jnp.zeros((n, HEAD_DIM - ROT_DIM), F32)
    z8 = jnp.zeros((n, half), F32)
    c = jnp.concatenate([cos, cos, one], axis=1)
    s1 = jnp.concatenate([z8, sin, zero], axis=1)
    s2 = jnp.concatenate([-sin, z8, zero], axis=1)
    return jnp.concatenate([c, c, s1, s1, s2, s2], axis=1)


def _largest_tile(limit, *sizes):
    t = limit
    while any(s % t for s in sizes):
        t //= 2
    return t


def kernel(x_prompt, x_sample, cache_nsa_cmp_kv, cache_nsa_slc_kv, state_nsa_win_kv, cache_fox_kv, cache_fox_logf, cache_dsa_kv, cache_dsa_idx_k, page_table, ln_g, ln_b, w_in_even, w_out_even, fox_f_bias, nsa_cmp_pos, nsa_cmp_phi, w_in_odd, w_out_odd, ffn_gu, ffn_down, moe_router, moe_gu, moe_down):
    nb, t, d = x_prompt.shape
    db, dt, _ = x_sample.shape
    n_pages = page_table.shape[1]
    lp = n_pages * PAGE_SIZE
    w_past = state_nsa_win_kv.shape[2]
    n_p, n_s = nb * t, db * dt
    depth = ln_g.shape[0]
    alpha = (2 * depth) ** 0.25
    tm = _largest_tile(512, n_p, n_s)
    tm_ff = _largest_tile(512, n_p, n_s)
    tq = _largest_tile(128, t)
    pt_flat = page_table.reshape(-1).astype(jnp.int32)

    x = jnp.concatenate([x_prompt.reshape(n_p, d), x_sample.reshape(n_s, d)], axis=0)
    pos = jnp.concatenate([jnp.tile(jnp.arange(t), nb), jnp.tile(lp + jnp.arange(dt), db)])
    rt = _rope_table(pos)
    nc_p, nc_s = t // CMP_BLOCK, (lp + dt) // CMP_BLOCK
    crt_p = _rope_table((jnp.arange(nc_p) + 1) * CMP_BLOCK - 1)
    crt_s = _rope_table((jnp.arange(nc_s) + 1) * CMP_BLOCK - 1)
    zero_carry = jnp.zeros((nb, 1, LANES), F32)

    ev_p, ev_s, od_p, od_s = [], [], [], []
    for layer in range(depth):
        j = layer // 2
        g0, b0 = ln_g[layer, 0][None, :], ln_b[layer, 0][None, :]
        g1, b1 = ln_g[layer, 1][None, :], ln_b[layer, 1][None, :]
        if layer % 2 == 0:
            w = w_in_even[j]
            w = jnp.concatenate([w[:, 0:512], w[:, 1304:2840], w[:, 512:1280], w[:, 1280:1304], w[:, 2840:2848],
                                 jnp.zeros((d, E_W - 2848), w.dtype)], axis=1).astype(BF16)
            h = _proj(x, w, rt, width=E_W, rope_full=EVEN_ROPE, rope_lo=(), tm=tm)

            pe = jnp.concatenate([nsa_cmp_pos[j], nsa_cmp_pos[j]], axis=-1)
            ph = nsa_cmp_phi[j]
            zz = jnp.zeros_like(ph)
            phi = jnp.concatenate([jnp.concatenate([ph, zz], axis=-1),
                                   jnp.concatenate([zz, ph], axis=-1)], axis=-2).astype(BF16)
            fb = jnp.zeros((1, LANES), F32).at[0, MISC_FF:MISC_FF + FOX_HEADS].set(fox_f_bias[j])

            cmp_past = _gather_pages(cache_nsa_cmp_kv, pt_flat, j, db, n_pages, F32)
            slc_past = _gather_pages(cache_nsa_slc_kv, pt_flat, j, db, n_pages, BF16)
            win_past = _flatten_state(state_nsa_win_kv, j, F32)
            ka_past, v_past, tot = _fox_gather(cache_fox_kv, cache_fox_logf, pt_flat, j, db, n_pages)

            rows_p = _largest_tile(256, t)
            qa_p, ka_p, lf_p = _fox_prep(h, fb, zero_carry, row0=0, n_seq=nb, t=t, rows=rows_p, out_dtype=BF16)
            qa_s, ka_s, lf_s = _fox_prep(h, fb, tot, row0=n_p, n_seq=db, t=dt, rows=dt, out_dtype=F32)

            o_nsa_p = _nsa_attn(h, None, pe, phi, crt_p, row0=0, n_seq=nb, t=t, tq=tq, q_off=0,
                                l_past=0, w_past=0, topn=SLC_TOPN)
            o_nsa_s = _nsa_attn(h, (cmp_past, slc_past, win_past), pe, phi, crt_s, row0=n_p, n_seq=db, t=dt, tq=dt,
                                q_off=lp, l_past=lp, w_past=w_past, topn=SLC_TOPN)
            o_fox_p = _fox_attn(qa_p, ka_p, h, None, row0=0, n_seq=nb, t=t, tq=tq, q_off=0, l_past=0)
            o_fox_s = _fox_attn(qa_s, ka_s, h, (ka_past, v_past), row0=n_p, n_seq=db, t=dt, tq=dt, q_off=lp, l_past=lp)

            wo = w_out_even[j].astype(BF16)
            x = _out_ln([o_nsa_p, o_fox_p], [o_nsa_s, o_fox_s], [wo[0:512], wo[512:1024]], x, g0, b0,
                        alpha=alpha, tm=tm)
            x = _ffn_ln(x, ffn_gu[j].astype(BF16), ffn_down[j].astype(BF16), g1, b1, alpha=alpha, tm=tm_ff,
                        fc=ffn_down.shape[1] // 2)

            hp, hs = h[:n_p].reshape(nb, t, E_W), h[n_p:].reshape(db, dt, E_W)
            win_new = hs[:, :, E_WIN:E_WIN + 256]
            win_all = jnp.concatenate([win_past, win_new], axis=1)
            keep_p, keep_s = min(WINDOW, t), min(WINDOW, w_past + dt)
            lf_p3 = lf_p[:, MISC_FF:MISC_FF + FOX_HEADS].reshape(nb, t, FOX_HEADS)
            lf_s3 = lf_s[:, MISC_FF:MISC_FF + FOX_HEADS].reshape(db, dt, FOX_HEADS)
            ev_p.append((hp[:, :, E_CMP:E_CMP + 256], hp[:, :, E_SLC:E_SLC + 256],
                         hp[:, t - keep_p:, E_WIN:E_WIN + 256], hp[:, :, E_FK:E_FK + 1024], lf_p3))
            ev_s.append((hs[:, :, E_CMP:E_CMP + 256], hs[:, :, E_SLC:E_SLC + 256],
                         win_all[:, w_past + dt - keep_s:], hs[:, :, E_FK:E_FK + 1024], lf_s3))
        else:
            w = jnp.concatenate([w_in_odd[j], jnp.zeros((d, O_W - w_in_odd.shape[2]), w_in_odd.dtype)],
                                axis=1).astype(BF16)
            h = _proj(x, w, rt, width=O_W, rope_full=ODD_ROPE, rope_lo=ODD_ROPE_LO, tm=tm)
            kv_past = _gather_pages(cache_dsa_kv, pt_flat, j, db, n_pages, BF16)
            ki_past = _gather_pages(cache_dsa_idx_k, pt_flat, j, db, n_pages, BF16)
            o_p = _dsa_attn(h, None, row0=0, n_seq=nb, t=t, tq=tq, q_off=0, l_past=0,
                            n_keep=min(IDX_TOPK, t // 4))
            o_s = _dsa_attn(h, (kv_past, ki_past), row0=n_p, n_seq=db, t=dt, tq=dt, q_off=lp, l_past=lp,
                            n_keep=min(IDX_TOPK, (lp + dt) // 4))
            x = _out_ln([o_p], [o_s], [w_out_odd[j].astype(BF16)], x, g0, b0, alpha=alpha, tm=tm)
            wr = jnp.concatenate([moe_router[j], jnp.zeros((d, LANES - N_EXPERTS), F32)], axis=1)
            wr_hi = wr.astype(BF16)
            wr_lo = (wr - wr_hi.astype(F32)).astype(BF16)
            x = _moe_ln(x, wr_hi, wr_lo, moe_gu[j].astype(BF16), moe_down[j].astype(BF16), g1, b1,
                        alpha=alpha, tm=tm_ff, fc=moe_down.shape[2] // 4)
            hp, hs = h[:n_p].reshape(nb, t, O_W), h[n_p:].reshape(db, dt, O_W)
            od_p.append((hp[:, :, O_KV:O_KV + 512], hp[:, :, O_KI:O_KI + IDX_DIM]))
            od_s.append((hs[:, :, O_KV:O_KV + 512], hs[:, :, O_KI:O_KI + IDX_DIM]))

    def stk(lst, idx, tail):
        a = jnp.stack([s[idx] for s in lst], axis=1)
        return a.reshape(a.shape[:3] + tail)

    kv2 = (2, NSA_KV_HEADS, HEAD_DIM)
    fkv = (2, FOX_HEADS, HEAD_DIM)
    dkv = (2, DSA_KV_HEADS, HEAD_DIM)
    return (x[:n_p].reshape(nb, t, d), x[n_p:].reshape(db, dt, d),
            stk(ev_p, 0, kv2), stk(ev_s, 0, kv2), stk(ev_p, 1, kv2), stk(ev_s, 1, kv2),
            stk(ev_p, 2, kv2), stk(ev_s, 2, kv2), stk(ev_p, 3, fkv), stk(ev_s, 3, fkv),
            stk(ev_p, 4, (FOX_HEADS,)), stk(ev_s, 4, (FOX_HEADS,)),
            stk(od_p, 0, dkv), stk(od_s, 0, dkv), stk(od_p, 1, (IDX_DIM,)), stk(od_s, 1, (IDX_DIM,)))
```

```python
import functools
import math

import jax
import jax.numpy as jnp
from jax import lax
from jax.experimental import pallas as pl
from jax.experimental.pallas import tpu as pltpu

F32 = jnp.float32
BF16 = jnp.bfloat16
NEG_INF = float("-inf")

HEAD_DIM = 64
ROT_DIM = HEAD_DIM // 4
ROPE_THETA = 500000.0
NSA_HEADS = 8
NSA_KV_HEADS = 2
NSA_GROUP = NSA_HEADS // NSA_KV_HEADS
CMP_BLOCK = 32
SLC_BLOCK = 64
SLC_TOPN = 16
WINDOW = 512
FOX_HEADS = 8
DSA_HEADS = 16
DSA_KV_HEADS = 4
DSA_GROUP = DSA_HEADS // DSA_KV_HEADS
IDX_HEADS = 8
IDX_DIM = 64
IDX_TOPK = 256
N_EXPERTS = 8
LN_EPS = 1e-5
PAGE_SIZE = 128
LANES = 128

E_NQ, E_FQ, E_FK, E_FV, E_CMP, E_SLC, E_WIN, E_MISC, E_W = 0, 512, 1024, 1536, 2048, 2304, 2560, 2816, 2944
MISC_GATES = 0
MISC_FF = 24
EVEN_ROPE = (0, 1, 2, 3, E_SLC // LANES, E_WIN // LANES)
O_Q, O_KV, O_QI, O_KI, O_W = 0, 1024, 1536, 2048, 2176
ODD_ROPE = tuple(range(0, 10)) + tuple(range(12, 16))
ODD_ROPE_LO = (O_KI // LANES,)

SEL_W = 64
KEY_CHUNK = 256
VMEM_LIMIT = 48 * 1024 * 1024


def _dot(a, b):
    return jnp.dot(a, b, preferred_element_type=F32)


def _dot_nt(a, b):
    return lax.dot_general(a, b, (((1,), (1,)), ((), ())), preferred_element_type=F32)


def _iota(shape, dim):
    return lax.broadcasted_iota(jnp.int32, shape, dim)


def _rope128(v, c, s1, s2):
    return v * c + pltpu.roll(v, 8, 1) * s1 + pltpu.roll(v, LANES - 8, 1) * s2


def _split3(x):
    hi = x.astype(BF16).astype(F32)
    r = x - hi
    mid = r.astype(BF16).astype(F32)
    lo = (r - mid).astype(BF16).astype(F32)
    return hi, mid, lo


def _st_reset(m_ref, l_ref, acc_ref):
    m_ref[...] = jnp.full(m_ref.shape, NEG_INF, F32)
    l_ref[...] = jnp.zeros(l_ref.shape, F32)
    acc_ref[...] = jnp.zeros(acc_ref.shape, F32)


def _st_update(m_ref, l_ref, acc_ref, s, pv):
    m_prev = m_ref[...]
    m_new = jnp.maximum(m_prev, jnp.max(s, axis=-1, keepdims=True))
    m_safe = jnp.where(m_new == NEG_INF, 0.0, m_new)
    alpha = jnp.exp(m_prev - m_safe)
    p = jnp.exp(s - m_safe)
    l_ref[...] = alpha * l_ref[...] + jnp.sum(p, axis=-1, keepdims=True)
    acc_ref[...] = alpha * acc_ref[...] + pv(p.astype(BF16))
    m_ref[...] = m_new


def _st_finish(l_ref, acc_ref):
    return acc_ref[...] / jnp.maximum(l_ref[...], 1e-30)


def _pad_rows(a, rows):
    if a.shape[0] == rows:
        return a
    return jnp.concatenate([a, jnp.zeros((rows - a.shape[0], a.shape[1]), a.dtype)], axis=0)


def _tile_rows(a, times):
    return jnp.concatenate([a] * times, axis=0)


def _layer_norm(z, g, b):
    mu = jnp.mean(z, axis=-1, keepdims=True)
    d = z - mu
    var = jnp.mean(d * d, axis=-1, keepdims=True)
    return d * lax.rsqrt(var + LN_EPS) * g + b


def _key_chunks(total, ch):
    return [(r0, min(ch, total - r0)) for r0 in range(0, total, ch)]


def _proj_kernel(x_ref, w_ref, rt_ref, o_ref, *, width, rope_full, rope_lo, cw):
    xb = x_ref[...].astype(BF16)
    c, s1, s2 = rt_ref[:, 0:128], rt_ref[:, 128:256], rt_ref[:, 256:384]
    lo = _iota((1, LANES), 1) < HEAD_DIM
    for c0 in range(0, width, cw):
        c1 = min(c0 + cw, width)
        acc = _dot(xb, w_ref[:, c0:c1])
        for blk in range(c0 // LANES, c1 // LANES):
            v = acc[:, blk * LANES - c0:(blk + 1) * LANES - c0]
            if blk in rope_full:
                v = _rope128(v, c, s1, s2)
            elif blk in rope_lo:
                v = _rope128(v, jnp.where(lo, c, 1.0), jnp.where(lo, s1, 0.0), jnp.where(lo, s2, 0.0))
            o_ref[:, blk * LANES:(blk + 1) * LANES] = v


def _proj(x, w, rt, *, width, rope_full, rope_lo, tm):
    n, d = x.shape
    return pl.pallas_call(
        functools.partial(_proj_kernel, width=width, rope_full=rope_full, rope_lo=rope_lo, cw=512),
        grid=(n // tm,),
        in_specs=[pl.BlockSpec((tm, d), lambda i: (i, 0)),
                  pl.BlockSpec((d, width), lambda i: (0, 0)),
                  pl.BlockSpec((tm, 3 * LANES), lambda i: (i, 0))],
        out_specs=pl.BlockSpec((tm, width), lambda i: (i, 0)),
        out_shape=jax.ShapeDtypeStruct((n, width), F32),
        compiler_params=pltpu.CompilerParams(dimension_semantics=("arbitrary",), vmem_limit_bytes=VMEM_LIMIT),
        name="proj",
    )(x, w, rt)


def _out_ln_kernel(*refs, n_parts, n_prompt_tiles, alpha):
    op = refs[0:n_parts]
    os_ = refs[n_parts:2 * n_parts]
    ws = refs[2 * n_parts:3 * n_parts]
    x_ref, g_ref, b_ref, o_ref = refs[3 * n_parts:]
    i = pl.program_id(0)

    def compute(parts):
        y = None
        for o, w in zip(parts, ws):
            t = _dot(o[...].astype(BF16), w[...])
            y = t if y is None else y + t
        o_ref[...] = _layer_norm(alpha * x_ref[...] + y, g_ref[...], b_ref[...])

    @pl.when(i < n_prompt_tiles)
    def _():
        compute(op)

    @pl.when(i >= n_prompt_tiles)
    def _():
        compute(os_)


def _out_ln(o_prompt, o_sample, w_parts, x, g, b, *, alpha, tm):
    n, d = x.shape
    npt = o_prompt[0].shape[0] // tm
    nst = o_sample[0].shape[0] // tm
    k = len(w_parts)
    in_specs = []
    for o in o_prompt:
        in_specs.append(pl.BlockSpec((tm, o.shape[1]), lambda i: (jnp.minimum(i, npt - 1), 0)))
    for o in o_sample:
        in_specs.append(pl.BlockSpec((tm, o.shape[1]), lambda i: (jnp.maximum(i - npt, 0), 0)))
    for w in w_parts:
        in_specs.append(pl.BlockSpec(w.shape, lambda i: (0, 0)))
    in_specs += [pl.BlockSpec((tm, d), lambda i: (i, 0)),
                 pl.BlockSpec((1, d), lambda i: (0, 0)),
                 pl.BlockSpec((1, d), lambda i: (0, 0))]
    assert npt + nst == n // tm
    return pl.pallas_call(
        functools.partial(_out_ln_kernel, n_parts=k, n_prompt_tiles=npt, alpha=alpha),
        grid=(n // tm,),
        in_specs=in_specs,
        out_specs=pl.BlockSpec((tm, d), lambda i: (i, 0)),
        out_shape=jax.ShapeDtypeStruct((n, d), F32),
        compiler_params=pltpu.CompilerParams(dimension_semantics=("arbitrary",), vmem_limit_bytes=VMEM_LIMIT),
        name="out_ln",
    )(*o_prompt, *o_sample, *w_parts, x, g, b)


def _ffn_ln_kernel(x_ref, wa_ref, wg_ref, wd_ref, g_ref, b_ref, o_ref, acc_ref, *, alpha, n_chunks):
    c = pl.program_id(1)

    @pl.when(c == 0)
    def _():
        acc_ref[...] = jnp.zeros_like(acc_ref)

    xb = x_ref[...].astype(BF16)
    a = _dot(xb, wa_ref[...])
    gg = _dot(xb, wg_ref[...])
    hmid = (jax.nn.silu(a) * gg).astype(BF16)
    acc_ref[...] += _dot(hmid, wd_ref[...])

    @pl.when(c == n_chunks - 1)
    def _():
        o_ref[...] = _layer_norm(alpha * x_ref[...] + acc_ref[...], g_ref[...], b_ref[...])


def _ffn_ln(x, w_gu, w_down, g, b, *, alpha, tm, fc):
    n, d = x.shape
    dff = w_down.shape[0]
    nch = dff // fc
    return pl.pallas_call(
        functools.partial(_ffn_ln_kernel, alpha=alpha, n_chunks=nch),
        grid=(n // tm, nch),
        in_specs=[pl.BlockSpec((tm, d), lambda i, c: (i, 0)),
                  pl.BlockSpec((d, fc), lambda i, c: (0, c)),
                  pl.BlockSpec((d, fc), lambda i, c: (0, c + nch)),
                  pl.BlockSpec((fc, d), lambda i, c: (c, 0)),
                  pl.BlockSpec((1, d), lambda i, c: (0, 0)),
                  pl.BlockSpec((1, d), lambda i, c: (0, 0))],
        out_specs=pl.BlockSpec((tm, d), lambda i, c: (i, 0)),
        out_shape=jax.ShapeDtypeStruct((n, d), F32),
        scratch_shapes=[pltpu.VMEM((tm, d), F32)],
        compiler_params=pltpu.CompilerParams(dimension_semantics=("arbitrary", "arbitrary"),
                                             vmem_limit_bytes=VMEM_LIMIT),
        name="ffn_ln",
    )(x, w_gu, w_gu, w_down, g, b)


def _moe_ln_kernel(x_ref, wrh_ref, wrl_ref, wa_ref, wg_ref, wd_ref, g_ref, b_ref, o_ref, acc_ref, comb_ref,
                   *, alpha, n_chunks):
    e = pl.program_id(1)
    c = pl.program_id(2)
    lane = _iota((1, LANES), 1).astype(F32)

    @pl.when((e == 0) & (c == 0))
    def _():
        acc_ref[...] = jnp.zeros_like(acc_ref)
        x = x_ref[...]
        xh = x.astype(BF16)
        xl = (x - xh.astype(F32)).astype(BF16)
        logits = _dot(xh, wrh_ref[...]) + _dot(xl, wrh_ref[...]) + _dot(xh, wrl_ref[...])
        lg = jnp.where(lane < N_EXPERTS, logits, NEG_INF)
        m1 = jnp.max(lg, axis=-1, keepdims=True)
        i1 = jnp.min(jnp.where(lg == m1, lane, float(LANES)), axis=-1, keepdims=True)
        lg2 = jnp.where(lane == i1, NEG_INF, lg)
        m2 = jnp.max(lg2, axis=-1, keepdims=True)
        i2 = jnp.min(jnp.where(lg2 == m2, lane, float(LANES)), axis=-1, keepdims=True)
        e2 = jnp.exp(m2 - m1)
        den = 1.0 + e2
        comb_ref[...] = jnp.where(lane == i1, 1.0 / den, 0.0) + jnp.where(lane == i2, e2 / den, 0.0)

    xb = x_ref[...].astype(BF16)
    a = _dot(xb, wa_ref[...])
    gg = _dot(xb, wg_ref[...])
    hmid = (jax.nn.silu(a) * gg).astype(BF16)
    ce = jnp.sum(jnp.where(lane == e.astype(F32), comb_ref[...], 0.0), axis=-1, keepdims=True)
    acc_ref[...] += ce * _dot(hmid, wd_ref[...])

    @pl.when((e == N_EXPERTS - 1) & (c == n_chunks - 1))
    def _():
        o_ref[...] = _layer_norm(alpha * x_ref[...] + acc_ref[...], g_ref[...], b_ref[...])


def _moe_ln(x, wr_hi, wr_lo, w_gu, w_down, g, b, *, alpha, tm, fc):
    n, d = x.shape
    dff = w_down.shape[1]
    nch = dff // fc
    return pl.pallas_call(
        functools.partial(_moe_ln_kernel, alpha=alpha, n_chunks=nch),
        grid=(n // tm, N_EXPERTS, nch),
        in_specs=[pl.BlockSpec((tm, d), lambda i, e, c: (i, 0)),
                  pl.BlockSpec((d, LANES), lambda i, e, c: (0, 0)),
                  pl.BlockSpec((d, LANES), lambda i, e, c: (0, 0)),
                  pl.BlockSpec((None, d, fc), lambda i, e, c: (e, 0, c)),
                  pl.BlockSpec((None, d, fc), lambda i, e, c: (e, 0, c + nch)),
                  pl.BlockSpec((None, fc, d), lambda i, e, c: (e, c, 0)),
                  pl.BlockSpec((1, d), lambda i, e, c: (0, 0)),
                  pl.BlockSpec((1, d), lambda i, e, c: (0, 0))],
        out_specs=pl.BlockSpec((tm, d), lambda i, e, c: (i, 0)),
        out_shape=jax.ShapeDtypeStruct((n, d), F32),
        scratch_shapes=[pltpu.VMEM((tm, d), F32), pltpu.VMEM((tm, LANES), F32)],
        compiler_params=pltpu.CompilerParams(dimension_semantics=("arbitrary", "arbitrary", "arbitrary"),
                                             vmem_limit_bytes=VMEM_LIMIT),
        name="moe_ln",
    )(x, wr_hi, wr_lo, w_gu, w_gu, w_down, g, b)


def _page_specs(block, n_pages, pp, layer):
    tail = (0,) * (len(block) - 2)
    return [pl.BlockSpec(block, lambda s, p, pt, k=k: (pt[s * n_pages + p * pp + k], layer) + tail)
            for k in range(pp)]


def _log_sigmoid(z):
    return jnp.minimum(z, 0.0) - jnp.log1p(jnp.exp(-jnp.abs(z)))


def _forget_lanes(x):
    lane = _iota((1, LANES), 1)
    return jnp.where((lane >= MISC_FF) & (lane < MISC_FF + FOX_HEADS), x, 0.0)


def _cum_small(lf):
    rows = lf.shape[0]
    ri = _iota((rows, LANES), 0)
    cum = jnp.zeros((rows, LANES), F32)
    for t in range(rows):
        cum = cum + jnp.where(ri >= t, lf[t:t + 1, :], 0.0)
    return cum


def _cmp_summaries(kc_ref, vc_ref, pe_ref, phi_ref, crt_ref, nc):
    acc_k = jnp.zeros((nc, LANES), F32)
    acc_v = jnp.zeros((nc, LANES), F32)
    for l in range(CMP_BLOCK):
        rk = kc_ref[pl.ds(l, nc, stride=CMP_BLOCK), :] + pe_ref[0, l:l + 1, :]
        acc_k = acc_k + _dot(rk.astype(BF16), phi_ref[0, l])
        rv = vc_ref[pl.ds(l, nc, stride=CMP_BLOCK), :] + pe_ref[1, l:l + 1, :]
        acc_v = acc_v + _dot(rv.astype(BF16), phi_ref[1, l])
    ck = _rope128(acc_k, crt_ref[:, 0:128], crt_ref[:, 128:256], crt_ref[:, 256:384])
    return ck, acc_v


def _nsa_cmp_select(q4, ck_e, ck_o, cv_e, cv_o, qpos, *, tq, nc, ns, topn):
    qpos4 = _tile_rows(qpos, NSA_GROUP)
    blk = _iota((1, SEL_W), 1)
    n_even, n_odd = (nc + 1) // 2, nc // 2
    ok_e = ((2 * blk + 1) * CMP_BLOCK - 1 <= qpos4) & (blk < n_even)
    ok_o = ((2 * blk + 2) * CMP_BLOCK - 1 <= qpos4) & (blk < n_odd)
    s_e = jnp.where(ok_e, _dot_nt(q4, ck_e), NEG_INF)
    s_o = jnp.where(ok_o, _dot_nt(q4, ck_o), NEG_INF)
    m = jnp.maximum(jnp.max(s_e, axis=-1, keepdims=True), jnp.max(s_o, axis=-1, keepdims=True))
    m = jnp.where(m == NEG_INF, 0.0, m)
    p_e = jnp.exp(s_e - m)
    p_o = jnp.exp(s_o - m)
    den = jnp.sum(p_e, axis=-1, keepdims=True) + jnp.sum(p_o, axis=-1, keepdims=True)
    inv = 1.0 / jnp.maximum(den, 1e-30)
    p_e = p_e * inv
    p_o = p_o * inv
    o_cmp = _dot(p_e.astype(BF16), cv_e) + _dot(p_o.astype(BF16), cv_o)
    pp = p_e + p_o
    imp = pp[0:tq]
    for r in range(1, NSA_GROUP):
        imp = imp + pp[r * tq:(r + 1) * tq]
    valid = blk * SLC_BLOCK <= qpos
    forced = (blk == 0) | (blk == (qpos >> 6))
    score = jnp.where(forced, jnp.inf, jnp.where(valid, imp, NEG_INF))
    score = jnp.where(blk < ns, score, NEG_INF)
    rank = jnp.zeros((tq, SEL_W), F32)
    for b2 in range(ns):
        col = score[:, b2:b2 + 1]
        beats = (col > score) | ((col == score) & (blk > b2))
        rank = rank + jnp.where(beats, 1.0, 0.0)
    sel = jnp.where(rank < topn, 1.0, 0.0)
    return o_cmp, sel


def _block_expand(pos0, n):
    return jnp.where(_iota((SEL_W, n), 0) == ((pos0 + _iota((SEL_W, n), 1)) >> 6), 1.0, 0.0).astype(BF16)


def _nsa_q4(hq_ref, g):
    q4 = jnp.concatenate([hq_ref[:, (NSA_GROUP * g + r) * 64:(NSA_GROUP * g + r + 1) * 64]
                          for r in range(NSA_GROUP)], axis=0)
    return (q4 * 0.125).astype(BF16)


def _nsa_write(o_ref, gates, g, tq, o_cmp, o_slc, o_win):
    for r in range(NSA_GROUP):
        hh = NSA_GROUP * g + r
        rs = slice(r * tq, (r + 1) * tq)
        c0 = MISC_GATES + 3 * hh
        o_ref[:, hh * 64:(hh + 1) * 64] = (gates[:, c0:c0 + 1] * o_cmp[rs] + gates[:, c0 + 1:c0 + 2] * o_slc[rs]
                                           + gates[:, c0 + 2:c0 + 3] * o_win[rs])


def _topk_bias(key_ref, n_keep, idx, count, row_shape):
    kk = float(n_keep)
    thr = jnp.where(count(key_ref[...] >= 0) >= kk, jnp.int32(0), jnp.int32(-2147483648))
    thr = thr + jnp.zeros(row_shape, jnp.int32)

    def descend(b, thr):
        cand = thr + (jnp.int32(1) << (30 - b))
        return jnp.where(count(key_ref[...] >= cand) >= kk, cand, thr)

    thr = lax.fori_loop(0, 31, descend, thr)
    return thr, jnp.max(count(key_ref[...] >= thr)) > kk


def _topk_bias_store(out_ref, key_ref, n_keep, idx, count, row_shape):
    kk = float(n_keep)
    thr, has_ties = _topk_bias(key_ref, n_keep, idx, count, row_shape)

    @pl.when(jnp.logical_not(has_ties))
    def _():
        out_ref[...] = jnp.where(key_ref[...] >= thr, 0.0, NEG_INF)

    @pl.when(has_ties)
    def _():
        keys = key_ref[...]
        gt = keys > thr
        eq = keys == thr
        need = kk - count(gt)

        def widen(b, bound):
            cand = bound + (jnp.int32(1) << (15 - b))
            return jnp.where(count(eq & (idx < cand)) <= need, cand, bound)

        bound = lax.fori_loop(0, 16, widen, jnp.zeros(row_shape, jnp.int32))
        out_ref[...] = jnp.where(gt | (eq & (idx < bound)), 0.0, NEG_INF)


def _sortable_key(score):
    bits = pltpu.bitcast(score, jnp.int32)
    return bits ^ ((bits >> 31) & jnp.int32(0x7FFFFFFF))


def _cum_tri(lf, carry):
    rows = lf.shape[0]
    tri = jnp.where(_iota((rows, rows), 0) >= _iota((rows, rows), 1), 1.0, 0.0).astype(BF16)
    hi, mid, lo = _split3(lf)
    return _dot(tri, hi.astype(BF16)) + _dot(tri, mid.astype(BF16)) + _dot(tri, lo.astype(BF16)) + carry


def _fox_aug_cols(cum3, hh):
    chi, cmid, clo = [part[:, MISC_FF + hh:MISC_FF + hh + 1] for part in cum3]
    l64 = _iota((1, HEAD_DIM), 1)
    qx = jnp.where(l64 == 0, chi, jnp.where(l64 == 1, cmid, jnp.where(l64 == 2, clo, jnp.where(l64 < 6, 1.0, 0.0))))
    kx = jnp.where(l64 < 3, 1.0,
                   jnp.where(l64 == 3, -chi, jnp.where(l64 == 4, -cmid, jnp.where(l64 == 5, -clo, 0.0))))
    return qx, kx


def _fox_prep_kernel(fq_ref, fk_ref, misc_ref, fb_ref, qa_ref, ka_ref, lf_ref, carry_ref):
    j = pl.program_id(1)

    @pl.when(j == 0)
    def _():
        carry_ref[...] = jnp.zeros_like(carry_ref)

    lf = _forget_lanes(_log_sigmoid(misc_ref[...] + fb_ref[...]))
    lf_ref[...] = lf
    cum = _cum_tri(lf, carry_ref[...])
    rows = lf.shape[0]
    carry_ref[...] = cum[rows - 1:rows, :]
    cum3 = _split3(cum)
    for hh in range(FOX_HEADS):
        qx, kx = _fox_aug_cols(cum3, hh)
        qa_ref[:, hh * 128:hh * 128 + 64] = (fq_ref[:, hh * 64:(hh + 1) * 64] * 0.125).astype(BF16)
        qa_ref[:, hh * 128 + 64:(hh + 1) * 128] = qx.astype(BF16)
        ka_ref[:, hh * 128:hh * 128 + 64] = fk_ref[:, hh * 64:(hh + 1) * 64].astype(BF16)
        ka_ref[:, hh * 128 + 64:(hh + 1) * 128] = kx.astype(BF16)


def _fox_prep(h, fb, *, n_seq, t, rows):
    nr = t // rows
    n = n_seq * t
    return pl.pallas_call(
        _fox_prep_kernel,
        grid=(n_seq, nr),
        in_specs=[pl.BlockSpec((rows, 512), lambda s, j: (s * nr + j, E_FQ // 512)),
                  pl.BlockSpec((rows, 512), lambda s, j: (s * nr + j, E_FK // 512)),
                  pl.BlockSpec((rows, LANES), lambda s, j: (s * nr + j, E_MISC // LANES)),
                  pl.BlockSpec((1, LANES), lambda s, j: (0, 0))],
        out_specs=[pl.BlockSpec((rows, FOX_HEADS * 128), lambda s, j: (s * nr + j, 0)),
                   pl.BlockSpec((rows, FOX_HEADS * 128), lambda s, j: (s * nr + j, 0)),
                   pl.BlockSpec((rows, LANES), lambda s, j: (s * nr + j, 0))],
        out_shape=[jax.ShapeDtypeStruct((n, FOX_HEADS * 128), BF16),
                   jax.ShapeDtypeStruct((n, FOX_HEADS * 128), BF16),
                   jax.ShapeDtypeStruct((n, LANES), F32)],
        scratch_shapes=[pltpu.VMEM((1, LANES), F32)],
        compiler_params=pltpu.CompilerParams(dimension_semantics=("arbitrary", "arbitrary")),
        name="fox_prep",
    )(h, h, h, fb)


def _fox_attn_kernel(qa_ref, ka_ref, v_ref, o_ref, m_s, l_s, acc_s, *, tq, t):
    i = pl.program_id(1)
    qpos = i * tq + _iota((tq, 1), 0)
    q_end = i * tq + tq - 1
    for hh in range(2):
        _st_reset(m_s.at[hh], l_s.at[hh], acc_s.at[hh])
    for r0, n in _key_chunks(t, KEY_CHUNK):
        @pl.when(r0 <= q_end)
        def _():
            ok = (r0 + _iota((1, n), 1)) <= qpos
            for hh in range(2):
                k = ka_ref[r0:r0 + n, hh * 128:(hh + 1) * 128]
                v = v_ref[r0:r0 + n, hh * 64:(hh + 1) * 64].astype(BF16)
                s = jnp.where(ok, _dot_nt(qa_ref[:, hh * 128:(hh + 1) * 128], k), NEG_INF)
                _st_update(m_s.at[hh], l_s.at[hh], acc_s.at[hh], s, lambda pb, v=v: _dot(pb, v))
    for hh in range(2):
        o_ref[:, hh * 64:(hh + 1) * 64] = _st_finish(l_s.at[hh], acc_s.at[hh])


def _fox_attn(qa, ka, h, *, n_seq, t, tq):
    nq = t // tq
    n = n_seq * t
    return pl.pallas_call(
        functools.partial(_fox_attn_kernel, tq=tq, t=t),
        grid=(n_seq, nq, FOX_HEADS // 2),
        in_specs=[pl.BlockSpec((tq, 256), lambda s, i, p: (s * nq + i, p)),
                  pl.BlockSpec((t, 256), lambda s, i, p: (s, p)),
                  pl.BlockSpec((t, LANES), lambda s, i, p: (s, E_FV // LANES + p))],
        out_specs=pl.BlockSpec((tq, LANES), lambda s, i, p: (s * nq + i, p)),
        out_shape=jax.ShapeDtypeStruct((n, FOX_HEADS * HEAD_DIM), F32),
        scratch_shapes=[pltpu.VMEM((2, tq, 1), F32), pltpu.VMEM((2, tq, 1), F32), pltpu.VMEM((2, tq, HEAD_DIM), F32)],
        compiler_params=pltpu.CompilerParams(dimension_semantics=("arbitrary", "arbitrary", "arbitrary"),
                                             vmem_limit_bytes=VMEM_LIMIT),
        name="fox_attn",
    )(qa, ka, h)


def _nsa_kernel(hq_ref, misc_ref, kc_ref, vc_ref, slc_ref, win_ref, pe_ref, phi_ref, crt_ref, o_ref,
                ck_s, cv_s, m_s, l_s, acc_s, *, tq, t, nc, ns, topn):
    i = pl.program_id(1)

    @pl.when(i == 0)
    def _():
        ck, cv = _cmp_summaries(kc_ref, vc_ref, pe_ref, phi_ref, crt_ref, nc)
        ck_s[...] = jnp.zeros_like(ck_s)
        cv_s[...] = jnp.zeros_like(cv_s)
        ck_s[0:nc, :] = ck
        cv_s[0:nc, :] = cv

    qs = i * tq
    q_end = qs + tq - 1
    qpos = qs + _iota((tq, 1), 0)
    gates = jax.nn.sigmoid(misc_ref[...])
    n_win = min(WINDOW + tq, t)
    w0 = pl.multiple_of(jnp.maximum(qs + tq - n_win, 0), 8)

    for g in range(NSA_KV_HEADS):
        kcol = slice(g * 64, (g + 1) * 64)
        vcol = slice(128 + g * 64, 128 + (g + 1) * 64)
        q4 = _nsa_q4(hq_ref, g)
        ck_e = ck_s[pl.ds(0, SEL_W, stride=2), :][:, kcol].astype(BF16)
        ck_o = ck_s[pl.ds(1, SEL_W, stride=2), :][:, kcol].astype(BF16)
        cv_e = cv_s[pl.ds(0, SEL_W, stride=2), :][:, kcol].astype(BF16)
        cv_o = cv_s[pl.ds(1, SEL_W, stride=2), :][:, kcol].astype(BF16)
        o_cmp, sel = _nsa_cmp_select(q4, ck_e, ck_o, cv_e, cv_o, qpos, tq=tq, nc=nc, ns=ns, topn=topn)
        sel = sel.astype(BF16)

        _st_reset(m_s, l_s, acc_s)
        for r0, n in _key_chunks(t, KEY_CHUNK):
            @pl.when(r0 <= q_end)
            def _():
                k = slc_ref[r0:r0 + n, kcol].astype(BF16)
                v = slc_ref[r0:r0 + n, vcol].astype(BF16)
                selx = _dot(sel, _block_expand(r0, n))
                ok = (selx > 0.5) & ((r0 + _iota((1, n), 1)) <= qpos)
                bias = jnp.where(ok, 0.0, NEG_INF)
                s = _dot_nt(q4, k) + _tile_rows(bias, NSA_GROUP)
                _st_update(m_s, l_s, acc_s, s, lambda pb: _dot(pb, v))
        o_slc = _st_finish(l_s, acc_s)

        _st_reset(m_s, l_s, acc_s)
        k = win_ref[pl.ds(w0, n_win), kcol].astype(BF16)
        v = win_ref[pl.ds(w0, n_win), vcol].astype(BF16)
        kpos = w0 + _iota((1, n_win), 1)
        ok = (kpos <= qpos) & (kpos > qpos - WINDOW)
        bias = jnp.where(ok, 0.0, NEG_INF)
        s = _dot_nt(q4, k) + _tile_rows(bias, NSA_GROUP)
        _st_update(m_s, l_s, acc_s, s, lambda pb: _dot(pb, v))
        o_win = _st_finish(l_s, acc_s)

        _nsa_write(o_ref, gates, g, tq, o_cmp, o_slc, o_win)


def _nsa_attn(h, pe, phi, crt, *, n_seq, t, tq, topn):
    nq = t // tq
    n = n_seq * t
    nc = t // CMP_BLOCK
    ns = -(-t // SLC_BLOCK)
    assert ns <= SEL_W and nc <= 2 * SEL_W
    rows4 = NSA_GROUP * tq

    def qblk(col):
        return lambda s, i: (s * nq + i, col)

    def kblk(col):
        return lambda s, i: (s, col)

    return pl.pallas_call(
        functools.partial(_nsa_kernel, tq=tq, t=t, nc=nc, ns=ns, topn=min(topn, ns)),
        grid=(n_seq, nq),
        in_specs=[pl.BlockSpec((tq, 512), qblk(E_NQ // 512)),
                  pl.BlockSpec((tq, LANES), qblk(E_MISC // LANES)),
                  pl.BlockSpec((t, LANES), kblk(E_CMP // LANES)),
                  pl.BlockSpec((t, LANES), kblk(E_CMP // LANES + 1)),
                  pl.BlockSpec((t, 256), kblk(E_SLC // 256)),
                  pl.BlockSpec((t, 256), kblk(E_WIN // 256)),
                  pl.BlockSpec(pe.shape, lambda s, i: (0, 0, 0)),
                  pl.BlockSpec(phi.shape, lambda s, i: (0, 0, 0, 0)),
                  pl.BlockSpec(crt.shape, lambda s, i: (0, 0))],
        out_specs=pl.BlockSpec((tq, NSA_HEADS * HEAD_DIM), lambda s, i: (s * nq + i, 0)),
        out_shape=jax.ShapeDtypeStruct((n, NSA_HEADS * HEAD_DIM), F32),
        scratch_shapes=[pltpu.VMEM((2 * SEL_W, LANES), F32), pltpu.VMEM((2 * SEL_W, LANES), F32),
                        pltpu.VMEM((rows4, 1), F32), pltpu.VMEM((rows4, 1), F32),
                        pltpu.VMEM((rows4, HEAD_DIM), F32)],
        compiler_params=pltpu.CompilerParams(dimension_semantics=("arbitrary", "arbitrary"),
                                             vmem_limit_bytes=VMEM_LIMIT),
        name="nsa_attn",
    )(h, h, h, h, h, h, pe, phi, crt)


def _dsa_kernel(q_ref, qi_ref, kw_ref, kv_ref, ki_ref, o_ref, sc_s, key_s, m_s, l_s, acc_s, *, tq, t, n_keep):
    i = pl.program_id(1)
    qpos = i * tq + _iota((tq, 1), 0)
    q_end = i * tq + tq - 1
    chunks = _key_chunks(t, KEY_CHUNK)

    sc_s[...] = jnp.full(sc_s.shape, NEG_INF, F32)
    wi = kw_ref[:, IDX_DIM:IDX_DIM + IDX_HEADS] * (IDX_HEADS ** -0.5)
    qi = [(qi_ref[:, hh * IDX_DIM:(hh + 1) * IDX_DIM] * (IDX_DIM ** -0.5)).astype(BF16) for hh in range(IDX_HEADS)]
    for r0, n in chunks:
        @pl.when(r0 <= q_end)
        def _():
            ki = ki_ref[r0:r0 + n, 0:IDX_DIM].astype(BF16)
            sc = jnp.zeros((tq, n), F32)
            for hh in range(IDX_HEADS):
                sc = sc + wi[:, hh:hh + 1] * jnp.maximum(_dot_nt(qi[hh], ki), 0.0)
            ok = (r0 + _iota((1, n), 1)) <= qpos
            sc_s[:, r0:r0 + n] = jnp.where(ok, sc + 0.0, NEG_INF)

    key_s[...] = _sortable_key(sc_s[...])
    _topk_bias_store(sc_s, key_s, n_keep, _iota((1, t), 1),
                     lambda mask: jnp.sum(jnp.where(mask, 1.0, 0.0), axis=-1, keepdims=True), (tq, 1))

    q4s = []
    for kh in range(DSA_KV_HEADS):
        q4 = jnp.concatenate([q_ref[:, (DSA_GROUP * kh + r) * 64:(DSA_GROUP * kh + r + 1) * 64]
                              for r in range(DSA_GROUP)], axis=0)
        q4s.append((q4 * 0.125).astype(BF16))
        _st_reset(m_s.at[kh], l_s.at[kh], acc_s.at[kh])
    for r0, n in chunks:
        @pl.when(r0 <= q_end)
        def _():
            ok = (r0 + _iota((1, n), 1)) <= qpos
            bias4 = _tile_rows(jnp.where(ok, sc_s[:, r0:r0 + n], NEG_INF), DSA_GROUP)
            for kh in range(DSA_KV_HEADS):
                k = kv_ref[r0:r0 + n, kh * 64:(kh + 1) * 64].astype(BF16)
                v = kv_ref[r0:r0 + n, 256 + kh * 64:256 + (kh + 1) * 64].astype(BF16)
                _st_update(m_s.at[kh], l_s.at[kh], acc_s.at[kh], _dot_nt(q4s[kh], k) + bias4,
                           lambda pb, v=v: _dot(pb, v))
    for kh in range(DSA_KV_HEADS):
        o = _st_finish(l_s.at[kh], acc_s.at[kh])
        for r in range(DSA_GROUP):
            hh = DSA_GROUP * kh + r
            o_ref[:, hh * 64:(hh + 1) * 64] = o[r * tq:(r + 1) * tq]


def _dsa_attn(h, *, n_seq, t, tq, n_keep):
    nq = t // tq
    n = n_seq * t
    rows4 = DSA_GROUP * tq
    assert t % LANES == 0 and t <= 1 << 16

    def qblk(w, col):
        return pl.BlockSpec((tq, w), lambda s, i: (s * nq + i, col))

    return pl.pallas_call(
        functools.partial(_dsa_kernel, tq=tq, t=t, n_keep=n_keep),
        grid=(n_seq, nq),
        in_specs=[qblk(1024, O_Q // 1024), qblk(512, O_QI // 512), qblk(LANES, O_KI // LANES),
                  pl.BlockSpec((t, 512), lambda s, i: (s, O_KV // 512)),
                  pl.BlockSpec((t, LANES), lambda s, i: (s, O_KI // LANES))],
        out_specs=pl.BlockSpec((tq, DSA_HEADS * HEAD_DIM), lambda s, i: (s * nq + i, 0)),
        out_shape=jax.ShapeDtypeStruct((n, DSA_HEADS * HEAD_DIM), F32),
        scratch_shapes=[pltpu.VMEM((tq, t), F32), pltpu.VMEM((tq, t), jnp.int32),
                        pltpu.VMEM((DSA_KV_HEADS, rows4, 1), F32), pltpu.VMEM((DSA_KV_HEADS, rows4, 1), F32),
                        pltpu.VMEM((DSA_KV_HEADS, rows4, HEAD_DIM), F32)],
        compiler_params=pltpu.CompilerParams(dimension_semantics=("arbitrary", "arbitrary"),
                                             vmem_limit_bytes=VMEM_LIMIT),
        name="dsa_attn",
    )(h, h, h, h, h)


def _cmp_sample_kernel(pt_ref, *refs, pp, n_steps, nc):
    slabs = refs[:pp]
    pe_ref, phi_ref, crt_ref, o_ref, k_s, v_s, c_s = refs[pp:]
    p = pl.program_id(1)
    for k in range(pp):
        row0 = pl.multiple_of((p * pp + k) * PAGE_SIZE, PAGE_SIZE)
        for g in range(NSA_KV_HEADS):
            k_s[pl.ds(row0, PAGE_SIZE), g * 64:(g + 1) * 64] = slabs[k][0, g].T
            v_s[pl.ds(row0, PAGE_SIZE), g * 64:(g + 1) * 64] = slabs[k][1, g].T

    @pl.when(p == n_steps - 1)
    def _():
        ck, cv = _cmp_summaries(k_s, v_s, pe_ref, phi_ref, crt_ref, nc)
        for idx, val in ((0, ck), (2, cv)):
            c_s[...] = jnp.zeros_like(c_s)
            c_s[0:nc, :] = val
            o_ref[idx] = c_s[pl.ds(0, SEL_W, stride=2), :]
            o_ref[idx + 1] = c_s[pl.ds(1, SEL_W, stride=2), :]


def _cmp_sample(cmp_v, pt_flat, pe, phi, crt, *, layer, n_seq, n_pages, pp, nc):
    lp = n_pages * PAGE_SIZE
    n_steps = n_pages // pp
    in_specs = _page_specs((None, None, 2, NSA_KV_HEADS, HEAD_DIM, PAGE_SIZE), n_pages, pp, layer)
    in_specs += [pl.BlockSpec(pe.shape, lambda s, p, pt: (0, 0, 0)),
                 pl.BlockSpec(phi.shape, lambda s, p, pt: (0, 0, 0, 0)),
                 pl.BlockSpec(crt.shape, lambda s, p, pt: (0, 0))]
    return pl.pallas_call(
        functools.partial(_cmp_sample_kernel, pp=pp, n_steps=n_steps, nc=nc),
        grid_spec=pltpu.PrefetchScalarGridSpec(
            num_scalar_prefetch=1, grid=(n_seq, n_steps), in_specs=in_specs,
            out_specs=pl.BlockSpec((None, 4, SEL_W, LANES), lambda s, p, pt: (s, 0, 0, 0)),
            scratch_shapes=[pltpu.VMEM((lp, LANES), F32), pltpu.VMEM((lp, LANES), F32),
                            pltpu.VMEM((2 * SEL_W, LANES), F32)]),
        out_shape=jax.ShapeDtypeStruct((n_seq, 4, SEL_W, LANES), F32),
        compiler_params=pltpu.CompilerParams(dimension_semantics=("arbitrary", "arbitrary"),
                                             vmem_limit_bytes=VMEM_LIMIT),
        name="cmp_sample",
    )(pt_flat, *([cmp_v] * pp), pe, phi, crt)


def _nsa_sample_kernel(pt_ref, *refs, pp, n_steps, dt, l_past, w_past, nc, ns, topn):
    hq_ref, misc_ref, slc_ref, win_ref, ckv_ref, winp_ref = refs[:6]
    slabs = refs[6:6 + pp]
    o_ref, sel_s, ocmp_s, m_s, l_s, acc_s = refs[6 + pp:]
    p = pl.program_id(1)
    qpos = l_past + _iota((dt, 1), 0)
    new_pos = l_past + _iota((1, LANES), 1)
    new_ok = (new_pos <= qpos) & (new_pos < l_past + dt)

    @pl.when(p == 0)
    def _():
        for g in range(NSA_KV_HEADS):
            kcol = slice(g * 64, (g + 1) * 64)
            o_cmp, sel = _nsa_cmp_select(
                _nsa_q4(hq_ref, g), ckv_ref[0][:, kcol].astype(BF16), ckv_ref[1][:, kcol].astype(BF16),
                ckv_ref[2][:, kcol].astype(BF16), ckv_ref[3][:, kcol].astype(BF16), qpos,
                tq=dt, nc=nc, ns=ns, topn=topn)
            ocmp_s[g] = o_cmp
            sel_s[g] = sel
            _st_reset(m_s.at[g], l_s.at[g], acc_s.at[g])

    for k in range(pp):
        expand = _block_expand((p * pp + k) * PAGE_SIZE, PAGE_SIZE)
        for g in range(NSA_KV_HEADS):
            kt = slabs[k][0, g].astype(BF16)
            vt = slabs[k][1, g].astype(BF16)
            selx = _dot(sel_s[g].astype(BF16), expand)
            bias = jnp.where(selx > 0.5, 0.0, NEG_INF)
            s = _dot(_nsa_q4(hq_ref, g), kt) + _tile_rows(bias, NSA_GROUP)
            _st_update(m_s.at[g], l_s.at[g], acc_s.at[g], s, lambda pb, vt=vt: _dot_nt(pb, vt))

    @pl.when(p == n_steps - 1)
    def _():
        gates = jax.nn.sigmoid(misc_ref[...])
        for g in range(NSA_KV_HEADS):
            kcol = slice(g * 64, (g + 1) * 64)
            vcol = slice(128 + g * 64, 128 + (g + 1) * 64)
            q4 = _nsa_q4(hq_ref, g)
            st = (m_s.at[g], l_s.at[g], acc_s.at[g])
            k = _pad_rows(slc_ref[:, kcol].astype(BF16), LANES)
            v = _pad_rows(slc_ref[:, vcol].astype(BF16), LANES)
            selx = _dot(sel_s[g].astype(BF16), _block_expand(l_past, LANES))
            bias = jnp.where((selx > 0.5) & new_ok, 0.0, NEG_INF)
            _st_update(*st, _dot_nt(q4, k) + _tile_rows(bias, NSA_GROUP), lambda pb: _dot(pb, v))
            o_slc = _st_finish(st[1], st[2])
            _st_reset(*st)
            kt = winp_ref[0, g].astype(BF16)
            vt = winp_ref[1, g].astype(BF16)
            wpos = l_past - w_past + _iota((1, w_past), 1)
            bias = jnp.where(wpos > qpos - WINDOW, 0.0, NEG_INF)
            _st_update(*st, _dot(q4, kt) + _tile_rows(bias, NSA_GROUP), lambda pb: _dot_nt(pb, vt))
            k = _pad_rows(win_ref[:, kcol].astype(BF16), LANES)
            v = _pad_rows(win_ref[:, vcol].astype(BF16), LANES)
            bias = jnp.where(new_ok & (new_pos > qpos - WINDOW), 0.0, NEG_INF)
            _st_update(*st, _dot_nt(q4, k) + _tile_rows(bias, NSA_GROUP), lambda pb: _dot(pb, v))
            o_win = _st_finish(st[1], st[2])
            _nsa_write(o_ref, gates, g, dt, ocmp_s[g], o_slc, o_win)


def _nsa_sample(h, ckv, slc_v, win_v, pt_flat, *, layer, row0, n_seq, dt, n_pages, pp, w_past, topn):
    l_past = n_pages * PAGE_SIZE
    l_tot = l_past + dt
    nc = l_tot // CMP_BLOCK
    ns = -(-l_tot // SLC_BLOCK)
    assert ns <= SEL_W and nc <= 2 * SEL_W and dt <= LANES
    n_steps = n_pages // pp
    rb0 = row0 // dt
    rows4 = NSA_GROUP * dt

    def hblk(w, col):
        return pl.BlockSpec((dt, w), lambda s, p, pt: (rb0 + s, col))

    in_specs = [hblk(512, E_NQ // 512), hblk(LANES, E_MISC // LANES), hblk(256, E_SLC // 256), hblk(256, E_WIN // 256),
                pl.BlockSpec((None, 4, SEL_W, LANES), lambda s, p, pt: (s, 0, 0, 0)),
                pl.BlockSpec((None, None, 2, NSA_KV_HEADS, HEAD_DIM, w_past), lambda s, p, pt: (s, layer, 0, 0, 0, 0))]
    in_specs += _page_specs((None, None, 2, NSA_KV_HEADS, HEAD_DIM, PAGE_SIZE), n_pages, pp, layer)
    return pl.pallas_call(
        functools.partial(_nsa_sample_kernel, pp=pp, n_steps=n_steps, dt=dt, l_past=l_past, w_past=w_past,
                          nc=nc, ns=ns, topn=min(topn, ns)),
        grid_spec=pltpu.PrefetchScalarGridSpec(
            num_scalar_prefetch=1, grid=(n_seq, n_steps), in_specs=in_specs,
            out_specs=pl.BlockSpec((dt, NSA_HEADS * HEAD_DIM), lambda s, p, pt: (s, 0)),
            scratch_shapes=[pltpu.VMEM((NSA_KV_HEADS, dt, SEL_W), F32), pltpu.VMEM((NSA_KV_HEADS, rows4, HEAD_DIM), F32),
                            pltpu.VMEM((NSA_KV_HEADS, rows4, 1), F32), pltpu.VMEM((NSA_KV_HEADS, rows4, 1), F32),
                            pltpu.VMEM((NSA_KV_HEADS, rows4, HEAD_DIM), F32)]),
        out_shape=jax.ShapeDtypeStruct((n_seq * dt, NSA_HEADS * HEAD_DIM), F32),
        compiler_params=pltpu.CompilerParams(dimension_semantics=("arbitrary", "arbitrary"),
                                             vmem_limit_bytes=VMEM_LIMIT),
        name="nsa_sample",
    )(pt_flat, h, h, h, h, ckv, win_v, *([slc_v] * pp))


def _fox_sample_kernel(pt_ref, *refs, pp, n_steps, dt, l_past):
    fq_ref, fk_ref, fv_ref, misc_ref, fb_ref = refs[:5]
    kv_slabs = refs[5:5 + pp]
    lf_slabs = refs[5 + pp:5 + 2 * pp]
    o_ref, lf_ref, qbd_s, carry_s, m_s, l_s, acc_s = refs[5 + 2 * pp:]
    p = pl.program_id(1)
    rows = FOX_HEADS * dt
    width = FOX_HEADS * HEAD_DIM

    @pl.when(p == 0)
    def _():
        qbd_s[...] = jnp.zeros_like(qbd_s)
        for hh in range(FOX_HEADS):
            cs = slice(hh * 64, (hh + 1) * 64)
            qbd_s[hh * dt:(hh + 1) * dt, cs] = (fq_ref[:, cs] * 0.125).astype(BF16)
        carry_s[...] = jnp.zeros_like(carry_s)
        _st_reset(m_s, l_s, acc_s)

    def head_rows(x):
        return jnp.concatenate([jnp.broadcast_to(x[hh:hh + 1, :], (dt, LANES)) for hh in range(FOX_HEADS)], axis=0)

    upper = jnp.where(_iota((LANES, LANES), 0) <= _iota((LANES, LANES), 1), 1.0, 0.0).astype(BF16)
    for k in range(pp):
        hi, mid, lo = _split3(lf_slabs[k][...])
        cum = (_dot(hi.astype(BF16), upper) + _dot(mid.astype(BF16), upper) + _dot(lo.astype(BF16), upper)
               + carry_s[...])
        carry_s[...] = jnp.broadcast_to(cum[:, LANES - 1:LANES], (FOX_HEADS, LANES))
        kt = kv_slabs[k][0].reshape(width, PAGE_SIZE).astype(BF16)
        vt = kv_slabs[k][1].reshape(width, PAGE_SIZE).astype(BF16)
        s = _dot(qbd_s[...], kt) - head_rows(cum)
        _st_update(m_s, l_s, acc_s, s, lambda pb, vt=vt: _dot_nt(pb, vt))

    @pl.when(p == n_steps - 1)
    def _():
        lf = _forget_lanes(_log_sigmoid(misc_ref[...] + fb_ref[...]))
        lf_ref[...] = lf
        parts = _split3(_pad_rows(_cum_small(lf), LANES))
        lane = _iota((dt, LANES), 1)
        pick = jnp.concatenate([jnp.where(lane == MISC_FF + hh, 1.0, 0.0) for hh in range(FOX_HEADS)], axis=0)
        pick = pick.astype(BF16)
        cum_new = sum(_dot_nt(pick, part.astype(BF16)) for part in parts)
        kn = _pad_rows(fk_ref[...].astype(BF16), LANES)
        vn = _pad_rows(fv_ref[...].astype(BF16), LANES)
        col = _iota((1, LANES), 1)
        trow = _tile_rows(_iota((dt, 1), 0), FOX_HEADS)
        ok = (col <= trow) & (col < dt)
        s = _dot_nt(qbd_s[...], kn) - (head_rows(carry_s[...]) + cum_new)
        _st_update(m_s, l_s, acc_s, jnp.where(ok, s, NEG_INF), lambda pb: _dot(pb, vn))
        o = _st_finish(l_s, acc_s)
        for hh in range(FOX_HEADS):
            o_ref[:, hh * 64:(hh + 1) * 64] = o[hh * dt:(hh + 1) * dt, hh * 64:(hh + 1) * 64]


def _fox_sample(h, fb, fox_v, flf_v, pt_flat, *, layer, row0, n_seq, dt, n_pages, pp):
    l_past = n_pages * PAGE_SIZE
    n_steps = n_pages // pp
    rb0 = row0 // dt
    rows = FOX_HEADS * dt
    width = FOX_HEADS * HEAD_DIM
    assert dt <= LANES

    def hblk(w, col):
        return pl.BlockSpec((dt, w), lambda s, p, pt: (rb0 + s, col))

    in_specs = [hblk(512, E_FQ // 512), hblk(512, E_FK // 512), hblk(512, E_FV // 512), hblk(LANES, E_MISC // LANES),
                pl.BlockSpec((1, LANES), lambda s, p, pt: (0, 0))]
    in_specs += _page_specs((None, None, 2, FOX_HEADS, HEAD_DIM, PAGE_SIZE), n_pages, pp, layer)
    in_specs += _page_specs((None, None, FOX_HEADS, PAGE_SIZE), n_pages, pp, layer)
    return pl.pallas_call(
        functools.partial(_fox_sample_kernel, pp=pp, n_steps=n_steps, dt=dt, l_past=l_past),
        grid_spec=pltpu.PrefetchScalarGridSpec(
            num_scalar_prefetch=1, grid=(n_seq, n_steps), in_specs=in_specs,
            out_specs=[pl.BlockSpec((dt, width), lambda s, p, pt: (s, 0)),
                       pl.BlockSpec((dt, LANES), lambda s, p, pt: (s, 0))],
            scratch_shapes=[pltpu.VMEM((rows, width), BF16), pltpu.VMEM((FOX_HEADS, LANES), F32),
                            pltpu.VMEM((rows, 1), F32), pltpu.VMEM((rows, 1), F32), pltpu.VMEM((rows, width), F32)]),
        out_shape=[jax.ShapeDtypeStruct((n_seq * dt, width), F32), jax.ShapeDtypeStruct((n_seq * dt, LANES), F32)],
        compiler_params=pltpu.CompilerParams(dimension_semantics=("arbitrary", "arbitrary"),
                                             vmem_limit_bytes=VMEM_LIMIT),
        name="fox_sample",
    )(pt_flat, h, h, h, h, fb, *([fox_v] * pp), *([flf_v] * pp))


def _dsa_index_kernel(pt_ref, *refs, pp, n_steps, dt, n_pages, n_keep):
    qi_ref, kw_ref = refs[:2]
    slabs = refs[2:2 + pp]
    o_ref, sc_s, key_s = refs[2 + pp:]
    p = pl.program_id(1)
    wi = kw_ref[:, IDX_DIM:IDX_DIM + IDX_HEADS] * (IDX_HEADS ** -0.5)
    qst = jnp.concatenate([qi_ref[:, hh * IDX_DIM:(hh + 1) * IDX_DIM] for hh in range(IDX_HEADS)], axis=0)
    qst = (qst * (IDX_DIM ** -0.5)).astype(BF16)

    def scores(lg):
        sc = jnp.zeros((dt, lg.shape[1]), F32)
        for hh in range(IDX_HEADS):
            sc = sc + wi[:, hh:hh + 1] * jnp.maximum(lg[hh * dt:(hh + 1) * dt], 0.0)
        return sc + 0.0

    for k in range(pp):
        sc_s[p * pp + k] = scores(_dot(qst, slabs[k][...].astype(BF16)))

    @pl.when(p == n_steps - 1)
    def _():
        kin = _pad_rows(kw_ref[:, 0:IDX_DIM].astype(BF16), LANES)
        col = _iota((1, LANES), 1)
        ok = (col <= _iota((dt, 1), 0)) & (col < dt)
        sc_s[n_pages] = jnp.where(ok, scores(_dot_nt(qst, kin)), NEG_INF)
        key_s[...] = _sortable_key(sc_s[...])
        shape = (n_pages + 1, dt, LANES)
        idx = _iota(shape, 0) * LANES + _iota(shape, 2)
        _topk_bias_store(o_ref, key_s, n_keep, idx,
                         lambda mask: jnp.sum(jnp.sum(jnp.where(mask, 1.0, 0.0), axis=0), axis=-1, keepdims=True)[None],
                         (1, dt, 1))


def _dsa_index(h, dik_v, pt_flat, *, layer, row0, n_seq, dt, n_pages, pp, n_keep):
    n_steps = n_pages // pp
    rb0 = row0 // dt
    shape = (n_pages + 1, dt, LANES)
    in_specs = [pl.BlockSpec((dt, 512), lambda s, p, pt: (rb0 + s, O_QI // 512)),
                pl.BlockSpec((dt, LANES), lambda s, p, pt: (rb0 + s, O_KI // LANES))]
    in_specs += _page_specs((None, None, IDX_DIM, PAGE_SIZE), n_pages, pp, layer)
    return pl.pallas_call(
        functools.partial(_dsa_index_kernel, pp=pp, n_steps=n_steps, dt=dt, n_pages=n_pages, n_keep=n_keep),
        grid_spec=pltpu.PrefetchScalarGridSpec(
            num_scalar_prefetch=1, grid=(n_seq, n_steps), in_specs=in_specs,
            out_specs=pl.BlockSpec((None,) + shape, lambda s, p, pt: (s, 0, 0, 0)),
            scratch_shapes=[pltpu.VMEM(shape, F32), pltpu.VMEM(shape, jnp.int32)]),
        out_shape=jax.ShapeDtypeStruct((n_seq,) + shape, F32),
        compiler_params=pltpu.CompilerParams(dimension_semantics=("arbitrary", "arbitrary")),
        name="dsa_index",
    )(pt_flat, h, h, *([dik_v] * pp))


def _dsa_sample_kernel(pt_ref, *refs, pp, n_steps, dt, n_pages):
    q_ref, kv_ref, bias_ref = refs[:3]
    slabs = refs[3:3 + pp]
    o_ref, qbd_s, m_s, l_s, acc_s = refs[3 + pp:]
    p = pl.program_id(1)
    width = DSA_KV_HEADS * HEAD_DIM

    @pl.when(p == 0)
    def _():
        qbd_s[...] = jnp.zeros_like(qbd_s)
        for hh in range(DSA_HEADS):
            kh = hh // DSA_GROUP
            qbd_s[hh * dt:(hh + 1) * dt, kh * 64:(kh + 1) * 64] = (q_ref[:, hh * 64:(hh + 1) * 64] * 0.125).astype(BF16)
        _st_reset(m_s, l_s, acc_s)

    for k in range(pp):
        kt = slabs[k][0].reshape(width, PAGE_SIZE).astype(BF16)
        vt = slabs[k][1].reshape(width, PAGE_SIZE).astype(BF16)
        s = _dot(qbd_s[...], kt) + _tile_rows(bias_ref[p * pp + k], DSA_HEADS)
        _st_update(m_s, l_s, acc_s, s, lambda pb, vt=vt: _dot_nt(pb, vt))

    @pl.when(p == n_steps - 1)
    def _():
        kn = _pad_rows(kv_ref[:, 0:width].astype(BF16), LANES)
        vn = _pad_rows(kv_ref[:, width:2 * width].astype(BF16), LANES)
        s = _dot_nt(qbd_s[...], kn) + _tile_rows(bias_ref[n_pages], DSA_HEADS)
        _st_update(m_s, l_s, acc_s, s, lambda pb: _dot(pb, vn))
        o = _st_finish(l_s, acc_s)
        for hh in range(DSA_HEADS):
            kh = hh // DSA_GROUP
            o_ref[:, hh * 64:(hh + 1) * 64] = o[hh * dt:(hh + 1) * dt, kh * 64:(kh + 1) * 64]


def _dsa_sample(h, bias, dkv_v, pt_flat, *, layer, row0, n_seq, dt, n_pages, pp):
    n_steps = n_pages // pp
    rb0 = row0 // dt
    rows = DSA_HEADS * dt
    width = DSA_KV_HEADS * HEAD_DIM
    in_specs = [pl.BlockSpec((dt, 1024), lambda s, p, pt: (rb0 + s, O_Q // 1024)),
                pl.BlockSpec((dt, 512), lambda s, p, pt: (rb0 + s, O_KV // 512)),
                pl.BlockSpec((None, n_pages + 1, dt, LANES), lambda s, p, pt: (s, 0, 0, 0))]
    in_specs += _page_specs((None, None, 2, DSA_KV_HEADS, HEAD_DIM, PAGE_SIZE), n_pages, pp, layer)
    return pl.pallas_call(
        functools.partial(_dsa_sample_kernel, pp=pp, n_steps=n_steps, dt=dt, n_pages=n_pages),
        grid_spec=pltpu.PrefetchScalarGridSpec(
            num_scalar_prefetch=1, grid=(n_seq, n_steps), in_specs=in_specs,
            out_specs=pl.BlockSpec((dt, DSA_HEADS * HEAD_DIM), lambda s, p, pt: (s, 0)),
            scratch_shapes=[pltpu.VMEM((rows, width), BF16), pltpu.VMEM((rows, 1), F32), pltpu.VMEM((rows, 1), F32),
                            pltpu.VMEM((rows, width), F32)]),
        out_shape=jax.ShapeDtypeStruct((n_seq * dt, DSA_HEADS * HEAD_DIM), F32),
        compiler_params=pltpu.CompilerParams(dimension_semantics=("arbitrary", "arbitrary"),
                                             vmem_limit_bytes=VMEM_LIMIT),
        name="dsa_sample",
    )(pt_flat, h, h, bias, *([dkv_v] * pp))


def _rope_table(pos):
    half = ROT_DIM // 2
    inv = ROPE_THETA ** (-2.0 * jnp.arange(half, dtype=F32) / ROT_DIM)
    ang = pos.astype(F32)[:, None] * inv[None, :]
    cos, sin = jnp.cos(ang), jnp.sin(ang)
    n = pos.shape[0]
    one = jnp.ones((n, HEAD_DIM - ROT_DIM), F32)
    zero = jnp.zeros((n, HEAD_DIM - ROT_DIM), F32)
    z8 = jnp.zeros((n, half), F32)
    c = jnp.concatenate([cos, cos, one], axis=1)
    s1 = jnp.concatenate([z8, sin, zero], axis=1)
    s2 = jnp.concatenate([-sin, z8, zero], axis=1)
    return jnp.concatenate([c, c, s1, s1, s2, s2], axis=1)


def _largest_tile(limit, *sizes):
    t = limit
    while any(s % t for s in sizes):
        t //= 2
    return t


def _rows_last(pool):
    nd = pool.ndim
    return jnp.transpose(pool, (0, 1) + tuple(range(3, nd)) + (2,))


def kernel(x_prompt, x_sample, cache_nsa_cmp_kv, cache_nsa_slc_kv, state_nsa_win_kv, cache_fox_kv, cache_fox_logf, cache_dsa_kv, cache_dsa_idx_k, page_table, ln_g, ln_b, w_in_even, w_out_even, fox_f_bias, nsa_cmp_pos, nsa_cmp_phi, w_in_odd, w_out_odd, ffn_gu, ffn_down, moe_router, moe_gu, moe_down):
    nb, t, d = x_prompt.shape
    db, dt, _ = x_sample.shape
    n_pages = page_table.shape[1]
    lp = n_pages * PAGE_SIZE
    w_past = state_nsa_win_kv.shape[2]
    n_p, n_s = nb * t, db * dt
    depth = ln_g.shape[0]
    alpha = (2 * depth) ** 0.25
    tm = _largest_tile(512, n_p, n_s)
    tm_ff = _largest_tile(512, n_p, n_s)
    tq = _largest_tile(128, t)
    pp_small = math.gcd(n_pages, 8)
    pp_big = math.gcd(n_pages, 4)
    pt_flat = page_table.reshape(-1).astype(jnp.int32)

    cmp_v, slc_v, win_v = _rows_last(cache_nsa_cmp_kv), _rows_last(cache_nsa_slc_kv), _rows_last(state_nsa_win_kv)
    fox_v, flf_v = _rows_last(cache_fox_kv), _rows_last(cache_fox_logf)
    dkv_v, dik_v = _rows_last(cache_dsa_kv), _rows_last(cache_dsa_idx_k)

    x = jnp.concatenate([x_prompt.reshape(n_p, d), x_sample.reshape(n_s, d)], axis=0)
    pos = jnp.concatenate([jnp.tile(jnp.arange(t), nb), jnp.tile(lp + jnp.arange(dt), db)])
    rt = _rope_table(pos)
    nc_p, nc_s = t // CMP_BLOCK, (lp + dt) // CMP_BLOCK
    assert lp % CMP_BLOCK == 0 and dt < CMP_BLOCK
    crt_p = _rope_table((jnp.arange(nc_p) + 1) * CMP_BLOCK - 1)
    crt_s = _rope_table((jnp.arange(nc_s) + 1) * CMP_BLOCK - 1)

    ev_p, ev_s, od_p, od_s = [], [], [], []
    for layer in range(depth):
        j = layer // 2
        g0, b0 = ln_g[layer, 0][None, :], ln_b[layer, 0][None, :]
        g1, b1 = ln_g[layer, 1][None, :], ln_b[layer, 1][None, :]
        if layer % 2 == 0:
            w = w_in_even[j]
            w = jnp.concatenate([w[:, 0:512], w[:, 1304:2840], w[:, 512:1280], w[:, 1280:1304], w[:, 2840:2848],
                                 jnp.zeros((d, E_W - 2848), w.dtype)], axis=1).astype(BF16)
            h = _proj(x, w, rt, width=E_W, rope_full=EVEN_ROPE, rope_lo=(), tm=tm)

            pe = jnp.concatenate([nsa_cmp_pos[j], nsa_cmp_pos[j]], axis=-1)
            ph = nsa_cmp_phi[j]
            zz = jnp.zeros_like(ph)
            phi = jnp.concatenate([jnp.concatenate([ph, zz], axis=-1),
                                   jnp.concatenate([zz, ph], axis=-1)], axis=-2).astype(BF16)
            fb = jnp.zeros((1, LANES), F32).at[0, MISC_FF:MISC_FF + FOX_HEADS].set(fox_f_bias[j])

            qa_p, ka_p, lf_p = _fox_prep(h, fb, n_seq=nb, t=t, rows=_largest_tile(256, t))
            o_nsa_p = _nsa_attn(h, pe, phi, crt_p, n_seq=nb, t=t, tq=tq, topn=SLC_TOPN)
            o_fox_p = _fox_attn(qa_p, ka_p, h, n_seq=nb, t=t, tq=tq)

            ckv = _cmp_sample(cmp_v, pt_flat, pe, phi, crt_s, layer=j, n_seq=db, n_pages=n_pages, pp=pp_small, nc=nc_s)
            o_nsa_s = _nsa_sample(h, ckv, slc_v, win_v, pt_flat, layer=j, row0=n_p, n_seq=db, dt=dt, n_pages=n_pages,
                                  pp=pp_small, w_past=w_past, topn=SLC_TOPN)
            o_fox_s, lf_s = _fox_sample(h, fb, fox_v, flf_v, pt_flat, layer=j, row0=n_p, n_seq=db, dt=dt,
                                        n_pages=n_pages, pp=pp_big)

            wo = w_out_even[j].astype(BF16)
            x = _out_ln([o_nsa_p, o_fox_p], [o_nsa_s, o_fox_s], [wo[0:512], wo[512:1024]], x, g0, b0,
                        alpha=alpha, tm=tm)
            x = _ffn_ln(x, ffn_gu[j].astype(BF16), ffn_down[j].astype(BF16), g1, b1, alpha=alpha, tm=tm_ff,
                        fc=ffn_down.shape[1] // 2)

            hp, hs = h[:n_p].reshape(nb, t, E_W), h[n_p:].reshape(db, dt, E_W)
            kv2 = (2, NSA_KV_HEADS, HEAD_DIM)
            win_new = hs[:, :, E_WIN:E_WIN + 256].reshape((db, dt) + kv2)
            keep_p, keep_s = min(WINDOW, t), min(WINDOW, w_past + dt)
            win_s = jnp.concatenate([state_nsa_win_kv[:, j, w_past + dt - keep_s:], win_new], axis=1)
            lf_p3 = lf_p[:, MISC_FF:MISC_FF + FOX_HEADS].reshape(nb, t, FOX_HEADS)
            lf_s3 = lf_s[:, MISC_FF:MISC_FF + FOX_HEADS].reshape(db, dt, FOX_HEADS)
            ev_p.append((hp[:, :, E_CMP:E_CMP + 256], hp[:, :, E_SLC:E_SLC + 256],
                         hp[:, t - keep_p:, E_WIN:E_WIN + 256], hp[:, :, E_FK:E_FK + 1024], lf_p3))
            ev_s.append((hs[:, :, E_CMP:E_CMP + 256], hs[:, :, E_SLC:E_SLC + 256],
                         win_s, hs[:, :, E_FK:E_FK + 1024], lf_s3))
        else:
            w = jnp.concatenate([w_in_odd[j], jnp.zeros((d, O_W - w_in_odd.shape[2]), w_in_odd.dtype)],
                                axis=1).astype(BF16)
            h = _proj(x, w, rt, width=O_W, rope_full=ODD_ROPE, rope_lo=ODD_ROPE_LO, tm=tm)
            o_p = _dsa_attn(h, n_seq=nb, t=t, tq=tq, n_keep=min(IDX_TOPK, t // 4))
            bias_s = _dsa_index(h, dik_v, pt_flat, layer=j, row0=n_p, n_seq=db, dt=dt, n_pages=n_pages,
                                pp=n_pages, n_keep=min(IDX_TOPK, (lp + dt) // 4))
            o_s = _dsa_sample(h, bias_s, dkv_v, pt_flat, layer=j, row0=n_p, n_seq=db, dt=dt, n_pages=n_pages,
                              pp=pp_big)
            x = _out_ln([o_p], [o_s], [w_out_odd[j].astype(BF16)], x, g0, b0, alpha=alpha, tm=tm)
            wr = jnp.concatenate([moe_router[j], jnp.zeros((d, LANES - N_EXPERTS), F32)], axis=1)
            wr_hi = wr.astype(BF16)
            wr_lo = (wr - wr_hi.astype(F32)).astype(BF16)
            x = _moe_ln(x, wr_hi, wr_lo, moe_gu[j].astype(BF16), moe_down[j].astype(BF16), g1, b1,
                        alpha=alpha, tm=tm_ff, fc=moe_down.shape[2] // 4)
            hp, hs = h[:n_p].reshape(nb, t, O_W), h[n_p:].reshape(db, dt, O_W)
            od_p.append((hp[:, :, O_KV:O_KV + 512], hp[:, :, O_KI:O_KI + IDX_DIM]))
            od_s.append((hs[:, :, O_KV:O_KV + 512], hs[:, :, O_KI:O_KI + IDX_DIM]))

    def stk(lst, idx, tail):
        a = jnp.stack([s[idx] for s in lst], axis=1)
        return a.reshape(a.shape[:3] + tail)

    kv2 = (2, NSA_KV_HEADS, HEAD_DIM)
    fkv = (2, FOX_HEADS, HEAD_DIM)
    dkv = (2, DSA_KV_HEADS, HEAD_DIM)
    return (x[:n_p].reshape(nb, t, d), x[n_p:].reshape(db, dt, d),
            stk(ev_p, 0, kv2), stk(ev_s, 0, kv2), stk(ev_p, 1, kv2), stk(ev_s, 1, kv2),
            stk(ev_p, 2, kv2), stk(ev_s, 2, kv2), stk(ev_p, 3, fkv), stk(ev_s, 3, fkv),
            stk(ev_p, 4, (FOX_HEADS,)), stk(ev_s, 4, (FOX_HEADS,)),
            stk(od_p, 0, dkv), stk(od_s, 0, dkv), stk(od_p, 1, (IDX_DIM,)), stk(od_s, 1, (IDX_DIM,)))
```

```python
import functools
import math

import jax
import jax.numpy as jnp
from jax import lax
from jax.experimental import pallas as pl
from jax.experimental.pallas import tpu as pltpu

F32 = jnp.float32
BF16 = jnp.bfloat16
NEG_INF = float("-inf")

HEAD_DIM = 64
ROT_DIM = HEAD_DIM // 4
ROPE_THETA = 500000.0
NSA_HEADS = 8
NSA_KV_HEADS = 2
NSA_GROUP = NSA_HEADS // NSA_KV_HEADS
CMP_BLOCK = 32
SLC_BLOCK = 64
SLC_TOPN = 16
WINDOW = 512
FOX_HEADS = 8
DSA_HEADS = 16
DSA_KV_HEADS = 4
DSA_GROUP = DSA_HEADS // DSA_KV_HEADS
IDX_HEADS = 8
IDX_DIM = 64
IDX_TOPK = 256
N_EXPERTS = 8
LN_EPS = 1e-5
PAGE_SIZE = 128
LANES = 128

E_NQ, E_FQ, E_FK, E_FV, E_CMP, E_SLC, E_WIN, E_MISC, E_W = 0, 512, 1024, 1536, 2048, 2304, 2560, 2816, 2944
MISC_GATES = 0
MISC_FF = 24
EVEN_ROPE = (0, 1, 2, 3, E_SLC // LANES, E_WIN // LANES)
O_Q, O_KV, O_QI, O_KI, O_W = 0, 1024, 1536, 2048, 2176
ODD_ROPE = tuple(range(0, 10)) + tuple(range(12, 16))
ODD_ROPE_LO = (O_KI // LANES,)

SEL_W = 64
KEY_CHUNK = 512
MOE_BLOCK = 256
MOE_TILE = 896
VMEM_LIMIT = 48 * 1024 * 1024


def _dot(a, b):
    return jnp.dot(a, b, preferred_element_type=F32)


def _dot_nt(a, b):
    return lax.dot_general(a, b, (((1,), (1,)), ((), ())), preferred_element_type=F32)


def _iota(shape, dim):
    return lax.broadcasted_iota(jnp.int32, shape, dim)


def _rope128(v, c, s1, s2):
    return v * c + pltpu.roll(v, 8, 1) * s1 + pltpu.roll(v, LANES - 8, 1) * s2


def _split3(x):
    hi = x.astype(BF16).astype(F32)
    r = x - hi
    mid = r.astype(BF16).astype(F32)
    lo = (r - mid).astype(BF16).astype(F32)
    return hi, mid, lo


def _st_reset(m_ref, l_ref, acc_ref):
    m_ref[...] = jnp.full(m_ref.shape, NEG_INF, F32)
    l_ref[...] = jnp.zeros(l_ref.shape, F32)
    acc_ref[...] = jnp.zeros(acc_ref.shape, F32)


def _st_update(m_ref, l_ref, acc_ref, s, pv):
    m_prev = m_ref[...]
    m_new = jnp.maximum(m_prev, jnp.max(s, axis=-1, keepdims=True))
    m_safe = jnp.where(m_new == NEG_INF, 0.0, m_new)
    alpha = jnp.exp(m_prev - m_safe)
    p = jnp.exp(s - m_safe)
    l_ref[...] = alpha * l_ref[...] + jnp.sum(p, axis=-1, keepdims=True)
    acc_ref[...] = alpha * acc_ref[...] + pv(p.astype(BF16))
    m_ref[...] = m_new


def _st_update_blocks(m_ref, l_ref, acc_ref, blocks):
    m_prev = m_ref[...]
    widest = blocks[0][0]
    for s, _ in blocks[1:]:
        widest = jnp.maximum(widest, s)
    m_new = jnp.maximum(m_prev, jnp.max(widest, axis=-1, keepdims=True))
    m_safe = jnp.where(m_new == NEG_INF, 0.0, m_new)
    alpha = jnp.exp(m_prev - m_safe)
    acc = alpha * acc_ref[...]
    total = None
    for s, pv in blocks:
        p = jnp.exp(s - m_safe)
        total = p if total is None else total + p
        acc = acc + pv(p.astype(BF16))
    l_ref[...] = alpha * l_ref[...] + jnp.sum(total, axis=-1, keepdims=True)
    acc_ref[...] = acc
    m_ref[...] = m_new


def _st_finish(l_ref, acc_ref):
    return acc_ref[...] / jnp.maximum(l_ref[...], 1e-30)


def _pad_rows(a, rows):
    if a.shape[0] == rows:
        return a
    return jnp.concatenate([a, jnp.zeros((rows - a.shape[0], a.shape[1]), a.dtype)], axis=0)


def _tile_rows(a, times):
    return jnp.concatenate([a] * times, axis=0)


def _layer_norm(z, g, b):
    mu = jnp.mean(z, axis=-1, keepdims=True)
    d = z - mu
    var = jnp.mean(d * d, axis=-1, keepdims=True)
    return d * lax.rsqrt(var + LN_EPS) * g + b


def _key_chunks(total, ch):
    return [(r0, min(ch, total - r0)) for r0 in range(0, total, ch)]


def _proj_kernel(x_ref, w_ref, rt_ref, o_ref, *, width, rope_full, rope_lo, cw):
    xb = x_ref[...].astype(BF16)
    c, s1, s2 = rt_ref[:, 0:128], rt_ref[:, 128:256], rt_ref[:, 256:384]
    lo = _iota((1, LANES), 1) < HEAD_DIM
    for c0 in range(0, width, cw):
        c1 = min(c0 + cw, width)
        acc = _dot(xb, w_ref[:, c0:c1])
        for blk in range(c0 // LANES, c1 // LANES):
            v = acc[:, blk * LANES - c0:(blk + 1) * LANES - c0]
            if blk in rope_full:
                v = _rope128(v, c, s1, s2)
            elif blk in rope_lo:
                v = _rope128(v, jnp.where(lo, c, 1.0), jnp.where(lo, s1, 0.0), jnp.where(lo, s2, 0.0))
            o_ref[:, blk * LANES:(blk + 1) * LANES] = v


def _proj(x, w, rt, *, width, rope_full, rope_lo, tm):
    n, d = x.shape
    return pl.pallas_call(
        functools.partial(_proj_kernel, width=width, rope_full=rope_full, rope_lo=rope_lo, cw=512),
        grid=(n // tm,),
        in_specs=[pl.BlockSpec((tm, d), lambda i: (i, 0)),
                  pl.BlockSpec((d, width), lambda i: (0, 0)),
                  pl.BlockSpec((tm, 3 * LANES), lambda i: (i, 0))],
        out_specs=pl.BlockSpec((tm, width), lambda i: (i, 0)),
        out_shape=jax.ShapeDtypeStruct((n, width), F32),
        compiler_params=pltpu.CompilerParams(dimension_semantics=("arbitrary",), vmem_limit_bytes=VMEM_LIMIT),
        name="proj",
    )(x, w, rt)


def _out_ln_kernel(*refs, n_parts, n_prompt_tiles, alpha):
    op = refs[0:n_parts]
    os_ = refs[n_parts:2 * n_parts]
    ws = refs[2 * n_parts:3 * n_parts]
    x_ref, g_ref, b_ref, o_ref = refs[3 * n_parts:]
    i = pl.program_id(0)

    def compute(parts):
        y = None
        for o, w in zip(parts, ws):
            t = _dot(o[...].astype(BF16), w[...])
            y = t if y is None else y + t
        o_ref[...] = _layer_norm(alpha * x_ref[...] + y, g_ref[...], b_ref[...])

    @pl.when(i < n_prompt_tiles)
    def _():
        compute(op)

    @pl.when(i >= n_prompt_tiles)
    def _():
        compute(os_)


def _out_ln(o_prompt, o_sample, w_parts, x, g, b, *, alpha, tm):
    n, d = x.shape
    npt = o_prompt[0].shape[0] // tm
    nst = o_sample[0].shape[0] // tm
    k = len(w_parts)
    in_specs = []
    for o in o_prompt:
        in_specs.append(pl.BlockSpec((tm, o.shape[1]), lambda i: (jnp.minimum(i, npt - 1), 0)))
    for o in o_sample:
        in_specs.append(pl.BlockSpec((tm, o.shape[1]), lambda i: (jnp.maximum(i - npt, 0), 0)))
    for w in w_parts:
        in_specs.append(pl.BlockSpec(w.shape, lambda i: (0, 0)))
    in_specs += [pl.BlockSpec((tm, d), lambda i: (i, 0)),
                 pl.BlockSpec((1, d), lambda i: (0, 0)),
                 pl.BlockSpec((1, d), lambda i: (0, 0))]
    assert npt + nst == n // tm
    return pl.pallas_call(
        functools.partial(_out_ln_kernel, n_parts=k, n_prompt_tiles=npt, alpha=alpha),
        grid=(n // tm,),
        in_specs=in_specs,
        out_specs=pl.BlockSpec((tm, d), lambda i: (i, 0)),
        out_shape=jax.ShapeDtypeStruct((n, d), F32),
        compiler_params=pltpu.CompilerParams(dimension_semantics=("arbitrary",), vmem_limit_bytes=VMEM_LIMIT),
        name="out_ln",
    )(*o_prompt, *o_sample, *w_parts, x, g, b)


def _ffn_ln_kernel(x_ref, wa_ref, wg_ref, wd_ref, g_ref, b_ref, o_ref, acc_ref, *, alpha, n_chunks):
    c = pl.program_id(1)

    @pl.when(c == 0)
    def _():
        acc_ref[...] = jnp.zeros_like(acc_ref)

    xb = x_ref[...].astype(BF16)
    a = _dot(xb, wa_ref[...])
    gg = _dot(xb, wg_ref[...])
    hmid = (jax.nn.silu(a) * gg).astype(BF16)
    acc_ref[...] += _dot(hmid, wd_ref[...])

    @pl.when(c == n_chunks - 1)
    def _():
        o_ref[...] = _layer_norm(alpha * x_ref[...] + acc_ref[...], g_ref[...], b_ref[...])


def _ffn_ln(x, w_gu, w_down, g, b, *, alpha, tm, fc):
    n, d = x.shape
    dff = w_down.shape[0]
    nch = dff // fc
    return pl.pallas_call(
        functools.partial(_ffn_ln_kernel, alpha=alpha, n_chunks=nch),
        grid=(n // tm, nch),
        in_specs=[pl.BlockSpec((tm, d), lambda i, c: (i, 0)),
                  pl.BlockSpec((d, fc), lambda i, c: (0, c)),
                  pl.BlockSpec((d, fc), lambda i, c: (0, c + nch)),
                  pl.BlockSpec((fc, d), lambda i, c: (c, 0)),
                  pl.BlockSpec((1, d), lambda i, c: (0, 0)),
                  pl.BlockSpec((1, d), lambda i, c: (0, 0))],
        out_specs=pl.BlockSpec((tm, d), lambda i, c: (i, 0)),
        out_shape=jax.ShapeDtypeStruct((n, d), F32),
        scratch_shapes=[pltpu.VMEM((tm, d), F32)],
        compiler_params=pltpu.CompilerParams(dimension_semantics=("arbitrary", "arbitrary"),
                                             vmem_limit_bytes=VMEM_LIMIT),
        name="ffn_ln",
    )(x, w_gu, w_gu, w_down, g, b)


def _moe_ln_kernel(x_ref, wrh_ref, wrl_ref, wat_ref, wgt_ref, wdt_ref, g_ref, b_ref, o_ref,
                   xt_s, acct_s, route_s, routet_s, xct_s, yct_s, *, alpha, n_chunks, n_valid, blk):
    i = pl.program_id(0)
    e = pl.program_id(1)
    c = pl.program_id(2)
    tmm = x_ref.shape[0]
    lane = _iota((1, LANES), 1).astype(F32)
    ef = e.astype(F32)

    @pl.when((e == 0) & (c == 0))
    def _():
        x = x_ref[...]
        xh = x.astype(BF16)
        xl = (x - xh.astype(F32)).astype(BF16)
        logits = _dot(xh, wrh_ref[...]) + _dot(xl, wrh_ref[...]) + _dot(xh, wrl_ref[...])
        lg = jnp.where(lane < N_EXPERTS, logits, NEG_INF)
        m1 = jnp.max(lg, axis=-1, keepdims=True)
        i1 = jnp.min(jnp.where(lg == m1, lane, float(LANES)), axis=-1, keepdims=True)
        lg2 = jnp.where(lane == i1, NEG_INF, lg)
        m2 = jnp.max(lg2, axis=-1, keepdims=True)
        i2 = jnp.min(jnp.where(lg2 == m2, lane, float(LANES)), axis=-1, keepdims=True)
        e2 = jnp.exp(m2 - m1)
        den = 1.0 + e2
        comb = jnp.where(lane == i1, 1.0 / den, 0.0) + jnp.where(lane == i2, e2 / den, 0.0)
        real = (i * tmm + _iota((tmm, 1), 0)) < n_valid
        sel = jnp.where(real & ((lane == i1) | (lane == i2)), 1.0, 0.0)
        before = jnp.where(_iota((tmm, tmm), 0) > _iota((tmm, tmm), 1), 1.0, 0.0).astype(BF16)
        rank = _dot(before, sel.astype(BF16))
        for idx, val in enumerate((comb, sel, rank)):
            route_s[idx] = val
            routet_s[idx] = val.T
        xt_s[...] = x.T.astype(BF16)
        acct_s[...] = jnp.zeros_like(acct_s)

    def pick(ref3, idx):
        return jnp.sum(jnp.where(lane == ef, ref3[idx], 0.0), axis=-1, keepdims=True)

    count = jnp.sum(jnp.where(lane == ef, route_s[1], 0.0))
    n_blocks = (count.astype(jnp.int32) + blk - 1) // blk

    @pl.when(c == 0)
    def _():
        sel_col, rank_col = pick(route_s, 1), pick(route_s, 2)

        def compact(k, carry):
            slot = (k * blk + _iota((1, blk), 1)).astype(F32)
            onehot = jnp.where(rank_col == slot, sel_col, 0.0).astype(BF16)
            xct_s[k] = _dot(xt_s[...], onehot).astype(BF16)
            yct_s[k] = jnp.zeros(yct_s.shape[1:], F32)
            return carry

        lax.fori_loop(0, n_blocks, compact, 0)

    def expert(k, carry):
        xc = xct_s[k]
        a = _dot(wat_ref[...], xc)
        gg = _dot(wgt_ref[...], xc)
        yct_s[k] += _dot(wdt_ref[...], (jax.nn.silu(a) * gg).astype(BF16))
        return carry

    lax.fori_loop(0, n_blocks, expert, 0)

    @pl.when(c == n_chunks - 1)
    def _():
        w_row = routet_s[0, pl.ds(e, 1), :]
        sel_row = routet_s[1, pl.ds(e, 1), :]
        rank_row = routet_s[2, pl.ds(e, 1), :]

        def scatter(k, carry):
            slot = (k * blk + _iota((blk, 1), 0)).astype(F32)
            onehot = jnp.where(rank_row == slot, sel_row, 0.0).astype(BF16)
            y = yct_s[k]
            yh = y.astype(BF16)
            yl = (y - yh.astype(F32)).astype(BF16)
            acct_s[...] += w_row * (_dot(yh, onehot) + _dot(yl, onehot))
            return carry

        lax.fori_loop(0, n_blocks, scatter, 0)

    @pl.when((e == N_EXPERTS - 1) & (c == n_chunks - 1))
    def _():
        o_ref[...] = _layer_norm(alpha * x_ref[...] + acct_s[...].T, g_ref[...], b_ref[...])


def _moe_ln(x, wr_hi, wr_lo, w_gu_t, w_down_t, g, b, *, alpha, tmm, fc, blk):
    n, d = x.shape
    dff = w_down_t.shape[2]
    nch = dff // fc
    n_tiles = -(-n // tmm)
    n_pad = n_tiles * tmm
    xp = jnp.pad(x, ((0, n_pad - n), (0, 0)))
    max_blocks = -(-tmm // blk)
    out = pl.pallas_call(
        functools.partial(_moe_ln_kernel, alpha=alpha, n_chunks=nch, n_valid=n, blk=blk),
        grid=(n_tiles, N_EXPERTS, nch),
        in_specs=[pl.BlockSpec((tmm, d), lambda i, e, c: (i, 0)),
                  pl.BlockSpec((d, LANES), lambda i, e, c: (0, 0)),
                  pl.BlockSpec((d, LANES), lambda i, e, c: (0, 0)),
                  pl.BlockSpec((None, fc, d), lambda i, e, c: (e, c, 0)),
                  pl.BlockSpec((None, fc, d), lambda i, e, c: (e, c + nch, 0)),
                  pl.BlockSpec((None, d, fc), lambda i, e, c: (e, 0, c)),
                  pl.BlockSpec((1, d), lambda i, e, c: (0, 0)),
                  pl.BlockSpec((1, d), lambda i, e, c: (0, 0))],
        out_specs=pl.BlockSpec((tmm, d), lambda i, e, c: (i, 0)),
        out_shape=jax.ShapeDtypeStruct((n_pad, d), F32),
        scratch_shapes=[pltpu.VMEM((d, tmm), BF16), pltpu.VMEM((d, tmm), F32),
                        pltpu.VMEM((3, tmm, LANES), F32), pltpu.VMEM((3, LANES, tmm), F32),
                        pltpu.VMEM((max_blocks, d, blk), BF16), pltpu.VMEM((max_blocks, d, blk), F32)],
        compiler_params=pltpu.CompilerParams(dimension_semantics=("arbitrary", "arbitrary", "arbitrary"),
                                             vmem_limit_bytes=VMEM_LIMIT),
        name="moe_ln",
    )(xp, wr_hi, wr_lo, w_gu_t, w_gu_t, w_down_t, g, b)
    return out[:n]


def _page_specs(block, n_pages, pp, layer):
    tail = (0,) * (len(block) - 2)
    return [pl.BlockSpec(block, lambda s, p, pt, k=k: (pt[s * n_pages + p * pp + k], layer) + tail)
            for k in range(pp)]


def _log_sigmoid(z):
    return jnp.minimum(z, 0.0) - jnp.log1p(jnp.exp(-jnp.abs(z)))


def _forget_lanes(x):
    lane = _iota((1, LANES), 1)
    return jnp.where((lane >= MISC_FF) & (lane < MISC_FF + FOX_HEADS), x, 0.0)


def _cum_small(lf):
    rows = lf.shape[0]
    ri = _iota((rows, LANES), 0)
    cum = jnp.zeros((rows, LANES), F32)
    for t in range(rows):
        cum = cum + jnp.where(ri >= t, lf[t:t + 1, :], 0.0)
    return cum


def _cmp_summaries(kc_ref, vc_ref, pe_ref, phi_ref, crt_ref, nc):
    acc_k = jnp.zeros((nc, LANES), F32)
    acc_v = jnp.zeros((nc, LANES), F32)
    for l in range(CMP_BLOCK):
        rk = kc_ref[pl.ds(l, nc, stride=CMP_BLOCK), :] + pe_ref[0, l:l + 1, :]
        acc_k = acc_k + _dot(rk.astype(BF16), phi_ref[0, l])
        rv = vc_ref[pl.ds(l, nc, stride=CMP_BLOCK), :] + pe_ref[1, l:l + 1, :]
        acc_v = acc_v + _dot(rv.astype(BF16), phi_ref[1, l])
    ck = _rope128(acc_k, crt_ref[:, 0:128], crt_ref[:, 128:256], crt_ref[:, 256:384])
    return ck, acc_v


def _nsa_cmp_select(q4, ck_e, ck_o, cv_e, cv_o, qpos, *, tq, nc, ns, topn):
    qpos4 = _tile_rows(qpos, NSA_GROUP)
    blk = _iota((1, SEL_W), 1)
    n_even, n_odd = (nc + 1) // 2, nc // 2
    ok_e = ((2 * blk + 1) * CMP_BLOCK - 1 <= qpos4) & (blk < n_even)
    ok_o = ((2 * blk + 2) * CMP_BLOCK - 1 <= qpos4) & (blk < n_odd)
    s_e = jnp.where(ok_e, _dot_nt(q4, ck_e), NEG_INF)
    s_o = jnp.where(ok_o, _dot_nt(q4, ck_o), NEG_INF)
    m = jnp.maximum(jnp.max(s_e, axis=-1, keepdims=True), jnp.max(s_o, axis=-1, keepdims=True))
    m = jnp.where(m == NEG_INF, 0.0, m)
    p_e = jnp.exp(s_e - m)
    p_o = jnp.exp(s_o - m)
    den = jnp.sum(p_e, axis=-1, keepdims=True) + jnp.sum(p_o, axis=-1, keepdims=True)
    inv = 1.0 / jnp.maximum(den, 1e-30)
    p_e = p_e * inv
    p_o = p_o * inv
    o_cmp = _dot(p_e.astype(BF16), cv_e) + _dot(p_o.astype(BF16), cv_o)
    pp = p_e + p_o
    imp = pp[0:tq]
    for r in range(1, NSA_GROUP):
        imp = imp + pp[r * tq:(r + 1) * tq]
    valid = blk * SLC_BLOCK <= qpos
    forced = (blk == 0) | (blk == (qpos >> 6))
    score = jnp.where(forced, jnp.inf, jnp.where(valid, imp, NEG_INF))
    score = jnp.where(blk < ns, score, NEG_INF)
    rank = jnp.zeros((tq, SEL_W), F32)
    for b2 in range(ns):
        col = score[:, b2:b2 + 1]
        beats = (col > score) | ((col == score) & (blk > b2))
        rank = rank + jnp.where(beats, 1.0, 0.0)
    sel = jnp.where(rank < topn, 1.0, 0.0)
    return o_cmp, sel


def _block_expand(pos0, n):
    return jnp.where(_iota((SEL_W, n), 0) == ((pos0 + _iota((SEL_W, n), 1)) >> 6), 1.0, 0.0).astype(BF16)


def _nsa_q4(hq_ref, g):
    q4 = jnp.concatenate([hq_ref[:, (NSA_GROUP * g + r) * 64:(NSA_GROUP * g + r + 1) * 64]
                          for r in range(NSA_GROUP)], axis=0)
    return (q4 * 0.125).astype(BF16)


def _nsa_write(o_ref, gates, g, tq, o_cmp, o_slc, o_win):
    for r in range(NSA_GROUP):
        hh = NSA_GROUP * g + r
        rs = slice(r * tq, (r + 1) * tq)
        c0 = MISC_GATES + 3 * hh
        o_ref[:, hh * 64:(hh + 1) * 64] = (gates[:, c0:c0 + 1] * o_cmp[rs] + gates[:, c0 + 1:c0 + 2] * o_slc[rs]
                                           + gates[:, c0 + 2:c0 + 3] * o_win[rs])


def _topk_bias(key_ref, n_keep, idx, count, row_shape):
    kk = float(n_keep)
    thr = jnp.where(count(key_ref[...] >= 0) >= kk, jnp.int32(0), jnp.int32(-2147483648))
    thr = thr + jnp.zeros(row_shape, jnp.int32)

    def descend(b, thr):
        cand = thr + (jnp.int32(1) << (30 - b))
        return jnp.where(count(key_ref[...] >= cand) >= kk, cand, thr)

    thr = lax.fori_loop(0, 31, descend, thr)
    return thr, jnp.max(count(key_ref[...] >= thr)) > kk


def _topk_bias_store(out_ref, key_ref, n_keep, idx, count, row_shape):
    kk = float(n_keep)
    thr, has_ties = _topk_bias(key_ref, n_keep, idx, count, row_shape)

    @pl.when(jnp.logical_not(has_ties))
    def _():
        out_ref[...] = jnp.where(key_ref[...] >= thr, 0.0, NEG_INF)

    @pl.when(has_ties)
    def _():
        keys = key_ref[...]
        gt = keys > thr
        eq = keys == thr
        need = kk - count(gt)

        def widen(b, bound):
            cand = bound + (jnp.int32(1) << (15 - b))
            return jnp.where(count(eq & (idx < cand)) <= need, cand, bound)

        bound = lax.fori_loop(0, 16, widen, jnp.zeros(row_shape, jnp.int32))
        out_ref[...] = jnp.where(gt | (eq & (idx < bound)), 0.0, NEG_INF)


def _sortable_key(score):
    bits = pltpu.bitcast(score, jnp.int32)
    return bits ^ ((bits >> 31) & jnp.int32(0x7FFFFFFF))


def _cum_tri(lf, carry):
    rows = lf.shape[0]
    tri = jnp.where(_iota((rows, rows), 0) >= _iota((rows, rows), 1), 1.0, 0.0).astype(BF16)
    hi, mid, lo = _split3(lf)
    return _dot(tri, hi.astype(BF16)) + _dot(tri, mid.astype(BF16)) + _dot(tri, lo.astype(BF16)) + carry


def _fox_aug_cols(cum3, hh):
    chi, cmid, clo = [part[:, MISC_FF + hh:MISC_FF + hh + 1] for part in cum3]
    l64 = _iota((1, HEAD_DIM), 1)
    qx = jnp.where(l64 == 0, chi, jnp.where(l64 == 1, cmid, jnp.where(l64 == 2, clo, jnp.where(l64 < 6, 1.0, 0.0))))
    kx = jnp.where(l64 < 3, 1.0,
                   jnp.where(l64 == 3, -chi, jnp.where(l64 == 4, -cmid, jnp.where(l64 == 5, -clo, 0.0))))
    return qx, kx


def _fox_prep_kernel(fq_ref, fk_ref, misc_ref, fb_ref, qa_ref, ka_ref, lf_ref, carry_ref):
    j = pl.program_id(1)

    @pl.when(j == 0)
    def _():
        carry_ref[...] = jnp.zeros_like(carry_ref)

    lf = _forget_lanes(_log_sigmoid(misc_ref[...] + fb_ref[...]))
    lf_ref[...] = lf
    cum = _cum_tri(lf, carry_ref[...])
    rows = lf.shape[0]
    carry_ref[...] = cum[rows - 1:rows, :]
    cum3 = _split3(cum)
    for hh in range(FOX_HEADS):
        qx, kx = _fox_aug_cols(cum3, hh)
        qa_ref[:, hh * 128:hh * 128 + 64] = (fq_ref[:, hh * 64:(hh + 1) * 64] * 0.125).astype(BF16)
        qa_ref[:, hh * 128 + 64:(hh + 1) * 128] = qx.astype(BF16)
        ka_ref[:, hh * 128:hh * 128 + 64] = fk_ref[:, hh * 64:(hh + 1) * 64].astype(BF16)
        ka_ref[:, hh * 128 + 64:(hh + 1) * 128] = kx.astype(BF16)


def _fox_prep(h, fb, *, n_seq, t, rows):
    nr = t // rows
    n = n_seq * t
    return pl.pallas_call(
        _fox_prep_kernel,
        grid=(n_seq, nr),
        in_specs=[pl.BlockSpec((rows, 512), lambda s, j: (s * nr + j, E_FQ // 512)),
                  pl.BlockSpec((rows, 512), lambda s, j: (s * nr + j, E_FK // 512)),
                  pl.BlockSpec((rows, LANES), lambda s, j: (s * nr + j, E_MISC // LANES)),
                  pl.BlockSpec((1, LANES), lambda s, j: (0, 0))],
        out_specs=[pl.BlockSpec((rows, FOX_HEADS * 128), lambda s, j: (s * nr + j, 0)),
                   pl.BlockSpec((rows, FOX_HEADS * 128), lambda s, j: (s * nr + j, 0)),
                   pl.BlockSpec((rows, LANES), lambda s, j: (s * nr + j, 0))],
        out_shape=[jax.ShapeDtypeStruct((n, FOX_HEADS * 128), BF16),
                   jax.ShapeDtypeStruct((n, FOX_HEADS * 128), BF16),
                   jax.ShapeDtypeStruct((n, LANES), F32)],
        scratch_shapes=[pltpu.VMEM((1, LANES), F32)],
        compiler_params=pltpu.CompilerParams(dimension_semantics=("arbitrary", "arbitrary")),
        name="fox_prep",
    )(h, h, h, fb)


def _fox_attn_kernel(qa_ref, ka_ref, v_ref, o_ref, m_s, l_s, acc_s, *, tq, t, hps):
    i = pl.program_id(2)
    qpos = i * tq + _iota((tq, 1), 0)
    q_end = i * tq + tq - 1
    for hh in range(hps):
        _st_reset(m_s.at[hh], l_s.at[hh], acc_s.at[hh])
    for r0, n in _key_chunks(t, KEY_CHUNK):
        @pl.when(r0 <= q_end)
        def _():
            bias = jnp.where((r0 + _iota((1, n), 1)) <= qpos, 0.0, NEG_INF)
            for hh in range(hps):
                k = ka_ref[r0:r0 + n, hh * 128:(hh + 1) * 128]
                v = v_ref[r0:r0 + n, hh * 64:(hh + 1) * 64].astype(BF16)
                s = _dot_nt(qa_ref[:, hh * 128:(hh + 1) * 128], k) + bias
                _st_update(m_s.at[hh], l_s.at[hh], acc_s.at[hh], s, lambda pb, v=v: _dot(pb, v))
    for hh in range(hps):
        o_ref[:, hh * 64:(hh + 1) * 64] = _st_finish(l_s.at[hh], acc_s.at[hh])


def _fox_attn(qa, ka, h, *, n_seq, t, tq, hps):
    nq = t // tq
    n = n_seq * t
    return pl.pallas_call(
        functools.partial(_fox_attn_kernel, tq=tq, t=t, hps=hps),
        grid=(n_seq, FOX_HEADS // hps, nq),
        in_specs=[pl.BlockSpec((tq, hps * 128), lambda s, p, i: (s * nq + i, p)),
                  pl.BlockSpec((t, hps * 128), lambda s, p, i: (s, p)),
                  pl.BlockSpec((t, hps * 64), lambda s, p, i: (s, E_FV // (hps * 64) + p))],
        out_specs=pl.BlockSpec((tq, hps * 64), lambda s, p, i: (s * nq + i, p)),
        out_shape=jax.ShapeDtypeStruct((n, FOX_HEADS * HEAD_DIM), F32),
        scratch_shapes=[pltpu.VMEM((hps, tq, 1), F32), pltpu.VMEM((hps, tq, 1), F32),
                        pltpu.VMEM((hps, tq, HEAD_DIM), F32)],
        compiler_params=pltpu.CompilerParams(dimension_semantics=("arbitrary", "arbitrary", "arbitrary"),
                                             vmem_limit_bytes=VMEM_LIMIT),
        name="fox_attn",
    )(qa, ka, h)


def _nsa_kernel(hq_ref, misc_ref, kc_ref, vc_ref, slc_ref, win_ref, pe_ref, phi_ref, crt_ref, o_ref,
                ck_s, cv_s, m_s, l_s, acc_s, *, tq, t, nc, ns, topn):
    i = pl.program_id(1)

    @pl.when(i == 0)
    def _():
        ck, cv = _cmp_summaries(kc_ref, vc_ref, pe_ref, phi_ref, crt_ref, nc)
        ck_s[...] = jnp.zeros_like(ck_s)
        cv_s[...] = jnp.zeros_like(cv_s)
        ck_s[0:nc, :] = ck
        cv_s[0:nc, :] = cv

    qs = i * tq
    q_end = qs + tq - 1
    qpos = qs + _iota((tq, 1), 0)
    gates = jax.nn.sigmoid(misc_ref[...])
    n_win = min(WINDOW + tq, t)
    w0 = pl.multiple_of(jnp.maximum(qs + tq - n_win, 0), 8)

    groups = range(NSA_KV_HEADS)
    kcols = [slice(g * 64, (g + 1) * 64) for g in groups]
    vcols = [slice(128 + g * 64, 128 + (g + 1) * 64) for g in groups]
    q4s = [_nsa_q4(hq_ref, g) for g in groups]
    o_cmps, sels = [], []
    for g in groups:
        ck_e = ck_s[pl.ds(0, SEL_W, stride=2), :][:, kcols[g]].astype(BF16)
        ck_o = ck_s[pl.ds(1, SEL_W, stride=2), :][:, kcols[g]].astype(BF16)
        cv_e = cv_s[pl.ds(0, SEL_W, stride=2), :][:, kcols[g]].astype(BF16)
        cv_o = cv_s[pl.ds(1, SEL_W, stride=2), :][:, kcols[g]].astype(BF16)
        o_cmp, sel = _nsa_cmp_select(q4s[g], ck_e, ck_o, cv_e, cv_o, qpos, tq=tq, nc=nc, ns=ns, topn=topn)
        o_cmps.append(o_cmp)
        sels.append(sel.astype(BF16))
        _st_reset(m_s.at[g], l_s.at[g], acc_s.at[g])

    for r0, n in _key_chunks(t, KEY_CHUNK):
        @pl.when(r0 <= q_end)
        def _():
            expand = _block_expand(r0, n)
            causal = (r0 + _iota((1, n), 1)) <= qpos
            for g in groups:
                k = slc_ref[r0:r0 + n, kcols[g]].astype(BF16)
                v = slc_ref[r0:r0 + n, vcols[g]].astype(BF16)
                bias = jnp.where((_dot(sels[g], expand) > 0.5) & causal, 0.0, NEG_INF)
                s = _dot_nt(q4s[g], k) + _tile_rows(bias, NSA_GROUP)
                _st_update(m_s.at[g], l_s.at[g], acc_s.at[g], s, lambda pb, v=v: _dot(pb, v))

    kpos = w0 + _iota((1, n_win), 1)
    wbias = _tile_rows(jnp.where((kpos <= qpos) & (kpos > qpos - WINDOW), 0.0, NEG_INF), NSA_GROUP)
    for g in groups:
        o_slc = _st_finish(l_s.at[g], acc_s.at[g])
        k = win_ref[pl.ds(w0, n_win), kcols[g]].astype(BF16)
        v = win_ref[pl.ds(w0, n_win), vcols[g]].astype(BF16)
        s = _dot_nt(q4s[g], k) + wbias
        p = jnp.exp(s - jnp.max(s, axis=-1, keepdims=True))
        o_win = _dot(p.astype(BF16), v) / jnp.maximum(jnp.sum(p, axis=-1, keepdims=True), 1e-30)
        _nsa_write(o_ref, gates, g, tq, o_cmps[g], o_slc, o_win)


def _nsa_attn(h, pe, phi, crt, *, n_seq, t, tq, topn):
    nq = t // tq
    n = n_seq * t
    nc = t // CMP_BLOCK
    ns = -(-t // SLC_BLOCK)
    assert ns <= SEL_W and nc <= 2 * SEL_W
    rows4 = NSA_GROUP * tq

    def qblk(col):
        return lambda s, i: (s * nq + i, col)

    def kblk(col):
        return lambda s, i: (s, col)

    return pl.pallas_call(
        functools.partial(_nsa_kernel, tq=tq, t=t, nc=nc, ns=ns, topn=min(topn, ns)),
        grid=(n_seq, nq),
        in_specs=[pl.BlockSpec((tq, 512), qblk(E_NQ // 512)),
                  pl.BlockSpec((tq, LANES), qblk(E_MISC // LANES)),
                  pl.BlockSpec((t, LANES), kblk(E_CMP // LANES)),
                  pl.BlockSpec((t, LANES), kblk(E_CMP // LANES + 1)),
                  pl.BlockSpec((t, 256), kblk(E_SLC // 256)),
                  pl.BlockSpec((t, 256), kblk(E_WIN // 256)),
                  pl.BlockSpec(pe.shape, lambda s, i: (0, 0, 0)),
                  pl.BlockSpec(phi.shape, lambda s, i: (0, 0, 0, 0)),
                  pl.BlockSpec(crt.shape, lambda s, i: (0, 0))],
        out_specs=pl.BlockSpec((tq, NSA_HEADS * HEAD_DIM), lambda s, i: (s * nq + i, 0)),
        out_shape=jax.ShapeDtypeStruct((n, NSA_HEADS * HEAD_DIM), F32),
        scratch_shapes=[pltpu.VMEM((2 * SEL_W, LANES), F32), pltpu.VMEM((2 * SEL_W, LANES), F32),
                        pltpu.VMEM((NSA_KV_HEADS, rows4, 1), F32), pltpu.VMEM((NSA_KV_HEADS, rows4, 1), F32),
                        pltpu.VMEM((NSA_KV_HEADS, rows4, HEAD_DIM), F32)],
        compiler_params=pltpu.CompilerParams(dimension_semantics=("arbitrary", "arbitrary"),
                                             vmem_limit_bytes=VMEM_LIMIT),
        name="nsa_attn",
    )(h, h, h, h, h, h, pe, phi, crt)


def _dsa_kernel(q_ref, qi_ref, kw_ref, kv_ref, ki_ref, o_ref, sc_s, key_s, m_s, l_s, acc_s, *, tq, t, n_keep):
    i = pl.program_id(1)
    qpos = i * tq + _iota((tq, 1), 0)
    q_end = i * tq + tq - 1
    chunks = _key_chunks(t, KEY_CHUNK)

    sc_s[...] = jnp.full(sc_s.shape, NEG_INF, F32)
    wi = kw_ref[:, IDX_DIM:IDX_DIM + IDX_HEADS] * (IDX_HEADS ** -0.5)
    qi = [(qi_ref[:, hh * IDX_DIM:(hh + 1) * IDX_DIM] * (IDX_DIM ** -0.5)).astype(BF16) for hh in range(IDX_HEADS)]
    for r0, n in chunks:
        @pl.when(r0 <= q_end)
        def _():
            ki = ki_ref[r0:r0 + n, 0:IDX_DIM].astype(BF16)
            sc = jnp.zeros((tq, n), F32)
            for hh in range(IDX_HEADS):
                sc = sc + wi[:, hh:hh + 1] * jnp.maximum(_dot_nt(qi[hh], ki), 0.0)
            ok = (r0 + _iota((1, n), 1)) <= qpos
            sc_s[:, r0:r0 + n] = jnp.where(ok, sc + 0.0, NEG_INF)

    key_s[...] = _sortable_key(sc_s[...])
    _topk_bias_store(sc_s, key_s, n_keep, _iota((1, t), 1),
                     lambda mask: jnp.sum(jnp.where(mask, 1.0, 0.0), axis=-1, keepdims=True), (tq, 1))

    q4s = []
    for kh in range(DSA_KV_HEADS):
        q4 = jnp.concatenate([q_ref[:, (DSA_GROUP * kh + r) * 64:(DSA_GROUP * kh + r + 1) * 64]
                              for r in range(DSA_GROUP)], axis=0)
        q4s.append((q4 * 0.125).astype(BF16))
        _st_reset(m_s.at[kh], l_s.at[kh], acc_s.at[kh])
    for r0, n in chunks:
        @pl.when(r0 <= q_end)
        def _():
            ok = (r0 + _iota((1, n), 1)) <= qpos
            bias4 = _tile_rows(jnp.where(ok, sc_s[:, r0:r0 + n], NEG_INF), DSA_GROUP)
            for kh in range(DSA_KV_HEADS):
                k = kv_ref[r0:r0 + n, kh * 64:(kh + 1) * 64].astype(BF16)
                v = kv_ref[r0:r0 + n, 256 + kh * 64:256 + (kh + 1) * 64].astype(BF16)
                _st_update(m_s.at[kh], l_s.at[kh], acc_s.at[kh], _dot_nt(q4s[kh], k) + bias4,
                           lambda pb, v=v: _dot(pb, v))
    for kh in range(DSA_KV_HEADS):
        o = _st_finish(l_s.at[kh], acc_s.at[kh])
        for r in range(DSA_GROUP):
            hh = DSA_GROUP * kh + r
            o_ref[:, hh * 64:(hh + 1) * 64] = o[r * tq:(r + 1) * tq]


def _dsa_attn(h, *, n_seq, t, tq, n_keep):
    nq = t // tq
    n = n_seq * t
    rows4 = DSA_GROUP * tq
    assert t % LANES == 0 and t <= 1 << 16

    def qblk(w, col):
        return pl.BlockSpec((tq, w), lambda s, i: (s * nq + i, col))

    return pl.pallas_call(
        functools.partial(_dsa_kernel, tq=tq, t=t, n_keep=n_keep),
        grid=(n_seq, nq),
        in_specs=[qblk(1024, O_Q // 1024), qblk(512, O_QI // 512), qblk(LANES, O_KI // LANES),
                  pl.BlockSpec((t, 512), lambda s, i: (s, O_KV // 512)),
                  pl.BlockSpec((t, LANES), lambda s, i: (s, O_KI // LANES))],
        out_specs=pl.BlockSpec((tq, DSA_HEADS * HEAD_DIM), lambda s, i: (s * nq + i, 0)),
        out_shape=jax.ShapeDtypeStruct((n, DSA_HEADS * HEAD_DIM), F32),
        scratch_shapes=[pltpu.VMEM((tq, t), F32), pltpu.VMEM((tq, t), jnp.int32),
                        pltpu.VMEM((DSA_KV_HEADS, rows4, 1), F32), pltpu.VMEM((DSA_KV_HEADS, rows4, 1), F32),
                        pltpu.VMEM((DSA_KV_HEADS, rows4, HEAD_DIM), F32)],
        compiler_params=pltpu.CompilerParams(dimension_semantics=("arbitrary", "arbitrary"),
                                             vmem_limit_bytes=VMEM_LIMIT),
        name="dsa_attn",
    )(h, h, h, h, h)


def _cmp_sample_kernel(pt_ref, *refs, pp, n_steps, nc):
    slabs = refs[:pp]
    pe_ref, phi_ref, crt_ref, o_ref, k_s, v_s, c_s = refs[pp:]
    p = pl.program_id(1)
    for k in range(pp):
        row0 = pl.multiple_of((p * pp + k) * PAGE_SIZE, PAGE_SIZE)
        for g in range(NSA_KV_HEADS):
            k_s[pl.ds(row0, PAGE_SIZE), g * 64:(g + 1) * 64] = slabs[k][0, g].T
            v_s[pl.ds(row0, PAGE_SIZE), g * 64:(g + 1) * 64] = slabs[k][1, g].T

    @pl.when(p == n_steps - 1)
    def _():
        ck, cv = _cmp_summaries(k_s, v_s, pe_ref, phi_ref, crt_ref, nc)
        for idx, val in ((0, ck), (2, cv)):
            c_s[...] = jnp.zeros_like(c_s)
            c_s[0:nc, :] = val
            o_ref[idx] = c_s[pl.ds(0, SEL_W, stride=2), :]
            o_ref[idx + 1] = c_s[pl.ds(1, SEL_W, stride=2), :]


def _cmp_sample(cmp_v, pt_flat, pe, phi, crt, *, layer, n_seq, n_pages, pp, nc):
    lp = n_pages * PAGE_SIZE
    n_steps = n_pages // pp
    in_specs = _page_specs((None, None, 2, NSA_KV_HEADS, HEAD_DIM, PAGE_SIZE), n_pages, pp, layer)
    in_specs += [pl.BlockSpec(pe.shape, lambda s, p, pt: (0, 0, 0)),
                 pl.BlockSpec(phi.shape, lambda s, p, pt: (0, 0, 0, 0)),
                 pl.BlockSpec(crt.shape, lambda s, p, pt: (0, 0))]
    return pl.pallas_call(
        functools.partial(_cmp_sample_kernel, pp=pp, n_steps=n_steps, nc=nc),
        grid_spec=pltpu.PrefetchScalarGridSpec(
            num_scalar_prefetch=1, grid=(n_seq, n_steps), in_specs=in_specs,
            out_specs=pl.BlockSpec((None, 4, SEL_W, LANES), lambda s, p, pt: (s, 0, 0, 0)),
            scratch_shapes=[pltpu.VMEM((lp, LANES), F32), pltpu.VMEM((lp, LANES), F32),
                            pltpu.VMEM((2 * SEL_W, LANES), F32)]),
        out_shape=jax.ShapeDtypeStruct((n_seq, 4, SEL_W, LANES), F32),
        compiler_params=pltpu.CompilerParams(dimension_semantics=("arbitrary", "arbitrary"),
                                             vmem_limit_bytes=VMEM_LIMIT),
        name="cmp_sample",
    )(pt_flat, *([cmp_v] * pp), pe, phi, crt)


def _nsa_sample_kernel(pt_ref, *refs, pp, n_steps, dt, l_past, w_past, nc, ns, topn):
    hq_ref, misc_ref, slc_ref, win_ref, ckv_ref, winp_ref = refs[:6]
    slabs = refs[6:6 + pp]
    o_ref, sel_s, ocmp_s, m_s, l_s, acc_s = refs[6 + pp:]
    p = pl.program_id(1)
    qpos = l_past + _iota((dt, 1), 0)
    new_pos = l_past + _iota((1, LANES), 1)
    new_ok = (new_pos <= qpos) & (new_pos < l_past + dt)

    @pl.when(p == 0)
    def _():
        for g in range(NSA_KV_HEADS):
            kcol = slice(g * 64, (g + 1) * 64)
            o_cmp, sel = _nsa_cmp_select(
                _nsa_q4(hq_ref, g), ckv_ref[0][:, kcol].astype(BF16), ckv_ref[1][:, kcol].astype(BF16),
                ckv_ref[2][:, kcol].astype(BF16), ckv_ref[3][:, kcol].astype(BF16), qpos,
                tq=dt, nc=nc, ns=ns, topn=topn)
            ocmp_s[g] = o_cmp
            sel_s[g] = sel
            _st_reset(m_s.at[g], l_s.at[g], acc_s.at[g])

    for g in range(NSA_KV_HEADS):
        q4 = _nsa_q4(hq_ref, g)
        sel = sel_s[g].astype(BF16)
        blocks = []
        for k in range(pp):
            kt = slabs[k][0, g].astype(BF16)
            vt = slabs[k][1, g].astype(BF16)
            selx = _dot(sel, _block_expand((p * pp + k) * PAGE_SIZE, PAGE_SIZE))
            bias = jnp.where(selx > 0.5, 0.0, NEG_INF)
            blocks.append((_dot(q4, kt) + _tile_rows(bias, NSA_GROUP), lambda pb, vt=vt: _dot_nt(pb, vt)))
        _st_update_blocks(m_s.at[g], l_s.at[g], acc_s.at[g], blocks)

    @pl.when(p == n_steps - 1)
    def _():
        gates = jax.nn.sigmoid(misc_ref[...])
        for g in range(NSA_KV_HEADS):
            kcol = slice(g * 64, (g + 1) * 64)
            vcol = slice(128 + g * 64, 128 + (g + 1) * 64)
            q4 = _nsa_q4(hq_ref, g)
            st = (m_s.at[g], l_s.at[g], acc_s.at[g])
            k = _pad_rows(slc_ref[:, kcol].astype(BF16), LANES)
            v = _pad_rows(slc_ref[:, vcol].astype(BF16), LANES)
            selx = _dot(sel_s[g].astype(BF16), _block_expand(l_past, LANES))
            bias = jnp.where((selx > 0.5) & new_ok, 0.0, NEG_INF)
            _st_update(*st, _dot_nt(q4, k) + _tile_rows(bias, NSA_GROUP), lambda pb: _dot(pb, v))
            o_slc = _st_finish(st[1], st[2])
            _st_reset(*st)
            kt = winp_ref[0, g].astype(BF16)
            vt = winp_ref[1, g].astype(BF16)
            wpos = l_past - w_past + _iota((1, w_past), 1)
            bias = jnp.where(wpos > qpos - WINDOW, 0.0, NEG_INF)
            _st_update(*st, _dot(q4, kt) + _tile_rows(bias, NSA_GROUP), lambda pb: _dot_nt(pb, vt))
            k = _pad_rows(win_ref[:, kcol].astype(BF16), LANES)
            v = _pad_rows(win_ref[:, vcol].astype(BF16), LANES)
            bias = jnp.where(new_ok & (new_pos > qpos - WINDOW), 0.0, NEG_INF)
            _st_update(*st, _dot_nt(q4, k) + _tile_rows(bias, NSA_GROUP), lambda pb: _dot(pb, v))
            o_win = _st_finish(st[1], st[2])
            _nsa_write(o_ref, gates, g, dt, ocmp_s[g], o_slc, o_win)


def _nsa_sample(h, ckv, slc_v, win_v, pt_flat, *, layer, row0, n_seq, dt, n_pages, pp, w_past, topn):
    l_past = n_pages * PAGE_SIZE
    l_tot = l_past + dt
    nc = l_tot // CMP_BLOCK
    ns = -(-l_tot // SLC_BLOCK)
    assert ns <= SEL_W and nc <= 2 * SEL_W and dt <= LANES
    n_steps = n_pages // pp
    rb0 = row0 // dt
    rows4 = NSA_GROUP * dt

    def hblk(w, col):
        return pl.BlockSpec((dt, w), lambda s, p, pt: (rb0 + s, col))

    in_specs = [hblk(512, E_NQ // 512), hblk(LANES, E_MISC // LANES), hblk(256, E_SLC // 256), hblk(256, E_WIN // 256),
                pl.BlockSpec((None, 4, SEL_W, LANES), lambda s, p, pt: (s, 0, 0, 0)),
                pl.BlockSpec((None, None, 2, NSA_KV_HEADS, HEAD_DIM, w_past), lambda s, p, pt: (s, layer, 0, 0, 0, 0))]
    in_specs += _page_specs((None, None, 2, NSA_KV_HEADS, HEAD_DIM, PAGE_SIZE), n_pages, pp, layer)
    return pl.pallas_call(
        functools.partial(_nsa_sample_kernel, pp=pp, n_steps=n_steps, dt=dt, l_past=l_past, w_past=w_past,
                          nc=nc, ns=ns, topn=min(topn, ns)),
        grid_spec=pltpu.PrefetchScalarGridSpec(
            num_scalar_prefetch=1, grid=(n_seq, n_steps), in_specs=in_specs,
            out_specs=pl.BlockSpec((dt, NSA_HEADS * HEAD_DIM), lambda s, p, pt: (s, 0)),
            scratch_shapes=[pltpu.VMEM((NSA_KV_HEADS, dt, SEL_W), F32), pltpu.VMEM((NSA_KV_HEADS, rows4, HEAD_DIM), F32),
                            pltpu.VMEM((NSA_KV_HEADS, rows4, 1), F32), pltpu.VMEM((NSA_KV_HEADS, rows4, 1), F32),
                            pltpu.VMEM((NSA_KV_HEADS, rows4, HEAD_DIM), F32)]),
        out_shape=jax.ShapeDtypeStruct((n_seq * dt, NSA_HEADS * HEAD_DIM), F32),
        compiler_params=pltpu.CompilerParams(dimension_semantics=("arbitrary", "arbitrary"),
                                             vmem_limit_bytes=VMEM_LIMIT),
        name="nsa_sample",
    )(pt_flat, h, h, h, h, ckv, win_v, *([slc_v] * pp))


def _fox_sample_kernel(pt_ref, *refs, pp, n_steps, dt, l_past):
    fq_ref, fk_ref, fv_ref, misc_ref, fb_ref = refs[:5]
    kv_slabs = refs[5:5 + pp]
    lf_slabs = refs[5 + pp:5 + 2 * pp]
    o_ref, lf_ref, qbd_s, carry_s, m_s, l_s, acc_s = refs[5 + 2 * pp:]
    p = pl.program_id(1)
    rows = FOX_HEADS * dt
    width = FOX_HEADS * HEAD_DIM

    @pl.when(p == 0)
    def _():
        qbd_s[...] = jnp.zeros_like(qbd_s)
        for hh in range(FOX_HEADS):
            cs = slice(hh * 64, (hh + 1) * 64)
            qbd_s[hh * dt:(hh + 1) * dt, cs] = (fq_ref[:, cs] * 0.125).astype(BF16)
        carry_s[...] = jnp.zeros_like(carry_s)
        _st_reset(m_s, l_s, acc_s)

    def head_rows(x):
        return jnp.concatenate([jnp.broadcast_to(x[hh:hh + 1, :], (dt, LANES)) for hh in range(FOX_HEADS)], axis=0)

    upper = jnp.where(_iota((LANES, LANES), 0) <= _iota((LANES, LANES), 1), 1.0, 0.0).astype(BF16)
    offset = carry_s[...]
    qbd = qbd_s[...]
    blocks = []
    for k in range(pp):
        hi, mid, lo = _split3(lf_slabs[k][...])
        local = _dot(hi.astype(BF16), upper) + _dot(mid.astype(BF16), upper) + _dot(lo.astype(BF16), upper)
        cum = local + offset
        offset = jnp.broadcast_to(cum[:, LANES - 1:LANES], (FOX_HEADS, LANES))
        kt = kv_slabs[k][0].reshape(width, PAGE_SIZE).astype(BF16)
        vt = kv_slabs[k][1].reshape(width, PAGE_SIZE).astype(BF16)
        blocks.append((_dot(qbd, kt) - head_rows(cum), lambda pb, vt=vt: _dot_nt(pb, vt)))
    carry_s[...] = offset
    _st_update_blocks(m_s, l_s, acc_s, blocks)

    @pl.when(p == n_steps - 1)
    def _():
        lf = _forget_lanes(_log_sigmoid(misc_ref[...] + fb_ref[...]))
        lf_ref[...] = lf
        parts = _split3(_pad_rows(_cum_small(lf), LANES))
        lane = _iota((dt, LANES), 1)
        pick = jnp.concatenate([jnp.where(lane == MISC_FF + hh, 1.0, 0.0) for hh in range(FOX_HEADS)], axis=0)
        pick = pick.astype(BF16)
        cum_new = sum(_dot_nt(pick, part.astype(BF16)) for part in parts)
        kn = _pad_rows(fk_ref[...].astype(BF16), LANES)
        vn = _pad_rows(fv_ref[...].astype(BF16), LANES)
        col = _iota((1, LANES), 1)
        trow = _tile_rows(_iota((dt, 1), 0), FOX_HEADS)
        ok = (col <= trow) & (col < dt)
        s = _dot_nt(qbd_s[...], kn) - (head_rows(carry_s[...]) + cum_new)
        _st_update(m_s, l_s, acc_s, jnp.where(ok, s, NEG_INF), lambda pb: _dot(pb, vn))
        o = _st_finish(l_s, acc_s)
        for hh in range(FOX_HEADS):
            o_ref[:, hh * 64:(hh + 1) * 64] = o[hh * dt:(hh + 1) * dt, hh * 64:(hh + 1) * 64]


def _fox_sample(h, fb, fox_v, flf_v, pt_flat, *, layer, row0, n_seq, dt, n_pages, pp):
    l_past = n_pages * PAGE_SIZE
    n_steps = n_pages // pp
    rb0 = row0 // dt
    rows = FOX_HEADS * dt
    width = FOX_HEADS * HEAD_DIM
    assert dt <= LANES

    def hblk(w, col):
        return pl.BlockSpec((dt, w), lambda s, p, pt: (rb0 + s, col))

    in_specs = [hblk(512, E_FQ // 512), hblk(512, E_FK // 512), hblk(512, E_FV // 512), hblk(LANES, E_MISC // LANES),
                pl.BlockSpec((1, LANES), lambda s, p, pt: (0, 0))]
    in_specs += _page_specs((None, None, 2, FOX_HEADS, HEAD_DIM, PAGE_SIZE), n_pages, pp, layer)
    in_specs += _page_specs((None, None, FOX_HEADS, PAGE_SIZE), n_pages, pp, layer)
    return pl.pallas_call(
        functools.partial(_fox_sample_kernel, pp=pp, n_steps=n_steps, dt=dt, l_past=l_past),
        grid_spec=pltpu.PrefetchScalarGridSpec(
            num_scalar_prefetch=1, grid=(n_seq, n_steps), in_specs=in_specs,
            out_specs=[pl.BlockSpec((dt, width), lambda s, p, pt: (s, 0)),
                       pl.BlockSpec((dt, LANES), lambda s, p, pt: (s, 0))],
            scratch_shapes=[pltpu.VMEM((rows, width), BF16), pltpu.VMEM((FOX_HEADS, LANES), F32),
                            pltpu.VMEM((rows, 1), F32), pltpu.VMEM((rows, 1), F32), pltpu.VMEM((rows, width), F32)]),
        out_shape=[jax.ShapeDtypeStruct((n_seq * dt, width), F32), jax.ShapeDtypeStruct((n_seq * dt, LANES), F32)],
        compiler_params=pltpu.CompilerParams(dimension_semantics=("arbitrary", "arbitrary"),
                                             vmem_limit_bytes=VMEM_LIMIT),
        name="fox_sample",
    )(pt_flat, h, h, h, h, fb, *([fox_v] * pp), *([flf_v] * pp))


def _dsa_index_kernel(pt_ref, *refs, pp, n_steps, dt, n_pages, n_keep):
    qi_ref, kw_ref = refs[:2]
    slabs = refs[2:2 + pp]
    o_ref, sc_s, key_s = refs[2 + pp:]
    p = pl.program_id(1)
    wi = kw_ref[:, IDX_DIM:IDX_DIM + IDX_HEADS] * (IDX_HEADS ** -0.5)
    qst = jnp.concatenate([qi_ref[:, hh * IDX_DIM:(hh + 1) * IDX_DIM] for hh in range(IDX_HEADS)], axis=0)
    qst = (qst * (IDX_DIM ** -0.5)).astype(BF16)

    def scores(lg):
        sc = jnp.zeros((dt, lg.shape[1]), F32)
        for hh in range(IDX_HEADS):
            sc = sc + wi[:, hh:hh + 1] * jnp.maximum(lg[hh * dt:(hh + 1) * dt], 0.0)
        return sc + 0.0

    for k in range(pp):
        sc_s[p * pp + k] = scores(_dot(qst, slabs[k][...].astype(BF16)))

    @pl.when(p == n_steps - 1)
    def _():
        kin = _pad_rows(kw_ref[:, 0:IDX_DIM].astype(BF16), LANES)
        col = _iota((1, LANES), 1)
        ok = (col <= _iota((dt, 1), 0)) & (col < dt)
        sc_s[n_pages] = jnp.where(ok, scores(_dot_nt(qst, kin)), NEG_INF)
        key_s[...] = _sortable_key(sc_s[...])
        shape = (n_pages + 1, dt, LANES)
        idx = _iota(shape, 0) * LANES + _iota(shape, 2)
        _topk_bias_store(o_ref, key_s, n_keep, idx,
                         lambda mask: jnp.sum(jnp.sum(jnp.where(mask, 1.0, 0.0), axis=0), axis=-1, keepdims=True)[None],
                         (1, dt, 1))


def _dsa_index(h, dik_v, pt_flat, *, layer, row0, n_seq, dt, n_pages, pp, n_keep):
    n_steps = n_pages // pp
    rb0 = row0 // dt
    shape = (n_pages + 1, dt, LANES)
    in_specs = [pl.BlockSpec((dt, 512), lambda s, p, pt: (rb0 + s, O_QI // 512)),
                pl.BlockSpec((dt, LANES), lambda s, p, pt: (rb0 + s, O_KI // LANES))]
    in_specs += _page_specs((None, None, IDX_DIM, PAGE_SIZE), n_pages, pp, layer)
    return pl.pallas_call(
        functools.partial(_dsa_index_kernel, pp=pp, n_steps=n_steps, dt=dt, n_pages=n_pages, n_keep=n_keep),
        grid_spec=pltpu.PrefetchScalarGridSpec(
            num_scalar_prefetch=1, grid=(n_seq, n_steps), in_specs=in_specs,
            out_specs=pl.BlockSpec((None,) + shape, lambda s, p, pt: (s, 0, 0, 0)),
            scratch_shapes=[pltpu.VMEM(shape, F32), pltpu.VMEM(shape, jnp.int32)]),
        out_shape=jax.ShapeDtypeStruct((n_seq,) + shape, F32),
        compiler_params=pltpu.CompilerParams(dimension_semantics=("arbitrary", "arbitrary")),
        name="dsa_index",
    )(pt_flat, h, h, *([dik_v] * pp))


def _dsa_sample_kernel(pt_ref, *refs, pp, n_steps, dt, n_pages):
    q_ref, kv_ref, bias_ref = refs[:3]
    slabs = refs[3:3 + pp]
    o_ref, qbd_s, m_s, l_s, acc_s = refs[3 + pp:]
    p = pl.program_id(1)
    width = DSA_KV_HEADS * HEAD_DIM

    @pl.when(p == 0)
    def _():
        qbd_s[...] = jnp.zeros_like(qbd_s)
        for hh in range(DSA_HEADS):
            kh = hh // DSA_GROUP
            qbd_s[hh * dt:(hh + 1) * dt, kh * 64:(kh + 1) * 64] = (q_ref[:, hh * 64:(hh + 1) * 64] * 0.125).astype(BF16)
        _st_reset(m_s, l_s, acc_s)

    qbd = qbd_s[...]
    blocks = []
    for k in range(pp):
        kt = slabs[k][0].reshape(width, PAGE_SIZE).astype(BF16)
        vt = slabs[k][1].reshape(width, PAGE_SIZE).astype(BF16)
        s = _dot(qbd, kt) + _tile_rows(bias_ref[p * pp + k], DSA_HEADS)
        blocks.append((s, lambda pb, vt=vt: _dot_nt(pb, vt)))
    _st_update_blocks(m_s, l_s, acc_s, blocks)

    @pl.when(p == n_steps - 1)
    def _():
        kn = _pad_rows(kv_ref[:, 0:width].astype(BF16), LANES)
        vn = _pad_rows(kv_ref[:, width:2 * width].astype(BF16), LANES)
        s = _dot_nt(qbd_s[...], kn) + _tile_rows(bias_ref[n_pages], DSA_HEADS)
        _st_update(m_s, l_s, acc_s, s, lambda pb: _dot(pb, vn))
        o = _st_finish(l_s, acc_s)
        for hh in range(DSA_HEADS):
            kh = hh // DSA_GROUP
            o_ref[:, hh * 64:(hh + 1) * 64] = o[hh * dt:(hh + 1) * dt, kh * 64:(kh + 1) * 64]


def _dsa_sample(h, bias, dkv_v, pt_flat, *, layer, row0, n_seq, dt, n_pages, pp):
    n_steps = n_pages // pp
    rb0 = row0 // dt
    rows = DSA_HEADS * dt
    width = DSA_KV_HEADS * HEAD_DIM
    in_specs = [pl.BlockSpec((dt, 1024), lambda s, p, pt: (rb0 + s, O_Q // 1024)),
                pl.BlockSpec((dt, 512), lambda s, p, pt: (rb0 + s, O_KV // 512)),
                pl.BlockSpec((None, n_pages + 1, dt, LANES), lambda s, p, pt: (s, 0, 0, 0))]
    in_specs += _page_specs((None, None, 2, DSA_KV_HEADS, HEAD_DIM, PAGE_SIZE), n_pages, pp, layer)
    return pl.pallas_call(
        functools.partial(_dsa_sample_kernel, pp=pp, n_steps=n_steps, dt=dt, n_pages=n_pages),
        grid_spec=pltpu.PrefetchScalarGridSpec(
            num_scalar_prefetch=1, grid=(n_seq, n_steps), in_specs=in_specs,
            out_specs=pl.BlockSpec((dt, DSA_HEADS * HEAD_DIM), lambda s, p, pt: (s, 0)),
            scratch_shapes=[pltpu.VMEM((rows, width), BF16), pltpu.VMEM((rows, 1), F32), pltpu.VMEM((rows, 1), F32),
                            pltpu.VMEM((rows, width), F32)]),
        out_shape=jax.ShapeDtypeStruct((n_seq * dt, DSA_HEADS * HEAD_DIM), F32),
        compiler_params=pltpu.CompilerParams(dimension_semantics=("arbitrary", "arbitrary"),
                                             vmem_limit_bytes=VMEM_LIMIT),
        name="dsa_sample",
    )(pt_flat, h, h, bias, *([dkv_v] * pp))


def _rope_table(pos):
    half = ROT_DIM // 2
    inv = ROPE_THETA ** (-2.0 * jnp.arange(half, dtype=F32) / ROT_DIM)
    ang = pos.astype(F32)[:, None] * inv[None, :]
    cos, sin = jnp.cos(ang), jnp.sin(ang)
    n = pos.shape[0]
    one = jnp.ones((n, HEAD_DIM - ROT_DIM), F32)
    zero = jnp.zeros((n, HEAD_DIM - ROT_DIM), F32)
    z8 = jnp.zeros((n, half), F32)
    c = jnp.concatenate([cos, cos, one], axis=1)
    s1 = jnp.concatenate([z8, sin, zero], axis=1)
    s2 = jnp.concatenate([-sin, z8, zero], axis=1)
    return jnp.concatenate([c, c, s1, s1, s2, s2], axis=1)


def _largest_tile(limit, *sizes):
    t = limit
    while any(s % t for s in sizes):
        t //= 2
    return t


def _rows_last(pool):
    nd = pool.ndim
    return jnp.transpose(pool, (0, 1) + tuple(range(3, nd)) + (2,))


def kernel(x_prompt, x_sample, cache_nsa_cmp_kv, cache_nsa_slc_kv, state_nsa_win_kv, cache_fox_kv, cache_fox_logf, cache_dsa_kv, cache_dsa_idx_k, page_table, ln_g, ln_b, w_in_even, w_out_even, fox_f_bias, nsa_cmp_pos, nsa_cmp_phi, w_in_odd, w_out_odd, ffn_gu, ffn_down, moe_router, moe_gu, moe_down):
    nb, t, d = x_prompt.shape
    db, dt, _ = x_sample.shape
    n_pages = page_table.shape[1]
    lp = n_pages * PAGE_SIZE
    w_past = state_nsa_win_kv.shape[2]
    n_p, n_s = nb * t, db * dt
    depth = ln_g.shape[0]
    alpha = (2 * depth) ** 0.25
    tm = _largest_tile(512, n_p, n_s)
    tm_ff = _largest_tile(512, n_p, n_s)
    tq = _largest_tile(256, t)
    pp_small = n_pages
    pp_big = n_pages
    pt_flat = page_table.reshape(-1).astype(jnp.int32)

    cmp_v, slc_v, win_v = _rows_last(cache_nsa_cmp_kv), _rows_last(cache_nsa_slc_kv), _rows_last(state_nsa_win_kv)
    fox_v, flf_v = _rows_last(cache_fox_kv), _rows_last(cache_fox_logf)
    dkv_v, dik_v = _rows_last(cache_dsa_kv), _rows_last(cache_dsa_idx_k)

    x = jnp.concatenate([x_prompt.reshape(n_p, d), x_sample.reshape(n_s, d)], axis=0)
    pos = jnp.concatenate([jnp.tile(jnp.arange(t), nb), jnp.tile(lp + jnp.arange(dt), db)])
    rt = _rope_table(pos)
    nc_p, nc_s = t // CMP_BLOCK, (lp + dt) // CMP_BLOCK
    assert lp % CMP_BLOCK == 0 and dt < CMP_BLOCK
    crt_p = _rope_table((jnp.arange(nc_p) + 1) * CMP_BLOCK - 1)
    crt_s = _rope_table((jnp.arange(nc_s) + 1) * CMP_BLOCK - 1)

    ev_p, ev_s, od_p, od_s = [], [], [], []
    for layer in range(depth):
        j = layer // 2
        g0, b0 = ln_g[layer, 0][None, :], ln_b[layer, 0][None, :]
        g1, b1 = ln_g[layer, 1][None, :], ln_b[layer, 1][None, :]
        if layer % 2 == 0:
            w = w_in_even[j]
            w = jnp.concatenate([w[:, 0:512], w[:, 1304:2840], w[:, 512:1280], w[:, 1280:1304], w[:, 2840:2848],
                                 jnp.zeros((d, E_W - 2848), w.dtype)], axis=1).astype(BF16)
            h = _proj(x, w, rt, width=E_W, rope_full=EVEN_ROPE, rope_lo=(), tm=tm)

            pe = jnp.concatenate([nsa_cmp_pos[j], nsa_cmp_pos[j]], axis=-1)
            ph = nsa_cmp_phi[j]
            zz = jnp.zeros_like(ph)
            phi = jnp.concatenate([jnp.concatenate([ph, zz], axis=-1),
                                   jnp.concatenate([zz, ph], axis=-1)], axis=-2).astype(BF16)
            fb = jnp.zeros((1, LANES), F32).at[0, MISC_FF:MISC_FF + FOX_HEADS].set(fox_f_bias[j])

            qa_p, ka_p, lf_p = _fox_prep(h, fb, n_seq=nb, t=t, rows=_largest_tile(256, t))
            o_nsa_p = _nsa_attn(h, pe, phi, crt_p, n_seq=nb, t=t, tq=tq, topn=SLC_TOPN)
            o_fox_p = _fox_attn(qa_p, ka_p, h, n_seq=nb, t=t, tq=tq, hps=FOX_HEADS)

            ckv = _cmp_sample(cmp_v, pt_flat, pe, phi, crt_s, layer=j, n_seq=db, n_pages=n_pages, pp=pp_small, nc=nc_s)
            o_nsa_s = _nsa_sample(h, ckv, slc_v, win_v, pt_flat, layer=j, row0=n_p, n_seq=db, dt=dt, n_pages=n_pages,
                                  pp=pp_small, w_past=w_past, topn=SLC_TOPN)
            o_fox_s, lf_s = _fox_sample(h, fb, fox_v, flf_v, pt_flat, layer=j, row0=n_p, n_seq=db, dt=dt,
                                        n_pages=n_pages, pp=pp_big)

            wo = w_out_even[j].astype(BF16)
            x = _out_ln([o_nsa_p, o_fox_p], [o_nsa_s, o_fox_s], [wo[0:512], wo[512:1024]], x, g0, b0,
                        alpha=alpha, tm=tm)
            x = _ffn_ln(x, ffn_gu[j].astype(BF16), ffn_down[j].astype(BF16), g1, b1, alpha=alpha, tm=tm_ff,
                        fc=ffn_down.shape[1] // 2)

            hp, hs = h[:n_p].reshape(nb, t, E_W), h[n_p:].reshape(db, dt, E_W)
            kv2 = (2, NSA_KV_HEADS, HEAD_DIM)
            win_new = hs[:, :, E_WIN:E_WIN + 256].reshape((db, dt) + kv2)
            keep_p, keep_s = min(WINDOW, t), min(WINDOW, w_past + dt)
            win_s = jnp.concatenate([state_nsa_win_kv[:, j, w_past + dt - keep_s:], win_new], axis=1)
            lf_p3 = lf_p[:, MISC_FF:MISC_FF + FOX_HEADS].reshape(nb, t, FOX_HEADS)
            lf_s3 = lf_s[:, MISC_FF:MISC_FF + FOX_HEADS].reshape(db, dt, FOX_HEADS)
            ev_p.append((hp[:, :, E_CMP:E_CMP + 256], hp[:, :, E_SLC:E_SLC + 256],
                         hp[:, t - keep_p:, E_WIN:E_WIN + 256], hp[:, :, E_FK:E_FK + 1024], lf_p3))
            ev_s.append((hs[:, :, E_CMP:E_CMP + 256], hs[:, :, E_SLC:E_SLC + 256],
                         win_s, hs[:, :, E_FK:E_FK + 1024], lf_s3))
        else:
            w = jnp.concatenate([w_in_odd[j], jnp.zeros((d, O_W - w_in_odd.shape[2]), w_in_odd.dtype)],
                                axis=1).astype(BF16)
            h = _proj(x, w, rt, width=O_W, rope_full=ODD_ROPE, rope_lo=ODD_ROPE_LO, tm=tm)
            o_p = _dsa_attn(h, n_seq=nb, t=t, tq=tq, n_keep=min(IDX_TOPK, t // 4))
            bias_s = _dsa_index(h, dik_v, pt_flat, layer=j, row0=n_p, n_seq=db, dt=dt, n_pages=n_pages,
                                pp=n_pages, n_keep=min(IDX_TOPK, (lp + dt) // 4))
            o_s = _dsa_sample(h, bias_s, dkv_v, pt_flat, layer=j, row0=n_p, n_seq=db, dt=dt, n_pages=n_pages,
                              pp=pp_big)
            x = _out_ln([o_p], [o_s], [w_out_odd[j].astype(BF16)], x, g0, b0, alpha=alpha, tm=tm)
            wr = jnp.concatenate([moe_router[j], jnp.zeros((d, LANES - N_EXPERTS), F32)], axis=1)
            wr_hi = wr.astype(BF16)
            wr_lo = (wr - wr_hi.astype(F32)).astype(BF16)
            x = _moe_ln(x, wr_hi, wr_lo, jnp.transpose(moe_gu[j], (0, 2, 1)).astype(BF16),
                        jnp.transpose(moe_down[j], (0, 2, 1)).astype(BF16), g1, b1,
                        alpha=alpha, tmm=MOE_TILE if n_p + n_s >= 8 * MOE_TILE else MOE_BLOCK,
                        fc=moe_down.shape[2] // 4, blk=MOE_BLOCK)
            hp, hs = h[:n_p].reshape(nb, t, O_W), h[n_p:].reshape(db, dt, O_W)
            od_p.append((hp[:, :, O_KV:O_KV + 512], hp[:, :, O_KI:O_KI + IDX_DIM]))
            od_s.append((hs[:, :, O_KV:O_KV + 512], hs[:, :, O_KI:O_KI + IDX_DIM]))

    def stk(lst, idx, tail):
        a = jnp.stack([s[idx] for s in lst], axis=1)
        return a.reshape(a.shape[:3] + tail)

    kv2 = (2, NSA_KV_HEADS, HEAD_DIM)
    fkv = (2, FOX_HEADS, HEAD_DIM)
    dkv = (2, DSA_KV_HEADS, HEAD_DIM)
    return (x[:n_p].reshape(nb, t, d), x[n_p:].reshape(db, dt, d),
            stk(ev_p, 0, kv2), stk(ev_s, 0, kv2), stk(ev_p, 1, kv2), stk(ev_s, 1, kv2),
            stk(ev_p, 2, kv2), stk(ev_s, 2, kv2), stk(ev_p, 3, fkv), stk(ev_s, 3, fkv),
            stk(ev_p, 4, (FOX_HEADS,)), stk(ev_s, 4, (FOX_HEADS,)),
            stk(od_p, 0, dkv), stk(od_s, 0, dkv), stk(od_p, 1, (IDX_DIM,)), stk(od_s, 1, (IDX_DIM,)))
```

```python
import functools
import math

import jax
import jax.numpy as jnp
from jax import lax
from jax.experimental import pallas as pl
from jax.experimental.pallas import tpu as pltpu

F32 = jnp.float32
BF16 = jnp.bfloat16
NEG_INF = float("-inf")

HEAD_DIM = 64
ROT_DIM = HEAD_DIM // 4
ROPE_THETA = 500000.0
NSA_HEADS = 8
NSA_KV_HEADS = 2
NSA_GROUP = NSA_HEADS // NSA_KV_HEADS
CMP_BLOCK = 32
SLC_BLOCK = 64
SLC_TOPN = 16
WINDOW = 512
FOX_HEADS = 8
DSA_HEADS = 16
DSA_KV_HEADS = 4
DSA_GROUP = DSA_HEADS // DSA_KV_HEADS
IDX_HEADS = 8
IDX_DIM = 64
IDX_TOPK = 256
N_EXPERTS = 8
LN_EPS = 1e-5
PAGE_SIZE = 128
LANES = 128

E_NQ, E_FQ, E_FK, E_FV, E_CMP, E_SLC, E_WIN, E_MISC, E_W = 0, 512, 1024, 1536, 2048, 2304, 2560, 2816, 2944
MISC_GATES = 0
MISC_FF = 24
EVEN_ROPE = (0, 1, 2, 3, E_SLC // LANES, E_WIN // LANES)
O_Q, O_KV, O_QI, O_KI, O_W = 0, 1024, 1536, 2048, 2176
ODD_ROPE = tuple(range(0, 10)) + tuple(range(12, 16))
ODD_ROPE_LO = (O_KI // LANES,)

SEL_W = 64
KEY_CHUNK = 1024
MOE_BLOCK = 256
MOE_TILE = 896
VMEM_LIMIT = 48 * 1024 * 1024


def _dot(a, b):
    return jnp.dot(a, b, preferred_element_type=F32)


def _dot_nt(a, b):
    return lax.dot_general(a, b, (((1,), (1,)), ((), ())), preferred_element_type=F32)


def _iota(shape, dim):
    return lax.broadcasted_iota(jnp.int32, shape, dim)


def _rope128(v, c, s1, s2):
    return v * c + pltpu.roll(v, 8, 1) * s1 + pltpu.roll(v, LANES - 8, 1) * s2


def _split3(x):
    hi = x.astype(BF16).astype(F32)
    r = x - hi
    mid = r.astype(BF16).astype(F32)
    lo = (r - mid).astype(BF16).astype(F32)
    return hi, mid, lo


def _st_reset(m_ref, l_ref, acc_ref):
    m_ref[...] = jnp.full(m_ref.shape, NEG_INF, F32)
    l_ref[...] = jnp.zeros(l_ref.shape, F32)
    acc_ref[...] = jnp.zeros(acc_ref.shape, F32)


def _st_update(m_ref, l_ref, acc_ref, s, pv):
    m_prev = m_ref[...]
    m_new = jnp.maximum(m_prev, jnp.max(s, axis=-1, keepdims=True))
    m_safe = jnp.where(m_new == NEG_INF, 0.0, m_new)
    alpha = jnp.exp(m_prev - m_safe)
    p = jnp.exp(s - m_safe)
    l_ref[...] = alpha * l_ref[...] + jnp.sum(p, axis=-1, keepdims=True)
    acc_ref[...] = alpha * acc_ref[...] + pv(p.astype(BF16))
    m_ref[...] = m_new


def _st_update_blocks(m_ref, l_ref, acc_ref, blocks):
    m_prev = m_ref[...]
    widest = blocks[0][0]
    for s, _ in blocks[1:]:
        widest = jnp.maximum(widest, s)
    m_new = jnp.maximum(m_prev, jnp.max(widest, axis=-1, keepdims=True))
    m_safe = jnp.where(m_new == NEG_INF, 0.0, m_new)
    alpha = jnp.exp(m_prev - m_safe)
    acc = alpha * acc_ref[...]
    total = None
    for s, pv in blocks:
        p = jnp.exp(s - m_safe)
        total = p if total is None else total + p
        acc = acc + pv(p.astype(BF16))
    l_ref[...] = alpha * l_ref[...] + jnp.sum(total, axis=-1, keepdims=True)
    acc_ref[...] = acc
    m_ref[...] = m_new


def _st_finish(l_ref, acc_ref):
    return acc_ref[...] / jnp.maximum(l_ref[...], 1e-30)


def _pad_rows(a, rows):
    if a.shape[0] == rows:
        return a
    return jnp.concatenate([a, jnp.zeros((rows - a.shape[0], a.shape[1]), a.dtype)], axis=0)


def _tile_rows(a, times):
    return jnp.concatenate([a] * times, axis=0)


def _layer_norm(z, g, b):
    mu = jnp.mean(z, axis=-1, keepdims=True)
    d = z - mu
    var = jnp.mean(d * d, axis=-1, keepdims=True)
    return d * lax.rsqrt(var + LN_EPS) * g + b


def _key_chunks(total, ch):
    return [(r0, min(ch, total - r0)) for r0 in range(0, total, ch)]


def _proj_kernel(x_ref, w_ref, rt_ref, o_ref, *, width, rope_full, rope_lo, cw):
    xb = x_ref[...].astype(BF16)
    c, s1, s2 = rt_ref[:, 0:128], rt_ref[:, 128:256], rt_ref[:, 256:384]
    lo = _iota((1, LANES), 1) < HEAD_DIM
    for c0 in range(0, width, cw):
        c1 = min(c0 + cw, width)
        acc = _dot(xb, w_ref[:, c0:c1])
        for blk in range(c0 // LANES, c1 // LANES):
            v = acc[:, blk * LANES - c0:(blk + 1) * LANES - c0]
            if blk in rope_full:
                v = _rope128(v, c, s1, s2)
            elif blk in rope_lo:
                v = _rope128(v, jnp.where(lo, c, 1.0), jnp.where(lo, s1, 0.0), jnp.where(lo, s2, 0.0))
            o_ref[:, blk * LANES:(blk + 1) * LANES] = v


def _proj(x, w, rt, *, width, rope_full, rope_lo, tm):
    n, d = x.shape
    return pl.pallas_call(
        functools.partial(_proj_kernel, width=width, rope_full=rope_full, rope_lo=rope_lo, cw=512),
        grid=(n // tm,),
        in_specs=[pl.BlockSpec((tm, d), lambda i: (i, 0)),
                  pl.BlockSpec((d, width), lambda i: (0, 0)),
                  pl.BlockSpec((tm, 3 * LANES), lambda i: (i, 0))],
        out_specs=pl.BlockSpec((tm, width), lambda i: (i, 0)),
        out_shape=jax.ShapeDtypeStruct((n, width), F32),
        compiler_params=pltpu.CompilerParams(dimension_semantics=("arbitrary",), vmem_limit_bytes=VMEM_LIMIT),
        name="proj",
    )(x, w, rt)


def _out_ln_kernel(*refs, n_parts, n_prompt_tiles, alpha):
    op = refs[0:n_parts]
    os_ = refs[n_parts:2 * n_parts]
    ws = refs[2 * n_parts:3 * n_parts]
    x_ref, g_ref, b_ref, o_ref = refs[3 * n_parts:]
    i = pl.program_id(0)

    def compute(parts):
        y = None
        for o, w in zip(parts, ws):
            t = _dot(o[...].astype(BF16), w[...])
            y = t if y is None else y + t
        o_ref[...] = _layer_norm(alpha * x_ref[...] + y, g_ref[...], b_ref[...])

    @pl.when(i < n_prompt_tiles)
    def _():
        compute(op)

    @pl.when(i >= n_prompt_tiles)
    def _():
        compute(os_)


def _out_ln(o_prompt, o_sample, w_parts, x, g, b, *, alpha, tm):
    n, d = x.shape
    npt = o_prompt[0].shape[0] // tm
    nst = o_sample[0].shape[0] // tm
    k = len(w_parts)
    in_specs = []
    for o in o_prompt:
        in_specs.append(pl.BlockSpec((tm, o.shape[1]), lambda i: (jnp.minimum(i, npt - 1), 0)))
    for o in o_sample:
        in_specs.append(pl.BlockSpec((tm, o.shape[1]), lambda i: (jnp.maximum(i - npt, 0), 0)))
    for w in w_parts:
        in_specs.append(pl.BlockSpec(w.shape, lambda i: (0, 0)))
    in_specs += [pl.BlockSpec((tm, d), lambda i: (i, 0)),
                 pl.BlockSpec((1, d), lambda i: (0, 0)),
                 pl.BlockSpec((1, d), lambda i: (0, 0))]
    assert npt + nst == n // tm
    return pl.pallas_call(
        functools.partial(_out_ln_kernel, n_parts=k, n_prompt_tiles=npt, alpha=alpha),
        grid=(n // tm,),
        in_specs=in_specs,
        out_specs=pl.BlockSpec((tm, d), lambda i: (i, 0)),
        out_shape=jax.ShapeDtypeStruct((n, d), F32),
        compiler_params=pltpu.CompilerParams(dimension_semantics=("arbitrary",), vmem_limit_bytes=VMEM_LIMIT),
        name="out_ln",
    )(*o_prompt, *o_sample, *w_parts, x, g, b)


def _ffn_ln_kernel(x_ref, wa_ref, wg_ref, wd_ref, g_ref, b_ref, o_ref, acc_ref, *, alpha, n_chunks):
    c = pl.program_id(1)

    @pl.when(c == 0)
    def _():
        acc_ref[...] = jnp.zeros_like(acc_ref)

    xb = x_ref[...].astype(BF16)
    a = _dot(xb, wa_ref[...])
    gg = _dot(xb, wg_ref[...])
    hmid = (jax.nn.silu(a) * gg).astype(BF16)
    acc_ref[...] += _dot(hmid, wd_ref[...])

    @pl.when(c == n_chunks - 1)
    def _():
        o_ref[...] = _layer_norm(alpha * x_ref[...] + acc_ref[...], g_ref[...], b_ref[...])


def _ffn_ln(x, w_gu, w_down, g, b, *, alpha, tm, fc):
    n, d = x.shape
    dff = w_down.shape[0]
    nch = dff // fc
    return pl.pallas_call(
        functools.partial(_ffn_ln_kernel, alpha=alpha, n_chunks=nch),
        grid=(n // tm, nch),
        in_specs=[pl.BlockSpec((tm, d), lambda i, c: (i, 0)),
                  pl.BlockSpec((d, fc), lambda i, c: (0, c)),
                  pl.BlockSpec((d, fc), lambda i, c: (0, c + nch)),
                  pl.BlockSpec((fc, d), lambda i, c: (c, 0)),
                  pl.BlockSpec((1, d), lambda i, c: (0, 0)),
                  pl.BlockSpec((1, d), lambda i, c: (0, 0))],
        out_specs=pl.BlockSpec((tm, d), lambda i, c: (i, 0)),
        out_shape=jax.ShapeDtypeStruct((n, d), F32),
        scratch_shapes=[pltpu.VMEM((tm, d), F32)],
        compiler_params=pltpu.CompilerParams(dimension_semantics=("arbitrary", "arbitrary"),
                                             vmem_limit_bytes=VMEM_LIMIT),
        name="ffn_ln",
    )(x, w_gu, w_gu, w_down, g, b)


def _moe_ln_kernel(x_ref, wrh_ref, wrl_ref, wat_ref, wgt_ref, wdt_ref, g_ref, b_ref, o_ref,
                   xt_s, acct_s, route_s, routet_s, xct_s, yct_s, *, alpha, n_chunks, n_valid, blk):
    i = pl.program_id(0)
    e = pl.program_id(1)
    c = pl.program_id(2)
    tmm = x_ref.shape[0]
    lane = _iota((1, LANES), 1).astype(F32)
    ef = e.astype(F32)

    @pl.when((e == 0) & (c == 0))
    def _():
        x = x_ref[...]
        xh = x.astype(BF16)
        xl = (x - xh.astype(F32)).astype(BF16)
        logits = _dot(xh, wrh_ref[...]) + _dot(xl, wrh_ref[...]) + _dot(xh, wrl_ref[...])
        lg = jnp.where(lane < N_EXPERTS, logits, NEG_INF)
        m1 = jnp.max(lg, axis=-1, keepdims=True)
        i1 = jnp.min(jnp.where(lg == m1, lane, float(LANES)), axis=-1, keepdims=True)
        lg2 = jnp.where(lane == i1, NEG_INF, lg)
        m2 = jnp.max(lg2, axis=-1, keepdims=True)
        i2 = jnp.min(jnp.where(lg2 == m2, lane, float(LANES)), axis=-1, keepdims=True)
        e2 = jnp.exp(m2 - m1)
        den = 1.0 + e2
        comb = jnp.where(lane == i1, 1.0 / den, 0.0) + jnp.where(lane == i2, e2 / den, 0.0)
        real = (i * tmm + _iota((tmm, 1), 0)) < n_valid
        sel = jnp.where(real & ((lane == i1) | (lane == i2)), 1.0, 0.0)
        before = jnp.where(_iota((tmm, tmm), 0) > _iota((tmm, tmm), 1), 1.0, 0.0).astype(BF16)
        rank = _dot(before, sel.astype(BF16))
        for idx, val in enumerate((comb, sel, rank)):
            route_s[idx] = val
            routet_s[idx] = val.T
        xt_s[...] = x.T.astype(BF16)
        acct_s[...] = jnp.zeros_like(acct_s)

    def pick(ref3, idx):
        return jnp.sum(jnp.where(lane == ef, ref3[idx], 0.0), axis=-1, keepdims=True)

    count = jnp.sum(jnp.where(lane == ef, route_s[1], 0.0))
    n_blocks = (count.astype(jnp.int32) + blk - 1) // blk

    @pl.when(c == 0)
    def _():
        sel_col, rank_col = pick(route_s, 1), pick(route_s, 2)

        def compact(k, carry):
            slot = (k * blk + _iota((1, blk), 1)).astype(F32)
            onehot = jnp.where(rank_col == slot, sel_col, 0.0).astype(BF16)
            xct_s[k] = _dot(xt_s[...], onehot).astype(BF16)
            yct_s[k] = jnp.zeros(yct_s.shape[1:], F32)
            return carry

        lax.fori_loop(0, n_blocks, compact, 0)

    def expert(k, carry):
        xc = xct_s[k]
        a = _dot(wat_ref[...], xc)
        gg = _dot(wgt_ref[...], xc)
        yct_s[k] += _dot(wdt_ref[...], (jax.nn.silu(a) * gg).astype(BF16))
        return carry

    lax.fori_loop(0, n_blocks, expert, 0)

    @pl.when(c == n_chunks - 1)
    def _():
        w_row = routet_s[0, pl.ds(e, 1), :]
        sel_row = routet_s[1, pl.ds(e, 1), :]
        rank_row = routet_s[2, pl.ds(e, 1), :]

        def scatter(k, carry):
            slot = (k * blk + _iota((blk, 1), 0)).astype(F32)
            onehot = jnp.where(rank_row == slot, sel_row, 0.0).astype(BF16)
            y = yct_s[k]
            yh = y.astype(BF16)
            yl = (y - yh.astype(F32)).astype(BF16)
            acct_s[...] += w_row * (_dot(yh, onehot) + _dot(yl, onehot))
            return carry

        lax.fori_loop(0, n_blocks, scatter, 0)

    @pl.when((e == N_EXPERTS - 1) & (c == n_chunks - 1))
    def _():
        o_ref[...] = _layer_norm(alpha * x_ref[...] + acct_s[...].T, g_ref[...], b_ref[...])


def _moe_ln(x, wr_hi, wr_lo, w_gu_t, w_down_t, g, b, *, alpha, tmm, fc, blk):
    n, d = x.shape
    dff = w_down_t.shape[2]
    nch = dff // fc
    n_tiles = -(-n // tmm)
    n_pad = n_tiles * tmm
    xp = jnp.pad(x, ((0, n_pad - n), (0, 0)))
    max_blocks = -(-tmm // blk)
    out = pl.pallas_call(
        functools.partial(_moe_ln_kernel, alpha=alpha, n_chunks=nch, n_valid=n, blk=blk),
        grid=(n_tiles, N_EXPERTS, nch),
        in_specs=[pl.BlockSpec((tmm, d), lambda i, e, c: (i, 0)),
                  pl.BlockSpec((d, LANES), lambda i, e, c: (0, 0)),
                  pl.BlockSpec((d, LANES), lambda i, e, c: (0, 0)),
                  pl.BlockSpec((None, fc, d), lambda i, e, c: (e, c, 0)),
                  pl.BlockSpec((None, fc, d), lambda i, e, c: (e, c + nch, 0)),
                  pl.BlockSpec((None, d, fc), lambda i, e, c: (e, 0, c)),
                  pl.BlockSpec((1, d), lambda i, e, c: (0, 0)),
                  pl.BlockSpec((1, d), lambda i, e, c: (0, 0))],
        out_specs=pl.BlockSpec((tmm, d), lambda i, e, c: (i, 0)),
        out_shape=jax.ShapeDtypeStruct((n_pad, d), F32),
        scratch_shapes=[pltpu.VMEM((d, tmm), BF16), pltpu.VMEM((d, tmm), F32),
                        pltpu.VMEM((3, tmm, LANES), F32), pltpu.VMEM((3, LANES, tmm), F32),
                        pltpu.VMEM((max_blocks, d, blk), BF16), pltpu.VMEM((max_blocks, d, blk), F32)],
        compiler_params=pltpu.CompilerParams(dimension_semantics=("arbitrary", "arbitrary", "arbitrary"),
                                             vmem_limit_bytes=VMEM_LIMIT),
        name="moe_ln",
    )(xp, wr_hi, wr_lo, w_gu_t, w_gu_t, w_down_t, g, b)
    return out[:n]


def _page_specs(block, n_pages, pp, layer):
    tail = (0,) * (len(block) - 2)
    return [pl.BlockSpec(block, lambda s, p, pt, k=k: (pt[s * n_pages + p * pp + k], layer) + tail)
            for k in range(pp)]


def _log_sigmoid(z):
    return jnp.minimum(z, 0.0) - jnp.log1p(jnp.exp(-jnp.abs(z)))


def _forget_lanes(x):
    lane = _iota((1, LANES), 1)
    return jnp.where((lane >= MISC_FF) & (lane < MISC_FF + FOX_HEADS), x, 0.0)


def _cum_small(lf):
    rows = lf.shape[0]
    ri = _iota((rows, LANES), 0)
    cum = jnp.zeros((rows, LANES), F32)
    for t in range(rows):
        cum = cum + jnp.where(ri >= t, lf[t:t + 1, :], 0.0)
    return cum


def _cmp_summaries(kc_ref, vc_ref, pe_ref, phi_ref, crt_ref, nc):
    acc_k = jnp.zeros((nc, LANES), F32)
    acc_v = jnp.zeros((nc, LANES), F32)
    for l in range(CMP_BLOCK):
        rk = kc_ref[pl.ds(l, nc, stride=CMP_BLOCK), :] + pe_ref[0, l:l + 1, :]
        acc_k = acc_k + _dot(rk.astype(BF16), phi_ref[0, l])
        rv = vc_ref[pl.ds(l, nc, stride=CMP_BLOCK), :] + pe_ref[1, l:l + 1, :]
        acc_v = acc_v + _dot(rv.astype(BF16), phi_ref[1, l])
    ck = _rope128(acc_k, crt_ref[:, 0:128], crt_ref[:, 128:256], crt_ref[:, 256:384])
    return ck, acc_v


def _nsa_cmp_select(q4, ck_e, ck_o, cv_e, cv_o, qpos, *, tq, nc, ns, topn):
    qpos4 = _tile_rows(qpos, NSA_GROUP)
    blk = _iota((1, SEL_W), 1)
    n_even, n_odd = (nc + 1) // 2, nc // 2
    ok_e = ((2 * blk + 1) * CMP_BLOCK - 1 <= qpos4) & (blk < n_even)
    ok_o = ((2 * blk + 2) * CMP_BLOCK - 1 <= qpos4) & (blk < n_odd)
    s_e = jnp.where(ok_e, _dot_nt(q4, ck_e), NEG_INF)
    s_o = jnp.where(ok_o, _dot_nt(q4, ck_o), NEG_INF)
    m = jnp.maximum(jnp.max(s_e, axis=-1, keepdims=True), jnp.max(s_o, axis=-1, keepdims=True))
    m = jnp.where(m == NEG_INF, 0.0, m)
    p_e = jnp.exp(s_e - m)
    p_o = jnp.exp(s_o - m)
    den = jnp.sum(p_e, axis=-1, keepdims=True) + jnp.sum(p_o, axis=-1, keepdims=True)
    inv = 1.0 / jnp.maximum(den, 1e-30)
    p_e = p_e * inv
    p_o = p_o * inv
    o_cmp = _dot(p_e.astype(BF16), cv_e) + _dot(p_o.astype(BF16), cv_o)
    pp = p_e + p_o
    imp = pp[0:tq]
    for r in range(1, NSA_GROUP):
        imp = imp + pp[r * tq:(r + 1) * tq]
    valid = blk * SLC_BLOCK <= qpos
    forced = (blk == 0) | (blk == (qpos >> 6))
    score = jnp.where(forced, jnp.inf, jnp.where(valid, imp, NEG_INF))
    score = jnp.where(blk < ns, score, NEG_INF)
    rank = jnp.zeros((tq, SEL_W), F32)
    for b2 in range(ns):
        col = score[:, b2:b2 + 1]
        beats = (col > score) | ((col == score) & (blk > b2))
        rank = rank + jnp.where(beats, 1.0, 0.0)
    sel = jnp.where(rank < topn, 1.0, 0.0)
    return o_cmp, sel


def _block_expand(pos0, n):
    return jnp.where(_iota((SEL_W, n), 0) == ((pos0 + _iota((SEL_W, n), 1)) >> 6), 1.0, 0.0).astype(BF16)


def _nsa_q4(hq_ref, g):
    q4 = jnp.concatenate([hq_ref[:, (NSA_GROUP * g + r) * 64:(NSA_GROUP * g + r + 1) * 64]
                          for r in range(NSA_GROUP)], axis=0)
    return (q4 * 0.125).astype(BF16)


def _nsa_write(o_ref, gates, g, tq, o_cmp, o_slc, o_win):
    for r in range(NSA_GROUP):
        hh = NSA_GROUP * g + r
        rs = slice(r * tq, (r + 1) * tq)
        c0 = MISC_GATES + 3 * hh
        o_ref[:, hh * 64:(hh + 1) * 64] = (gates[:, c0:c0 + 1] * o_cmp[rs] + gates[:, c0 + 1:c0 + 2] * o_slc[rs]
                                           + gates[:, c0 + 2:c0 + 3] * o_win[rs])


def _kth_largest_key(key_ref, n_keep, count, row_shape, bits):
    kk = float(n_keep)

    def descend(step, thr):
        shift = 32 - bits * (step + 1)
        digit = jnp.zeros(row_shape, jnp.int32)
        for d in range(1, 1 << bits):
            cand = thr + (jnp.int32(d) << shift)
            digit = digit + jnp.where(count(key_ref[...] >= cand) >= kk, 1, 0)
        return thr + (digit << shift)

    return lax.fori_loop(0, 32 // bits, descend, jnp.full(row_shape, -2147483648, jnp.int32))


def _topk_bias_store(out_ref, key_ref, n_keep, idx, count, row_shape, bits=1):
    kk = float(n_keep)
    thr = _kth_largest_key(key_ref, n_keep, count, row_shape, bits)
    has_ties = jnp.max(count(key_ref[...] >= thr)) > kk

    @pl.when(jnp.logical_not(has_ties))
    def _():
        out_ref[...] = jnp.where(key_ref[...] >= thr, 0.0, NEG_INF)

    @pl.when(has_ties)
    def _():
        keys = key_ref[...]
        gt = keys > thr
        eq = keys == thr
        need = kk - count(gt)

        def widen(b, bound):
            cand = bound + (jnp.int32(1) << (15 - b))
            return jnp.where(count(eq & (idx < cand)) <= need, cand, bound)

        bound = lax.fori_loop(0, 16, widen, jnp.zeros(row_shape, jnp.int32))
        out_ref[...] = jnp.where(gt | (eq & (idx < bound)), 0.0, NEG_INF)


def _sortable_key(score):
    bits = pltpu.bitcast(score, jnp.int32)
    return bits ^ ((bits >> 31) & jnp.int32(0x7FFFFFFF))


def _cum_tri(lf, carry):
    rows = lf.shape[0]
    tri = jnp.where(_iota((rows, rows), 0) >= _iota((rows, rows), 1), 1.0, 0.0).astype(BF16)
    hi, mid, lo = _split3(lf)
    return _dot(tri, hi.astype(BF16)) + _dot(tri, mid.astype(BF16)) + _dot(tri, lo.astype(BF16)) + carry


def _fox_aug_cols(cum3, hh):
    chi, cmid, clo = [part[:, MISC_FF + hh:MISC_FF + hh + 1] for part in cum3]
    l64 = _iota((1, HEAD_DIM), 1)
    qx = jnp.where(l64 == 0, chi, jnp.where(l64 == 1, cmid, jnp.where(l64 == 2, clo, jnp.where(l64 < 6, 1.0, 0.0))))
    kx = jnp.where(l64 < 3, 1.0,
                   jnp.where(l64 == 3, -chi, jnp.where(l64 == 4, -cmid, jnp.where(l64 == 5, -clo, 0.0))))
    return qx, kx


def _fox_prep_kernel(fq_ref, fk_ref, misc_ref, fb_ref, qa_ref, ka_ref, lf_ref, carry_ref):
    j = pl.program_id(1)

    @pl.when(j == 0)
    def _():
        carry_ref[...] = jnp.zeros_like(carry_ref)

    lf = _forget_lanes(_log_sigmoid(misc_ref[...] + fb_ref[...]))
    lf_ref[...] = lf
    cum = _cum_tri(lf, carry_ref[...])
    rows = lf.shape[0]
    carry_ref[...] = cum[rows - 1:rows, :]
    cum3 = _split3(cum)
    for hh in range(FOX_HEADS):
        qx, kx = _fox_aug_cols(cum3, hh)
        qa_ref[:, hh * 128:hh * 128 + 64] = (fq_ref[:, hh * 64:(hh + 1) * 64] * 0.125).astype(BF16)
        qa_ref[:, hh * 128 + 64:(hh + 1) * 128] = qx.astype(BF16)
        ka_ref[:, hh * 128:hh * 128 + 64] = fk_ref[:, hh * 64:(hh + 1) * 64].astype(BF16)
        ka_ref[:, hh * 128 + 64:(hh + 1) * 128] = kx.astype(BF16)


def _fox_prep(h, fb, *, n_seq, t, rows):
    nr = t // rows
    n = n_seq * t
    return pl.pallas_call(
        _fox_prep_kernel,
        grid=(n_seq, nr),
        in_specs=[pl.BlockSpec((rows, 512), lambda s, j: (s * nr + j, E_FQ // 512)),
                  pl.BlockSpec((rows, 512), lambda s, j: (s * nr + j, E_FK // 512)),
                  pl.BlockSpec((rows, LANES), lambda s, j: (s * nr + j, E_MISC // LANES)),
                  pl.BlockSpec((1, LANES), lambda s, j: (0, 0))],
        out_specs=[pl.BlockSpec((rows, FOX_HEADS * 128), lambda s, j: (s * nr + j, 0)),
                   pl.BlockSpec((rows, FOX_HEADS * 128), lambda s, j: (s * nr + j, 0)),
                   pl.BlockSpec((rows, LANES), lambda s, j: (s * nr + j, 0))],
        out_shape=[jax.ShapeDtypeStruct((n, FOX_HEADS * 128), BF16),
                   jax.ShapeDtypeStruct((n, FOX_HEADS * 128), BF16),
                   jax.ShapeDtypeStruct((n, LANES), F32)],
        scratch_shapes=[pltpu.VMEM((1, LANES), F32)],
        compiler_params=pltpu.CompilerParams(dimension_semantics=("arbitrary", "arbitrary")),
        name="fox_prep",
    )(h, h, h, fb)


def _fox_attn_kernel(qa_ref, ka_ref, v_ref, o_ref, m_s, l_s, acc_s, *, tq, t, hps):
    i = pl.program_id(2)
    qpos = i * tq + _iota((tq, 1), 0)
    q_end = i * tq + tq - 1
    for hh in range(hps):
        _st_reset(m_s.at[hh], l_s.at[hh], acc_s.at[hh])
    for r0, n in _key_chunks(t, KEY_CHUNK):
        @pl.when(r0 <= q_end)
        def _():
            bias = jnp.where((r0 + _iota((1, n), 1)) <= qpos, 0.0, NEG_INF)
            for hh in range(hps):
                k = ka_ref[r0:r0 + n, hh * 128:(hh + 1) * 128]
                v = v_ref[r0:r0 + n, hh * 64:(hh + 1) * 64].astype(BF16)
                s = _dot_nt(qa_ref[:, hh * 128:(hh + 1) * 128], k) + bias
                _st_update(m_s.at[hh], l_s.at[hh], acc_s.at[hh], s, lambda pb, v=v: _dot(pb, v))
    for hh in range(hps):
        o_ref[:, hh * 64:(hh + 1) * 64] = _st_finish(l_s.at[hh], acc_s.at[hh])


def _fox_attn(qa, ka, h, *, n_seq, t, tq, hps):
    nq = t // tq
    n = n_seq * t
    return pl.pallas_call(
        functools.partial(_fox_attn_kernel, tq=tq, t=t, hps=hps),
        grid=(n_seq, FOX_HEADS // hps, nq),
        in_specs=[pl.BlockSpec((tq, hps * 128), lambda s, p, i: (s * nq + i, p)),
                  pl.BlockSpec((t, hps * 128), lambda s, p, i: (s, p)),
                  pl.BlockSpec((t, hps * 64), lambda s, p, i: (s, E_FV // (hps * 64) + p))],
        out_specs=pl.BlockSpec((tq, hps * 64), lambda s, p, i: (s * nq + i, p)),
        out_shape=jax.ShapeDtypeStruct((n, FOX_HEADS * HEAD_DIM), F32),
        scratch_shapes=[pltpu.VMEM((hps, tq, 1), F32), pltpu.VMEM((hps, tq, 1), F32),
                        pltpu.VMEM((hps, tq, HEAD_DIM), F32)],
        compiler_params=pltpu.CompilerParams(dimension_semantics=("arbitrary", "arbitrary", "arbitrary"),
                                             vmem_limit_bytes=VMEM_LIMIT),
        name="fox_attn",
    )(qa, ka, h)


def _nsa_kernel(hq_ref, misc_ref, kc_ref, vc_ref, slc_ref, win_ref, pe_ref, phi_ref, crt_ref, o_ref,
                ck_s, cv_s, m_s, l_s, acc_s, *, tq, t, nc, ns, topn):
    i = pl.program_id(1)

    @pl.when(i == 0)
    def _():
        ck, cv = _cmp_summaries(kc_ref, vc_ref, pe_ref, phi_ref, crt_ref, nc)
        ck_s[...] = jnp.zeros_like(ck_s)
        cv_s[...] = jnp.zeros_like(cv_s)
        ck_s[0:nc, :] = ck
        cv_s[0:nc, :] = cv

    qs = i * tq
    q_end = qs + tq - 1
    qpos = qs + _iota((tq, 1), 0)
    gates = jax.nn.sigmoid(misc_ref[...])
    n_win = min(WINDOW + tq, t)
    w0 = pl.multiple_of(jnp.maximum(qs + tq - n_win, 0), 8)

    groups = range(NSA_KV_HEADS)
    kcols = [slice(g * 64, (g + 1) * 64) for g in groups]
    vcols = [slice(128 + g * 64, 128 + (g + 1) * 64) for g in groups]
    qh = [(hq_ref[:, hh * 64:(hh + 1) * 64] * 0.125).astype(BF16) for hh in range(NSA_HEADS)]
    o_cmps, sels = [], []
    for g in groups:
        q4 = jnp.concatenate(qh[NSA_GROUP * g:NSA_GROUP * (g + 1)], axis=0)
        ck_e = ck_s[pl.ds(0, SEL_W, stride=2), :][:, kcols[g]].astype(BF16)
        ck_o = ck_s[pl.ds(1, SEL_W, stride=2), :][:, kcols[g]].astype(BF16)
        cv_e = cv_s[pl.ds(0, SEL_W, stride=2), :][:, kcols[g]].astype(BF16)
        cv_o = cv_s[pl.ds(1, SEL_W, stride=2), :][:, kcols[g]].astype(BF16)
        o_cmp, sel = _nsa_cmp_select(q4, ck_e, ck_o, cv_e, cv_o, qpos, tq=tq, nc=nc, ns=ns, topn=topn)
        o_cmps.append(o_cmp)
        sels.append(sel.astype(BF16))
    for hh in range(NSA_HEADS):
        _st_reset(m_s.at[hh], l_s.at[hh], acc_s.at[hh])

    for r0, n in _key_chunks(t, KEY_CHUNK):
        @pl.when(r0 <= q_end)
        def _():
            expand = _block_expand(r0, n)
            causal = (r0 + _iota((1, n), 1)) <= qpos
            for g in groups:
                k = slc_ref[r0:r0 + n, kcols[g]].astype(BF16)
                v = slc_ref[r0:r0 + n, vcols[g]].astype(BF16)
                bias = jnp.where((_dot(sels[g], expand) > 0.5) & causal, 0.0, NEG_INF)
                for r in range(NSA_GROUP):
                    hh = NSA_GROUP * g + r
                    _st_update(m_s.at[hh], l_s.at[hh], acc_s.at[hh], _dot_nt(qh[hh], k) + bias,
                               lambda pb, v=v: _dot(pb, v))

    kpos = w0 + _iota((1, n_win), 1)
    wbias = jnp.where((kpos <= qpos) & (kpos > qpos - WINDOW), 0.0, NEG_INF)
    for g in groups:
        k = win_ref[pl.ds(w0, n_win), kcols[g]].astype(BF16)
        v = win_ref[pl.ds(w0, n_win), vcols[g]].astype(BF16)
        for r in range(NSA_GROUP):
            hh = NSA_GROUP * g + r
            o_slc = _st_finish(l_s.at[hh], acc_s.at[hh])
            s = _dot_nt(qh[hh], k) + wbias
            p = jnp.exp(s - jnp.max(s, axis=-1, keepdims=True))
            o_win = _dot(p.astype(BF16), v) / jnp.maximum(jnp.sum(p, axis=-1, keepdims=True), 1e-30)
            c0 = MISC_GATES + 3 * hh
            o_ref[:, hh * 64:(hh + 1) * 64] = (gates[:, c0:c0 + 1] * o_cmps[g][r * tq:(r + 1) * tq]
                                               + gates[:, c0 + 1:c0 + 2] * o_slc + gates[:, c0 + 2:c0 + 3] * o_win)


def _nsa_attn(h, pe, phi, crt, *, n_seq, t, tq, topn):
    nq = t // tq
    n = n_seq * t
    nc = t // CMP_BLOCK
    ns = -(-t // SLC_BLOCK)
    assert ns <= SEL_W and nc <= 2 * SEL_W
    rows4 = NSA_GROUP * tq

    def qblk(col):
        return lambda s, i: (s * nq + i, col)

    def kblk(col):
        return lambda s, i: (s, col)

    return pl.pallas_call(
        functools.partial(_nsa_kernel, tq=tq, t=t, nc=nc, ns=ns, topn=min(topn, ns)),
        grid=(n_seq, nq),
        in_specs=[pl.BlockSpec((tq, 512), qblk(E_NQ // 512)),
                  pl.BlockSpec((tq, LANES), qblk(E_MISC // LANES)),
                  pl.BlockSpec((t, LANES), kblk(E_CMP // LANES)),
                  pl.BlockSpec((t, LANES), kblk(E_CMP // LANES + 1)),
                  pl.BlockSpec((t, 256), kblk(E_SLC // 256)),
                  pl.BlockSpec((t, 256), kblk(E_WIN // 256)),
                  pl.BlockSpec(pe.shape, lambda s, i: (0, 0, 0)),
                  pl.BlockSpec(phi.shape, lambda s, i: (0, 0, 0, 0)),
                  pl.BlockSpec(crt.shape, lambda s, i: (0, 0))],
        out_specs=pl.BlockSpec((tq, NSA_HEADS * HEAD_DIM), lambda s, i: (s * nq + i, 0)),
        out_shape=jax.ShapeDtypeStruct((n, NSA_HEADS * HEAD_DIM), F32),
        scratch_shapes=[pltpu.VMEM((2 * SEL_W, LANES), F32), pltpu.VMEM((2 * SEL_W, LANES), F32),
                        pltpu.VMEM((NSA_HEADS, tq, 1), F32), pltpu.VMEM((NSA_HEADS, tq, 1), F32),
                        pltpu.VMEM((NSA_HEADS, tq, HEAD_DIM), F32)],
        compiler_params=pltpu.CompilerParams(dimension_semantics=("arbitrary", "arbitrary"),
                                             vmem_limit_bytes=VMEM_LIMIT),
        name="nsa_attn",
    )(h, h, h, h, h, h, pe, phi, crt)


def _dsa_kernel(q_ref, qi_ref, kw_ref, kv_ref, ki_ref, o_ref, sc_s, key_s, m_s, l_s, acc_s, *, tq, t, n_keep):
    i = pl.program_id(1)
    qpos = i * tq + _iota((tq, 1), 0)
    q_end = i * tq + tq - 1
    chunks = _key_chunks(t, KEY_CHUNK)

    sc_s[...] = jnp.full(sc_s.shape, NEG_INF, F32)
    wi = kw_ref[:, IDX_DIM:IDX_DIM + IDX_HEADS] * (IDX_HEADS ** -0.5)
    qi = [(qi_ref[:, hh * IDX_DIM:(hh + 1) * IDX_DIM] * (IDX_DIM ** -0.5)).astype(BF16) for hh in range(IDX_HEADS)]
    for r0, n in chunks:
        @pl.when(r0 <= q_end)
        def _():
            ki = ki_ref[r0:r0 + n, 0:IDX_DIM].astype(BF16)
            sc = jnp.zeros((tq, n), F32)
            for hh in range(IDX_HEADS):
                sc = sc + wi[:, hh:hh + 1] * jnp.maximum(_dot_nt(qi[hh], ki), 0.0)
            ok = (r0 + _iota((1, n), 1)) <= qpos
            sc_s[:, r0:r0 + n] = jnp.where(ok, sc + 0.0, NEG_INF)

    key_s[...] = _sortable_key(sc_s[...])
    _topk_bias_store(sc_s, key_s, n_keep, _iota((1, t), 1),
                     lambda mask: jnp.sum(jnp.where(mask, 1.0, 0.0), axis=-1, keepdims=True), (tq, 1))

    qs = [(q_ref[:, hh * 64:(hh + 1) * 64] * 0.125).astype(BF16) for hh in range(DSA_HEADS)]
    for hh in range(DSA_HEADS):
        _st_reset(m_s.at[hh], l_s.at[hh], acc_s.at[hh])
    for r0, n in chunks:
        @pl.when(r0 <= q_end)
        def _():
            ok = (r0 + _iota((1, n), 1)) <= qpos
            bias = jnp.where(ok, sc_s[:, r0:r0 + n], NEG_INF)
            for kh in range(DSA_KV_HEADS):
                k = kv_ref[r0:r0 + n, kh * 64:(kh + 1) * 64].astype(BF16)
                v = kv_ref[r0:r0 + n, 256 + kh * 64:256 + (kh + 1) * 64].astype(BF16)
                for r in range(DSA_GROUP):
                    hh = DSA_GROUP * kh + r
                    _st_update(m_s.at[hh], l_s.at[hh], acc_s.at[hh], _dot_nt(qs[hh], k) + bias,
                               lambda pb, v=v: _dot(pb, v))
    for hh in range(DSA_HEADS):
        o_ref[:, hh * 64:(hh + 1) * 64] = _st_finish(l_s.at[hh], acc_s.at[hh])


def _dsa_attn(h, *, n_seq, t, tq, n_keep):
    nq = t // tq
    n = n_seq * t
    rows4 = DSA_GROUP * tq
    assert t % LANES == 0 and t <= 1 << 16

    def qblk(w, col):
        return pl.BlockSpec((tq, w), lambda s, i: (s * nq + i, col))

    return pl.pallas_call(
        functools.partial(_dsa_kernel, tq=tq, t=t, n_keep=n_keep),
        grid=(n_seq, nq),
        in_specs=[qblk(1024, O_Q // 1024), qblk(512, O_QI // 512), qblk(LANES, O_KI // LANES),
                  pl.BlockSpec((t, 512), lambda s, i: (s, O_KV // 512)),
                  pl.BlockSpec((t, LANES), lambda s, i: (s, O_KI // LANES))],
        out_specs=pl.BlockSpec((tq, DSA_HEADS * HEAD_DIM), lambda s, i: (s * nq + i, 0)),
        out_shape=jax.ShapeDtypeStruct((n, DSA_HEADS * HEAD_DIM), F32),
        scratch_shapes=[pltpu.VMEM((tq, t), F32), pltpu.VMEM((tq, t), jnp.int32),
                        pltpu.VMEM((DSA_HEADS, tq, 1), F32), pltpu.VMEM((DSA_HEADS, tq, 1), F32),
                        pltpu.VMEM((DSA_HEADS, tq, HEAD_DIM), F32)],
        compiler_params=pltpu.CompilerParams(dimension_semantics=("arbitrary", "arbitrary"),
                                             vmem_limit_bytes=VMEM_LIMIT),
        name="dsa_attn",
    )(h, h, h, h, h)


def _cmp_sample_kernel(pt_ref, *refs, pp, n_steps, nc):
    slabs = refs[:pp]
    pe_ref, phi_ref, crt_ref, o_ref, k_s, v_s, c_s = refs[pp:]
    p = pl.program_id(1)
    for k in range(pp):
        row0 = pl.multiple_of((p * pp + k) * PAGE_SIZE, PAGE_SIZE)
        for g in range(NSA_KV_HEADS):
            k_s[pl.ds(row0, PAGE_SIZE), g * 64:(g + 1) * 64] = slabs[k][0, g].T
            v_s[pl.ds(row0, PAGE_SIZE), g * 64:(g + 1) * 64] = slabs[k][1, g].T

    @pl.when(p == n_steps - 1)
    def _():
        ck, cv = _cmp_summaries(k_s, v_s, pe_ref, phi_ref, crt_ref, nc)
        for idx, val in ((0, ck), (2, cv)):
            c_s[...] = jnp.zeros_like(c_s)
            c_s[0:nc, :] = val
            o_ref[idx] = c_s[pl.ds(0, SEL_W, stride=2), :]
            o_ref[idx + 1] = c_s[pl.ds(1, SEL_W, stride=2), :]


def _cmp_sample(cmp_v, pt_flat, pe, phi, crt, *, layer, n_seq, n_pages, pp, nc):
    lp = n_pages * PAGE_SIZE
    n_steps = n_pages // pp
    in_specs = _page_specs((None, None, 2, NSA_KV_HEADS, HEAD_DIM, PAGE_SIZE), n_pages, pp, layer)
    in_specs += [pl.BlockSpec(pe.shape, lambda s, p, pt: (0, 0, 0)),
                 pl.BlockSpec(phi.shape, lambda s, p, pt: (0, 0, 0, 0)),
                 pl.BlockSpec(crt.shape, lambda s, p, pt: (0, 0))]
    return pl.pallas_call(
        functools.partial(_cmp_sample_kernel, pp=pp, n_steps=n_steps, nc=nc),
        grid_spec=pltpu.PrefetchScalarGridSpec(
            num_scalar_prefetch=1, grid=(n_seq, n_steps), in_specs=in_specs,
            out_specs=pl.BlockSpec((None, 4, SEL_W, LANES), lambda s, p, pt: (s, 0, 0, 0)),
            scratch_shapes=[pltpu.VMEM((lp, LANES), F32), pltpu.VMEM((lp, LANES), F32),
                            pltpu.VMEM((2 * SEL_W, LANES), F32)]),
        out_shape=jax.ShapeDtypeStruct((n_seq, 4, SEL_W, LANES), F32),
        compiler_params=pltpu.CompilerParams(dimension_semantics=("arbitrary", "arbitrary"),
                                             vmem_limit_bytes=VMEM_LIMIT),
        name="cmp_sample",
    )(pt_flat, *([cmp_v] * pp), pe, phi, crt)


def _nsa_sample_kernel(pt_ref, *refs, pp, n_steps, dt, l_past, w_past, nc, ns, topn):
    hq_ref, misc_ref, slc_ref, win_ref, ckv_ref, winp_ref = refs[:6]
    slabs = refs[6:6 + pp]
    o_ref, sel_s, ocmp_s, m_s, l_s, acc_s = refs[6 + pp:]
    p = pl.program_id(1)
    qpos = l_past + _iota((dt, 1), 0)
    new_pos = l_past + _iota((1, LANES), 1)
    new_ok = (new_pos <= qpos) & (new_pos < l_past + dt)

    @pl.when(p == 0)
    def _():
        for g in range(NSA_KV_HEADS):
            kcol = slice(g * 64, (g + 1) * 64)
            o_cmp, sel = _nsa_cmp_select(
                _nsa_q4(hq_ref, g), ckv_ref[0][:, kcol].astype(BF16), ckv_ref[1][:, kcol].astype(BF16),
                ckv_ref[2][:, kcol].astype(BF16), ckv_ref[3][:, kcol].astype(BF16), qpos,
                tq=dt, nc=nc, ns=ns, topn=topn)
            ocmp_s[g] = o_cmp
            sel_s[g] = sel
            _st_reset(m_s.at[g], l_s.at[g], acc_s.at[g])

    for g in range(NSA_KV_HEADS):
        q4 = _nsa_q4(hq_ref, g)
        sel = sel_s[g].astype(BF16)
        blocks = []
        for k in range(pp):
            kt = slabs[k][0, g].astype(BF16)
            vt = slabs[k][1, g].astype(BF16)
            selx = _dot(sel, _block_expand((p * pp + k) * PAGE_SIZE, PAGE_SIZE))
            bias = jnp.where(selx > 0.5, 0.0, NEG_INF)
            blocks.append((_dot(q4, kt) + _tile_rows(bias, NSA_GROUP), lambda pb, vt=vt: _dot_nt(pb, vt)))
        _st_update_blocks(m_s.at[g], l_s.at[g], acc_s.at[g], blocks)

    @pl.when(p == n_steps - 1)
    def _():
        gates = jax.nn.sigmoid(misc_ref[...])
        for g in range(NSA_KV_HEADS):
            kcol = slice(g * 64, (g + 1) * 64)
            vcol = slice(128 + g * 64, 128 + (g + 1) * 64)
            q4 = _nsa_q4(hq_ref, g)
            st = (m_s.at[g], l_s.at[g], acc_s.at[g])
            k = _pad_rows(slc_ref[:, kcol].astype(BF16), LANES)
            v = _pad_rows(slc_ref[:, vcol].astype(BF16), LANES)
            selx = _dot(sel_s[g].astype(BF16), _block_expand(l_past, LANES))
            bias = jnp.where((selx > 0.5) & new_ok, 0.0, NEG_INF)
            _st_update(*st, _dot_nt(q4, k) + _tile_rows(bias, NSA_GROUP), lambda pb: _dot(pb, v))
            o_slc = _st_finish(st[1], st[2])
            _st_reset(*st)
            kt = winp_ref[0, g].astype(BF16)
            vt = winp_ref[1, g].astype(BF16)
            wpos = l_past - w_past + _iota((1, w_past), 1)
            bias = jnp.where(wpos > qpos - WINDOW, 0.0, NEG_INF)
            _st_update(*st, _dot(q4, kt) + _tile_rows(bias, NSA_GROUP), lambda pb: _dot_nt(pb, vt))
            k = _pad_rows(win_ref[:, kcol].astype(BF16), LANES)
            v = _pad_rows(win_ref[:, vcol].astype(BF16), LANES)
            bias = jnp.where(new_ok & (new_pos > qpos - WINDOW), 0.0, NEG_INF)
            _st_update(*st, _dot_nt(q4, k) + _tile_rows(bias, NSA_GROUP), lambda pb: _dot(pb, v))
            o_win = _st_finish(st[1], st[2])
            _nsa_write(o_ref, gates, g, dt, ocmp_s[g], o_slc, o_win)


def _nsa_sample(h, ckv, slc_v, win_v, pt_flat, *, layer, row0, n_seq, dt, n_pages, pp, w_past, topn):
    l_past = n_pages * PAGE_SIZE
    l_tot = l_past + dt
    nc = l_tot // CMP_BLOCK
    ns = -(-l_tot // SLC_BLOCK)
    assert ns <= SEL_W and nc <= 2 * SEL_W and dt <= LANES
    n_steps = n_pages // pp
    rb0 = row0 // dt
    rows4 = NSA_GROUP * dt

    def hblk(w, col):
        return pl.BlockSpec((dt, w), lambda s, p, pt: (rb0 + s, col))

    in_specs = [hblk(512, E_NQ // 512), hblk(LANES, E_MISC // LANES), hblk(256, E_SLC // 256), hblk(256, E_WIN // 256),
                pl.BlockSpec((None, 4, SEL_W, LANES), lambda s, p, pt: (s, 0, 0, 0)),
                pl.BlockSpec((None, None, 2, NSA_KV_HEADS, HEAD_DIM, w_past), lambda s, p, pt: (s, layer, 0, 0, 0, 0))]
    in_specs += _page_specs((None, None, 2, NSA_KV_HEADS, HEAD_DIM, PAGE_SIZE), n_pages, pp, layer)
    return pl.pallas_call(
        functools.partial(_nsa_sample_kernel, pp=pp, n_steps=n_steps, dt=dt, l_past=l_past, w_past=w_past,
                          nc=nc, ns=ns, topn=min(topn, ns)),
        grid_spec=pltpu.PrefetchScalarGridSpec(
            num_scalar_prefetch=1, grid=(n_seq, n_steps), in_specs=in_specs,
            out_specs=pl.BlockSpec((dt, NSA_HEADS * HEAD_DIM), lambda s, p, pt: (s, 0)),
            scratch_shapes=[pltpu.VMEM((NSA_KV_HEADS, dt, SEL_W), F32), pltpu.VMEM((NSA_KV_HEADS, rows4, HEAD_DIM), F32),
                            pltpu.VMEM((NSA_KV_HEADS, rows4, 1), F32), pltpu.VMEM((NSA_KV_HEADS, rows4, 1), F32),
                            pltpu.VMEM((NSA_KV_HEADS, rows4, HEAD_DIM), F32)]),
        out_shape=jax.ShapeDtypeStruct((n_seq * dt, NSA_HEADS * HEAD_DIM), F32),
        compiler_params=pltpu.CompilerParams(dimension_semantics=("arbitrary", "arbitrary"),
                                             vmem_limit_bytes=VMEM_LIMIT),
        name="nsa_sample",
    )(pt_flat, h, h, h, h, ckv, win_v, *([slc_v] * pp))


def _fox_sample_kernel(pt_ref, *refs, pp, n_steps, dt, l_past):
    fq_ref, fk_ref, fv_ref, misc_ref, fb_ref = refs[:5]
    kv_slabs = refs[5:5 + pp]
    lf_slabs = refs[5 + pp:5 + 2 * pp]
    o_ref, lf_ref, qbd_s, carry_s, m_s, l_s, acc_s = refs[5 + 2 * pp:]
    p = pl.program_id(1)
    rows = FOX_HEADS * dt
    width = FOX_HEADS * HEAD_DIM

    @pl.when(p == 0)
    def _():
        qbd_s[...] = jnp.zeros_like(qbd_s)
        for hh in range(FOX_HEADS):
            cs = slice(hh * 64, (hh + 1) * 64)
            qbd_s[hh * dt:(hh + 1) * dt, cs] = (fq_ref[:, cs] * 0.125).astype(BF16)
        carry_s[...] = jnp.zeros_like(carry_s)
        _st_reset(m_s, l_s, acc_s)

    def head_rows(x):
        return jnp.concatenate([jnp.broadcast_to(x[hh:hh + 1, :], (dt, LANES)) for hh in range(FOX_HEADS)], axis=0)

    upper = jnp.where(_iota((LANES, LANES), 0) <= _iota((LANES, LANES), 1), 1.0, 0.0).astype(BF16)
    offset = carry_s[...]
    qbd = qbd_s[...]
    blocks = []
    for k in range(pp):
        hi, mid, lo = _split3(lf_slabs[k][...])
        local = _dot(hi.astype(BF16), upper) + _dot(mid.astype(BF16), upper) + _dot(lo.astype(BF16), upper)
        cum = local + offset
        offset = jnp.broadcast_to(cum[:, LANES - 1:LANES], (FOX_HEADS, LANES))
        kt = kv_slabs[k][0].reshape(width, PAGE_SIZE).astype(BF16)
        vt = kv_slabs[k][1].reshape(width, PAGE_SIZE).astype(BF16)
        blocks.append((_dot(qbd, kt) - head_rows(cum), lambda pb, vt=vt: _dot_nt(pb, vt)))
    carry_s[...] = offset
    _st_update_blocks(m_s, l_s, acc_s, blocks)

    @pl.when(p == n_steps - 1)
    def _():
        lf = _forget_lanes(_log_sigmoid(misc_ref[...] + fb_ref[...]))
        lf_ref[...] = lf
        parts = _split3(_pad_rows(_cum_small(lf), LANES))
        lane = _iota((dt, LANES), 1)
        pick = jnp.concatenate([jnp.where(lane == MISC_FF + hh, 1.0, 0.0) for hh in range(FOX_HEADS)], axis=0)
        pick = pick.astype(BF16)
        cum_new = sum(_dot_nt(pick, part.astype(BF16)) for part in parts)
        kn = _pad_rows(fk_ref[...].astype(BF16), LANES)
        vn = _pad_rows(fv_ref[...].astype(BF16), LANES)
        col = _iota((1, LANES), 1)
        trow = _tile_rows(_iota((dt, 1), 0), FOX_HEADS)
        ok = (col <= trow) & (col < dt)
        s = _dot_nt(qbd_s[...], kn) - (head_rows(carry_s[...]) + cum_new)
        _st_update(m_s, l_s, acc_s, jnp.where(ok, s, NEG_INF), lambda pb: _dot(pb, vn))
        o = _st_finish(l_s, acc_s)
        for hh in range(FOX_HEADS):
            o_ref[:, hh * 64:(hh + 1) * 64] = o[hh * dt:(hh + 1) * dt, hh * 64:(hh + 1) * 64]


def _fox_sample(h, fb, fox_v, flf_v, pt_flat, *, layer, row0, n_seq, dt, n_pages, pp):
    l_past = n_pages * PAGE_SIZE
    n_steps = n_pages // pp
    rb0 = row0 // dt
    rows = FOX_HEADS * dt
    width = FOX_HEADS * HEAD_DIM
    assert dt <= LANES

    def hblk(w, col):
        return pl.BlockSpec((dt, w), lambda s, p, pt: (rb0 + s, col))

    in_specs = [hblk(512, E_FQ // 512), hblk(512, E_FK // 512), hblk(512, E_FV // 512), hblk(LANES, E_MISC // LANES),
                pl.BlockSpec((1, LANES), lambda s, p, pt: (0, 0))]
    in_specs += _page_specs((None, None, 2, FOX_HEADS, HEAD_DIM, PAGE_SIZE), n_pages, pp, layer)
    in_specs += _page_specs((None, None, FOX_HEADS, PAGE_SIZE), n_pages, pp, layer)
    return pl.pallas_call(
        functools.partial(_fox_sample_kernel, pp=pp, n_steps=n_steps, dt=dt, l_past=l_past),
        grid_spec=pltpu.PrefetchScalarGridSpec(
            num_scalar_prefetch=1, grid=(n_seq, n_steps), in_specs=in_specs,
            out_specs=[pl.BlockSpec((dt, width), lambda s, p, pt: (s, 0)),
                       pl.BlockSpec((dt, LANES), lambda s, p, pt: (s, 0))],
            scratch_shapes=[pltpu.VMEM((rows, width), BF16), pltpu.VMEM((FOX_HEADS, LANES), F32),
                            pltpu.VMEM((rows, 1), F32), pltpu.VMEM((rows, 1), F32), pltpu.VMEM((rows, width), F32)]),
        out_shape=[jax.ShapeDtypeStruct((n_seq * dt, width), F32), jax.ShapeDtypeStruct((n_seq * dt, LANES), F32)],
        compiler_params=pltpu.CompilerParams(dimension_semantics=("arbitrary", "arbitrary"),
                                             vmem_limit_bytes=VMEM_LIMIT),
        name="fox_sample",
    )(pt_flat, h, h, h, h, fb, *([fox_v] * pp), *([flf_v] * pp))


def _dsa_index_kernel(pt_ref, *refs, pp, n_steps, dt, n_pages, n_keep):
    qi_ref, kw_ref = refs[:2]
    slabs = refs[2:2 + pp]
    o_ref, sc_s, key_s = refs[2 + pp:]
    p = pl.program_id(1)
    wi = kw_ref[:, IDX_DIM:IDX_DIM + IDX_HEADS] * (IDX_HEADS ** -0.5)
    qst = jnp.concatenate([qi_ref[:, hh * IDX_DIM:(hh + 1) * IDX_DIM] for hh in range(IDX_HEADS)], axis=0)
    qst = (qst * (IDX_DIM ** -0.5)).astype(BF16)

    def scores(lg):
        sc = jnp.zeros((dt, lg.shape[1]), F32)
        for hh in range(IDX_HEADS):
            sc = sc + wi[:, hh:hh + 1] * jnp.maximum(lg[hh * dt:(hh + 1) * dt], 0.0)
        return sc + 0.0

    for k in range(pp):
        sc_s[p * pp + k] = scores(_dot(qst, slabs[k][...].astype(BF16)))

    @pl.when(p == n_steps - 1)
    def _():
        kin = _pad_rows(kw_ref[:, 0:IDX_DIM].astype(BF16), LANES)
        col = _iota((1, LANES), 1)
        ok = (col <= _iota((dt, 1), 0)) & (col < dt)
        sc_s[n_pages] = jnp.where(ok, scores(_dot_nt(qst, kin)), NEG_INF)
        key_s[...] = _sortable_key(sc_s[...])
        shape = (n_pages + 1, dt, LANES)
        idx = _iota(shape, 0) * LANES + _iota(shape, 2)
        _topk_bias_store(o_ref, key_s, n_keep, idx,
                         lambda mask: jnp.sum(jnp.sum(jnp.where(mask, 1.0, 0.0), axis=0), axis=-1, keepdims=True)[None],
                         (1, dt, 1), bits=4)


def _dsa_index(h, dik_v, pt_flat, *, layer, row0, n_seq, dt, n_pages, pp, n_keep):
    n_steps = n_pages // pp
    rb0 = row0 // dt
    shape = (n_pages + 1, dt, LANES)
    in_specs = [pl.BlockSpec((dt, 512), lambda s, p, pt: (rb0 + s, O_QI // 512)),
                pl.BlockSpec((dt, LANES), lambda s, p, pt: (rb0 + s, O_KI // LANES))]
    in_specs += _page_specs((None, None, IDX_DIM, PAGE_SIZE), n_pages, pp, layer)
    return pl.pallas_call(
        functools.partial(_dsa_index_kernel, pp=pp, n_steps=n_steps, dt=dt, n_pages=n_pages, n_keep=n_keep),
        grid_spec=pltpu.PrefetchScalarGridSpec(
            num_scalar_prefetch=1, grid=(n_seq, n_steps), in_specs=in_specs,
            out_specs=pl.BlockSpec((None,) + shape, lambda s, p, pt: (s, 0, 0, 0)),
            scratch_shapes=[pltpu.VMEM(shape, F32), pltpu.VMEM(shape, jnp.int32)]),
        out_shape=jax.ShapeDtypeStruct((n_seq,) + shape, F32),
        compiler_params=pltpu.CompilerParams(dimension_semantics=("arbitrary", "arbitrary")),
        name="dsa_index",
    )(pt_flat, h, h, *([dik_v] * pp))


def _dsa_sample_kernel(pt_ref, *refs, pp, n_steps, dt, n_pages):
    q_ref, kv_ref, bias_ref = refs[:3]
    slabs = refs[3:3 + pp]
    o_ref, qbd_s, m_s, l_s, acc_s = refs[3 + pp:]
    p = pl.program_id(1)
    width = DSA_KV_HEADS * HEAD_DIM

    @pl.when(p == 0)
    def _():
        qbd_s[...] = jnp.zeros_like(qbd_s)
        for hh in range(DSA_HEADS):
            kh = hh // DSA_GROUP
            qbd_s[hh * dt:(hh + 1) * dt, kh * 64:(kh + 1) * 64] = (q_ref[:, hh * 64:(hh + 1) * 64] * 0.125).astype(BF16)
        _st_reset(m_s, l_s, acc_s)

    qbd = qbd_s[...]
    blocks = []
    for k in range(pp):
        kt = slabs[k][0].reshape(width, PAGE_SIZE).astype(BF16)
        vt = slabs[k][1].reshape(width, PAGE_SIZE).astype(BF16)
        s = _dot(qbd, kt) + _tile_rows(bias_ref[p * pp + k], DSA_HEADS)
        blocks.append((s, lambda pb, vt=vt: _dot_nt(pb, vt)))
    _st_update_blocks(m_s, l_s, acc_s, blocks)

    @pl.when(p == n_steps - 1)
    def _():
        kn = _pad_rows(kv_ref[:, 0:width].astype(BF16), LANES)
        vn = _pad_rows(kv_ref[:, width:2 * width].astype(BF16), LANES)
        s = _dot_nt(qbd_s[...], kn) + _tile_rows(bias_ref[n_pages], DSA_HEADS)
        _st_update(m_s, l_s, acc_s, s, lambda pb: _dot(pb, vn))
        o = _st_finish(l_s, acc_s)
        for hh in range(DSA_HEADS):
            kh = hh // DSA_GROUP
            o_ref[:, hh * 64:(hh + 1) * 64] = o[hh * dt:(hh + 1) * dt, kh * 64:(kh + 1) * 64]


def _dsa_sample(h, bias, dkv_v, pt_flat, *, layer, row0, n_seq, dt, n_pages, pp):
    n_steps = n_pages // pp
    rb0 = row0 // dt
    rows = DSA_HEADS * dt
    width = DSA_KV_HEADS * HEAD_DIM
    in_specs = [pl.BlockSpec((dt, 1024), lambda s, p, pt: (rb0 + s, O_Q // 1024)),
                pl.BlockSpec((dt, 512), lambda s, p, pt: (rb0 + s, O_KV // 512)),
                pl.BlockSpec((None, n_pages + 1, dt, LANES), lambda s, p, pt: (s, 0, 0, 0))]
    in_specs += _page_specs((None, None, 2, DSA_KV_HEADS, HEAD_DIM, PAGE_SIZE), n_pages, pp, layer)
    return pl.pallas_call(
        functools.partial(_dsa_sample_kernel, pp=pp, n_steps=n_steps, dt=dt, n_pages=n_pages),
        grid_spec=pltpu.PrefetchScalarGridSpec(
            num_scalar_prefetch=1, grid=(n_seq, n_steps), in_specs=in_specs,
            out_specs=pl.BlockSpec((dt, DSA_HEADS * HEAD_DIM), lambda s, p, pt: (s, 0)),
            scratch_shapes=[pltpu.VMEM((rows, width), BF16), pltpu.VMEM((rows, 1), F32), pltpu.VMEM((rows, 1), F32),
                            pltpu.VMEM((rows, width), F32)]),
        out_shape=jax.ShapeDtypeStruct((n_seq * dt, DSA_HEADS * HEAD_DIM), F32),
        compiler_params=pltpu.CompilerParams(dimension_semantics=("arbitrary", "arbitrary"),
                                             vmem_limit_bytes=VMEM_LIMIT),
        name="dsa_sample",
    )(pt_flat, h, h, bias, *([dkv_v] * pp))


def _rope_table(pos):
    half = ROT_DIM // 2
    inv = ROPE_THETA ** (-2.0 * jnp.arange(half, dtype=F32) / ROT_DIM)
    ang = pos.astype(F32)[:, None] * inv[None, :]
    cos, sin = jnp.cos(ang), jnp.sin(ang)
    n = pos.shape[0]
    one = jnp.ones((n, HEAD_DIM - ROT_DIM), F32)
    zero = jnp.zeros((n, HEAD_DIM - ROT_DIM), F32)
    z8 = jnp.zeros((n, half), F32)
    c = jnp.concatenate([cos, cos, one], axis=1)
    s1 = jnp.concatenate([z8, sin, zero], axis=1)
    s2 = jnp.concatenate([-sin, z8, zero], axis=1)
    return jnp.concatenate([c, c, s1, s1, s2, s2], axis=1)


def _largest_tile(limit, *sizes):
    t = limit
    while any(s % t for s in sizes):
        t //= 2
    return t


def _rows_last(pool):
    nd = pool.ndim
    return jnp.transpose(pool, (0, 1) + tuple(range(3, nd)) + (2,))


def kernel(x_prompt, x_sample, cache_nsa_cmp_kv, cache_nsa_slc_kv, state_nsa_win_kv, cache_fox_kv, cache_fox_logf, cache_dsa_kv, cache_dsa_idx_k, page_table, ln_g, ln_b, w_in_even, w_out_even, fox_f_bias, nsa_cmp_pos, nsa_cmp_phi, w_in_odd, w_out_odd, ffn_gu, ffn_down, moe_router, moe_gu, moe_down):
    nb, t, d = x_prompt.shape
    db, dt, _ = x_sample.shape
    n_pages = page_table.shape[1]
    lp = n_pages * PAGE_SIZE
    w_past = state_nsa_win_kv.shape[2]
    n_p, n_s = nb * t, db * dt
    depth = ln_g.shape[0]
    alpha = (2 * depth) ** 0.25
    tm = _largest_tile(512, n_p, n_s)
    tm_ff = _largest_tile(512, n_p, n_s)
    tq = _largest_tile(256, t)
    pp_small = n_pages
    pp_big = n_pages
    pt_flat = page_table.reshape(-1).astype(jnp.int32)

    cmp_v, slc_v, win_v = _rows_last(cache_nsa_cmp_kv), _rows_last(cache_nsa_slc_kv), _rows_last(state_nsa_win_kv)
    fox_v, flf_v = _rows_last(cache_fox_kv), _rows_last(cache_fox_logf)
    dkv_v, dik_v = _rows_last(cache_dsa_kv), _rows_last(cache_dsa_idx_k)

    x = jnp.concatenate([x_prompt.reshape(n_p, d), x_sample.reshape(n_s, d)], axis=0)
    pos = jnp.concatenate([jnp.tile(jnp.arange(t), nb), jnp.tile(lp + jnp.arange(dt), db)])
    rt = _rope_table(pos)
    nc_p, nc_s = t // CMP_BLOCK, (lp + dt) // CMP_BLOCK
    assert lp % CMP_BLOCK == 0 and dt < CMP_BLOCK
    crt_p = _rope_table((jnp.arange(nc_p) + 1) * CMP_BLOCK - 1)
    crt_s = _rope_table((jnp.arange(nc_s) + 1) * CMP_BLOCK - 1)

    ev_p, ev_s, od_p, od_s = [], [], [], []
    for layer in range(depth):
        j = layer // 2
        g0, b0 = ln_g[layer, 0][None, :], ln_b[layer, 0][None, :]
        g1, b1 = ln_g[layer, 1][None, :], ln_b[layer, 1][None, :]
        if layer % 2 == 0:
            w = w_in_even[j]
            w = jnp.concatenate([w[:, 0:512], w[:, 1304:2840], w[:, 512:1280], w[:, 1280:1304], w[:, 2840:2848],
                                 jnp.zeros((d, E_W - 2848), w.dtype)], axis=1).astype(BF16)
            h = _proj(x, w, rt, width=E_W, rope_full=EVEN_ROPE, rope_lo=(), tm=tm)

            pe = jnp.concatenate([nsa_cmp_pos[j], nsa_cmp_pos[j]], axis=-1)
            ph = nsa_cmp_phi[j]
            zz = jnp.zeros_like(ph)
            phi = jnp.concatenate([jnp.concatenate([ph, zz], axis=-1),
                                   jnp.concatenate([zz, ph], axis=-1)], axis=-2).astype(BF16)
            fb = jnp.zeros((1, LANES), F32).at[0, MISC_FF:MISC_FF + FOX_HEADS].set(fox_f_bias[j])

            qa_p, ka_p, lf_p = _fox_prep(h, fb, n_seq=nb, t=t, rows=_largest_tile(256, t))
            o_nsa_p = _nsa_attn(h, pe, phi, crt_p, n_seq=nb, t=t, tq=tq, topn=SLC_TOPN)
            o_fox_p = _fox_attn(qa_p, ka_p, h, n_seq=nb, t=t, tq=tq, hps=FOX_HEADS)

            ckv = _cmp_sample(cmp_v, pt_flat, pe, phi, crt_s, layer=j, n_seq=db, n_pages=n_pages, pp=pp_small, nc=nc_s)
            o_nsa_s = _nsa_sample(h, ckv, slc_v, win_v, pt_flat, layer=j, row0=n_p, n_seq=db, dt=dt, n_pages=n_pages,
                                  pp=pp_small, w_past=w_past, topn=SLC_TOPN)
            o_fox_s, lf_s = _fox_sample(h, fb, fox_v, flf_v, pt_flat, layer=j, row0=n_p, n_seq=db, dt=dt,
                                        n_pages=n_pages, pp=pp_big)

            wo = w_out_even[j].astype(BF16)
            x = _out_ln([o_nsa_p, o_fox_p], [o_nsa_s, o_fox_s], [wo[0:512], wo[512:1024]], x, g0, b0,
                        alpha=alpha, tm=tm)
            x = _ffn_ln(x, ffn_gu[j].astype(BF16), ffn_down[j].astype(BF16), g1, b1, alpha=alpha, tm=tm_ff,
                        fc=ffn_down.shape[1] // 2)

            hp, hs = h[:n_p].reshape(nb, t, E_W), h[n_p:].reshape(db, dt, E_W)
            kv2 = (2, NSA_KV_HEADS, HEAD_DIM)
            win_new = hs[:, :, E_WIN:E_WIN + 256].reshape((db, dt) + kv2)
            keep_p, keep_s = min(WINDOW, t), min(WINDOW, w_past + dt)
            win_s = jnp.concatenate([state_nsa_win_kv[:, j, w_past + dt - keep_s:], win_new], axis=1)
            lf_p3 = lf_p[:, MISC_FF:MISC_FF + FOX_HEADS].reshape(nb, t, FOX_HEADS)
            lf_s3 = lf_s[:, MISC_FF:MISC_FF + FOX_HEADS].reshape(db, dt, FOX_HEADS)
            ev_p.append((hp[:, :, E_CMP:E_CMP + 256], hp[:, :, E_SLC:E_SLC + 256],
                         hp[:, t - keep_p:, E_WIN:E_WIN + 256], hp[:, :, E_FK:E_FK + 1024], lf_p3))
            ev_s.append((hs[:, :, E_CMP:E_CMP + 256], hs[:, :, E_SLC:E_SLC + 256],
                         win_s, hs[:, :, E_FK:E_FK + 1024], lf_s3))
        else:
            w = jnp.concatenate([w_in_odd[j], jnp.zeros((d, O_W - w_in_odd.shape[2]), w_in_odd.dtype)],
                                axis=1).astype(BF16)
            h = _proj(x, w, rt, width=O_W, rope_full=ODD_ROPE, rope_lo=ODD_ROPE_LO, tm=tm)
            o_p = _dsa_attn(h, n_seq=nb, t=t, tq=tq, n_keep=min(IDX_TOPK, t // 4))
            bias_s = _dsa_index(h, dik_v, pt_flat, layer=j, row0=n_p, n_seq=db, dt=dt, n_pages=n_pages,
                                pp=n_pages, n_keep=min(IDX_TOPK, (lp + dt) // 4))
            o_s = _dsa_sample(h, bias_s, dkv_v, pt_flat, layer=j, row0=n_p, n_seq=db, dt=dt, n_pages=n_pages,
                              pp=pp_big)
            x = _out_ln([o_p], [o_s], [w_out_odd[j].astype(BF16)], x, g0, b0, alpha=alpha, tm=tm)
            wr = jnp.concatenate([moe_router[j], jnp.zeros((d, LANES - N_EXPERTS), F32)], axis=1)
            wr_hi = wr.astype(BF16)
            wr_lo = (wr - wr_hi.astype(F32)).astype(BF16)
            x = _moe_ln(x, wr_hi, wr_lo, jnp.transpose(moe_gu[j], (0, 2, 1)).astype(BF16),
                        jnp.transpose(moe_down[j], (0, 2, 1)).astype(BF16), g1, b1,
                        alpha=alpha, tmm=MOE_TILE if n_p + n_s >= 8 * MOE_TILE else MOE_BLOCK,
                        fc=moe_down.shape[2] // 4, blk=MOE_BLOCK)
            hp, hs = h[:n_p].reshape(nb, t, O_W), h[n_p:].reshape(db, dt, O_W)
            od_p.append((hp[:, :, O_KV:O_KV + 512], hp[:, :, O_KI:O_KI + IDX_DIM]))
            od_s.append((hs[:, :, O_KV:O_KV + 512], hs[:, :, O_KI:O_KI + IDX_DIM]))

    def stk(lst, idx, tail):
        a = jnp.stack([s[idx] for s in lst], axis=1)
        return a.reshape(a.shape[:3] + tail)

    kv2 = (2, NSA_KV_HEADS, HEAD_DIM)
    fkv = (2, FOX_HEADS, HEAD_DIM)
    dkv = (2, DSA_KV_HEADS, HEAD_DIM)
    return (x[:n_p].reshape(nb, t, d), x[n_p:].reshape(db, dt, d),
            stk(ev_p, 0, kv2), stk(ev_s, 0, kv2), stk(ev_p, 1, kv2), stk(ev_s, 1, kv2),
            stk(ev_p, 2, kv2), stk(ev_s, 2, kv2), stk(ev_p, 3, fkv), stk(ev_s, 3, fkv),
            stk(ev_p, 4, (FOX_HEADS,)), stk(ev_s, 4, (FOX_HEADS,)),
            stk(od_p, 0, dkv), stk(od_s, 0, dkv), stk(od_p, 1, (IDX_DIM,)), stk(od_s, 1, (IDX_DIM,)))
```

```python
import functools
import math

import jax
import jax.numpy as jnp
from jax import lax
from jax.experimental import pallas as pl
from jax.experimental.pallas import tpu as pltpu

F32 = jnp.float32
BF16 = jnp.bfloat16
NEG_INF = float("-inf")

HEAD_DIM = 64
ROT_DIM = HEAD_DIM // 4
ROPE_THETA = 500000.0
NSA_HEADS = 8
NSA_KV_HEADS = 2
NSA_GROUP = NSA_HEADS // NSA_KV_HEADS
CMP_BLOCK = 32
SLC_BLOCK = 64
SLC_TOPN = 16
WINDOW = 512
FOX_HEADS = 8
DSA_HEADS = 16
DSA_KV_HEADS = 4
DSA_GROUP = DSA_HEADS // DSA_KV_HEADS
IDX_HEADS = 8
IDX_DIM = 64
IDX_TOPK = 256
N_EXPERTS = 8
LN_EPS = 1e-5
PAGE_SIZE = 128
LANES = 128

E_NQ, E_FQ, E_FK, E_FV, E_CMP, E_SLC, E_WIN, E_MISC, E_W = 0, 512, 1024, 1536, 2048, 2304, 2560, 2816, 2944
MISC_GATES = 0
MISC_FF = 24
EVEN_ROPE = (0, 1, 2, 3, E_SLC // LANES, E_WIN // LANES)
O_Q, O_KV, O_QI, O_KI, O_W = 0, 1024, 1536, 2048, 2176
ODD_ROPE = tuple(range(0, 10)) + tuple(range(12, 16))
ODD_ROPE_LO = (O_KI // LANES,)

SEL_W = 64
KEY_CHUNK = 1024
SEQS_PER_STEP = 2
MOE_BLOCK = 256
MOE_TILE = 896
VMEM_LIMIT = 48 * 1024 * 1024


def _dot(a, b):
    return jnp.dot(a, b, preferred_element_type=F32)


def _dot_nt(a, b):
    return lax.dot_general(a, b, (((1,), (1,)), ((), ())), preferred_element_type=F32)


def _iota(shape, dim):
    return lax.broadcasted_iota(jnp.int32, shape, dim)


def _rope128(v, c, s1, s2):
    return v * c + pltpu.roll(v, 8, 1) * s1 + pltpu.roll(v, LANES - 8, 1) * s2


def _split3(x):
    hi = x.astype(BF16).astype(F32)
    r = x - hi
    mid = r.astype(BF16).astype(F32)
    lo = (r - mid).astype(BF16).astype(F32)
    return hi, mid, lo


def _st_reset(m_ref, l_ref, acc_ref):
    m_ref[...] = jnp.full(m_ref.shape, NEG_INF, F32)
    l_ref[...] = jnp.zeros(l_ref.shape, F32)
    acc_ref[...] = jnp.zeros(acc_ref.shape, F32)


def _st_update(m_ref, l_ref, acc_ref, s, pv):
    m_prev = m_ref[...]
    m_new = jnp.maximum(m_prev, jnp.max(s, axis=-1, keepdims=True))
    m_safe = jnp.where(m_new == NEG_INF, 0.0, m_new)
    alpha = jnp.exp(m_prev - m_safe)
    p = jnp.exp(s - m_safe)
    l_ref[...] = alpha * l_ref[...] + jnp.sum(p, axis=-1, keepdims=True)
    acc_ref[...] = alpha * acc_ref[...] + pv(p.astype(BF16))
    m_ref[...] = m_new


def _st_update_blocks(m_ref, l_ref, acc_ref, blocks):
    m_prev = m_ref[...]
    widest = blocks[0][0]
    for s, _ in blocks[1:]:
        widest = jnp.maximum(widest, s)
    m_new = jnp.maximum(m_prev, jnp.max(widest, axis=-1, keepdims=True))
    m_safe = jnp.where(m_new == NEG_INF, 0.0, m_new)
    alpha = jnp.exp(m_prev - m_safe)
    acc = alpha * acc_ref[...]
    total = None
    for s, pv in blocks:
        p = jnp.exp(s - m_safe)
        total = p if total is None else total + p
        acc = acc + pv(p.astype(BF16))
    l_ref[...] = alpha * l_ref[...] + jnp.sum(total, axis=-1, keepdims=True)
    acc_ref[...] = acc
    m_ref[...] = m_new


def _st_finish(l_ref, acc_ref):
    return acc_ref[...] / jnp.maximum(l_ref[...], 1e-30)


def _pad_rows(a, rows):
    if a.shape[0] == rows:
        return a
    return jnp.concatenate([a, jnp.zeros((rows - a.shape[0], a.shape[1]), a.dtype)], axis=0)


def _tile_rows(a, times):
    return jnp.concatenate([a] * times, axis=0)


def _layer_norm(z, g, b):
    mu = jnp.mean(z, axis=-1, keepdims=True)
    d = z - mu
    var = jnp.mean(d * d, axis=-1, keepdims=True)
    return d * lax.rsqrt(var + LN_EPS) * g + b


def _key_chunks(total, ch):
    return [(r0, min(ch, total - r0)) for r0 in range(0, total, ch)]


def _proj_kernel(x_ref, w_ref, rt_ref, o_ref, *, width, rope_full, rope_lo, cw):
    xb = x_ref[...].astype(BF16)
    c, s1, s2 = rt_ref[:, 0:128], rt_ref[:, 128:256], rt_ref[:, 256:384]
    lo = _iota((1, LANES), 1) < HEAD_DIM
    for c0 in range(0, width, cw):
        c1 = min(c0 + cw, width)
        acc = _dot(xb, w_ref[:, c0:c1])
        for blk in range(c0 // LANES, c1 // LANES):
            v = acc[:, blk * LANES - c0:(blk + 1) * LANES - c0]
            if blk in rope_full:
                v = _rope128(v, c, s1, s2)
            elif blk in rope_lo:
                v = _rope128(v, jnp.where(lo, c, 1.0), jnp.where(lo, s1, 0.0), jnp.where(lo, s2, 0.0))
            o_ref[:, blk * LANES:(blk + 1) * LANES] = v


def _proj(x, w, rt, *, width, rope_full, rope_lo, tm):
    n, d = x.shape
    return pl.pallas_call(
        functools.partial(_proj_kernel, width=width, rope_full=rope_full, rope_lo=rope_lo, cw=512),
        grid=(n // tm,),
        in_specs=[pl.BlockSpec((tm, d), lambda i: (i, 0)),
                  pl.BlockSpec((d, width), lambda i: (0, 0)),
                  pl.BlockSpec((tm, 3 * LANES), lambda i: (i, 0))],
        out_specs=pl.BlockSpec((tm, width), lambda i: (i, 0)),
        out_shape=jax.ShapeDtypeStruct((n, width), F32),
        compiler_params=pltpu.CompilerParams(dimension_semantics=("arbitrary",), vmem_limit_bytes=VMEM_LIMIT),
        name="proj",
    )(x, w, rt)


def _out_ln_kernel(*refs, n_parts, n_prompt_tiles, alpha):
    op = refs[0:n_parts]
    os_ = refs[n_parts:2 * n_parts]
    ws = refs[2 * n_parts:3 * n_parts]
    x_ref, g_ref, b_ref, o_ref = refs[3 * n_parts:]
    i = pl.program_id(0)

    def compute(parts):
        y = None
        for o, w in zip(parts, ws):
            t = _dot(o[...].astype(BF16), w[...])
            y = t if y is None else y + t
        o_ref[...] = _layer_norm(alpha * x_ref[...] + y, g_ref[...], b_ref[...])

    @pl.when(i < n_prompt_tiles)
    def _():
        compute(op)

    @pl.when(i >= n_prompt_tiles)
    def _():
        compute(os_)


def _out_ln(o_prompt, o_sample, w_parts, x, g, b, *, alpha, tm):
    n, d = x.shape
    npt = o_prompt[0].shape[0] // tm
    nst = o_sample[0].shape[0] // tm
    k = len(w_parts)
    in_specs = []
    for o in o_prompt:
        in_specs.append(pl.BlockSpec((tm, o.shape[1]), lambda i: (jnp.minimum(i, npt - 1), 0)))
    for o in o_sample:
        in_specs.append(pl.BlockSpec((tm, o.shape[1]), lambda i: (jnp.maximum(i - npt, 0), 0)))
    for w in w_parts:
        in_specs.append(pl.BlockSpec(w.shape, lambda i: (0, 0)))
    in_specs += [pl.BlockSpec((tm, d), lambda i: (i, 0)),
                 pl.BlockSpec((1, d), lambda i: (0, 0)),
                 pl.BlockSpec((1, d), lambda i: (0, 0))]
    assert npt + nst == n // tm
    return pl.pallas_call(
        functools.partial(_out_ln_kernel, n_parts=k, n_prompt_tiles=npt, alpha=alpha),
        grid=(n // tm,),
        in_specs=in_specs,
        out_specs=pl.BlockSpec((tm, d), lambda i: (i, 0)),
        out_shape=jax.ShapeDtypeStruct((n, d), F32),
        compiler_params=pltpu.CompilerParams(dimension_semantics=("arbitrary",), vmem_limit_bytes=VMEM_LIMIT),
        name="out_ln",
    )(*o_prompt, *o_sample, *w_parts, x, g, b)


def _ffn_ln_kernel(x_ref, wa_ref, wg_ref, wd_ref, g_ref, b_ref, o_ref, acc_ref, *, alpha, n_chunks):
    c = pl.program_id(1)

    @pl.when(c == 0)
    def _():
        acc_ref[...] = jnp.zeros_like(acc_ref)

    xb = x_ref[...].astype(BF16)
    a = _dot(xb, wa_ref[...])
    gg = _dot(xb, wg_ref[...])
    hmid = (jax.nn.silu(a) * gg).astype(BF16)
    acc_ref[...] += _dot(hmid, wd_ref[...])

    @pl.when(c == n_chunks - 1)
    def _():
        o_ref[...] = _layer_norm(alpha * x_ref[...] + acc_ref[...], g_ref[...], b_ref[...])


def _ffn_ln(x, w_gu, w_down, g, b, *, alpha, tm, fc):
    n, d = x.shape
    dff = w_down.shape[0]
    nch = dff // fc
    return pl.pallas_call(
        functools.partial(_ffn_ln_kernel, alpha=alpha, n_chunks=nch),
        grid=(n // tm, nch),
        in_specs=[pl.BlockSpec((tm, d), lambda i, c: (i, 0)),
                  pl.BlockSpec((d, fc), lambda i, c: (0, c)),
                  pl.BlockSpec((d, fc), lambda i, c: (0, c + nch)),
                  pl.BlockSpec((fc, d), lambda i, c: (c, 0)),
                  pl.BlockSpec((1, d), lambda i, c: (0, 0)),
                  pl.BlockSpec((1, d), lambda i, c: (0, 0))],
        out_specs=pl.BlockSpec((tm, d), lambda i, c: (i, 0)),
        out_shape=jax.ShapeDtypeStruct((n, d), F32),
        scratch_shapes=[pltpu.VMEM((tm, d), F32)],
        compiler_params=pltpu.CompilerParams(dimension_semantics=("arbitrary", "arbitrary"),
                                             vmem_limit_bytes=VMEM_LIMIT),
        name="ffn_ln",
    )(x, w_gu, w_gu, w_down, g, b)


def _moe_ln_kernel(x_ref, wrh_ref, wrl_ref, wat_ref, wgt_ref, wdt_ref, g_ref, b_ref, o_ref,
                   xt_s, acct_s, route_s, routet_s, xct_s, yct_s, *, alpha, n_chunks, n_valid, blk):
    i = pl.program_id(0)
    e = pl.program_id(1)
    c = pl.program_id(2)
    tmm = x_ref.shape[0]
    lane = _iota((1, LANES), 1).astype(F32)
    ef = e.astype(F32)

    @pl.when((e == 0) & (c == 0))
    def _():
        x = x_ref[...]
        xh = x.astype(BF16)
        xl = (x - xh.astype(F32)).astype(BF16)
        logits = _dot(xh, wrh_ref[...]) + _dot(xl, wrh_ref[...]) + _dot(xh, wrl_ref[...])
        lg = jnp.where(lane < N_EXPERTS, logits, NEG_INF)
        m1 = jnp.max(lg, axis=-1, keepdims=True)
        i1 = jnp.min(jnp.where(lg == m1, lane, float(LANES)), axis=-1, keepdims=True)
        lg2 = jnp.where(lane == i1, NEG_INF, lg)
        m2 = jnp.max(lg2, axis=-1, keepdims=True)
        i2 = jnp.min(jnp.where(lg2 == m2, lane, float(LANES)), axis=-1, keepdims=True)
        e2 = jnp.exp(m2 - m1)
        den = 1.0 + e2
        comb = jnp.where(lane == i1, 1.0 / den, 0.0) + jnp.where(lane == i2, e2 / den, 0.0)
        real = (i * tmm + _iota((tmm, 1), 0)) < n_valid
        sel = jnp.where(real & ((lane == i1) | (lane == i2)), 1.0, 0.0)
        before = jnp.where(_iota((tmm, tmm), 0) > _iota((tmm, tmm), 1), 1.0, 0.0).astype(BF16)
        rank = _dot(before, sel.astype(BF16))
        for idx, val in enumerate((comb, sel, rank)):
            route_s[idx] = val
            routet_s[idx] = val.T
        xt_s[...] = x.T.astype(BF16)
        acct_s[...] = jnp.zeros_like(acct_s)

    def pick(ref3, idx):
        return jnp.sum(jnp.where(lane == ef, ref3[idx], 0.0), axis=-1, keepdims=True)

    count = jnp.sum(jnp.where(lane == ef, route_s[1], 0.0))
    n_blocks = (count.astype(jnp.int32) + blk - 1) // blk

    @pl.when(c == 0)
    def _():
        sel_col, rank_col = pick(route_s, 1), pick(route_s, 2)

        def compact(k, carry):
            slot = (k * blk + _iota((1, blk), 1)).astype(F32)
            onehot = jnp.where(rank_col == slot, sel_col, 0.0).astype(BF16)
            xct_s[k] = _dot(xt_s[...], onehot).astype(BF16)
            yct_s[k] = jnp.zeros(yct_s.shape[1:], F32)
            return carry

        lax.fori_loop(0, n_blocks, compact, 0)

    def expert(k, carry):
        xc = xct_s[k]
        a = _dot(wat_ref[...], xc)
        gg = _dot(wgt_ref[...], xc)
        yct_s[k] += _dot(wdt_ref[...], (jax.nn.silu(a) * gg).astype(BF16))
        return carry

    lax.fori_loop(0, n_blocks, expert, 0)

    @pl.when(c == n_chunks - 1)
    def _():
        w_row = routet_s[0, pl.ds(e, 1), :]
        sel_row = routet_s[1, pl.ds(e, 1), :]
        rank_row = routet_s[2, pl.ds(e, 1), :]

        def scatter(k, carry):
            slot = (k * blk + _iota((blk, 1), 0)).astype(F32)
            onehot = jnp.where(rank_row == slot, sel_row, 0.0).astype(BF16)
            y = yct_s[k]
            yh = y.astype(BF16)
            yl = (y - yh.astype(F32)).astype(BF16)
            acct_s[...] += w_row * (_dot(yh, onehot) + _dot(yl, onehot))
            return carry

        lax.fori_loop(0, n_blocks, scatter, 0)

    @pl.when((e == N_EXPERTS - 1) & (c == n_chunks - 1))
    def _():
        o_ref[...] = _layer_norm(alpha * x_ref[...] + acct_s[...].T, g_ref[...], b_ref[...])


def _moe_ln(x, wr_hi, wr_lo, w_gu_t, w_down_t, g, b, *, alpha, tmm, fc, blk):
    n, d = x.shape
    dff = w_down_t.shape[2]
    nch = dff // fc
    n_tiles = -(-n // tmm)
    n_pad = n_tiles * tmm
    xp = jnp.pad(x, ((0, n_pad - n), (0, 0)))
    max_blocks = -(-tmm // blk)
    out = pl.pallas_call(
        functools.partial(_moe_ln_kernel, alpha=alpha, n_chunks=nch, n_valid=n, blk=blk),
        grid=(n_tiles, N_EXPERTS, nch),
        in_specs=[pl.BlockSpec((tmm, d), lambda i, e, c: (i, 0)),
                  pl.BlockSpec((d, LANES), lambda i, e, c: (0, 0)),
                  pl.BlockSpec((d, LANES), lambda i, e, c: (0, 0)),
                  pl.BlockSpec((None, fc, d), lambda i, e, c: (e, c, 0)),
                  pl.BlockSpec((None, fc, d), lambda i, e, c: (e, c + nch, 0)),
                  pl.BlockSpec((None, d, fc), lambda i, e, c: (e, 0, c)),
                  pl.BlockSpec((1, d), lambda i, e, c: (0, 0)),
                  pl.BlockSpec((1, d), lambda i, e, c: (0, 0))],
        out_specs=pl.BlockSpec((tmm, d), lambda i, e, c: (i, 0)),
        out_shape=jax.ShapeDtypeStruct((n_pad, d), F32),
        scratch_shapes=[pltpu.VMEM((d, tmm), BF16), pltpu.VMEM((d, tmm), F32),
                        pltpu.VMEM((3, tmm, LANES), F32), pltpu.VMEM((3, LANES, tmm), F32),
                        pltpu.VMEM((max_blocks, d, blk), BF16), pltpu.VMEM((max_blocks, d, blk), F32)],
        compiler_params=pltpu.CompilerParams(dimension_semantics=("arbitrary", "arbitrary", "arbitrary"),
                                             vmem_limit_bytes=VMEM_LIMIT),
        name="moe_ln",
    )(xp, wr_hi, wr_lo, w_gu_t, w_gu_t, w_down_t, g, b)
    return out[:n]


def _page_specs(block, n_pages, pp, layer):
    tail = (0,) * (len(block) - 2)
    return [pl.BlockSpec(block, lambda s, p, pt, k=k: (pt[s * n_pages + p * pp + k], layer) + tail)
            for k in range(pp)]


def _log_sigmoid(z):
    return jnp.minimum(z, 0.0) - jnp.log1p(jnp.exp(-jnp.abs(z)))


def _forget_lanes(x):
    lane = _iota((1, LANES), 1)
    return jnp.where((lane >= MISC_FF) & (lane < MISC_FF + FOX_HEADS), x, 0.0)


def _cum_small(lf):
    rows = lf.shape[0]
    ri = _iota((rows, LANES), 0)
    cum = jnp.zeros((rows, LANES), F32)
    for t in range(rows):
        cum = cum + jnp.where(ri >= t, lf[t:t + 1, :], 0.0)
    return cum


def _cmp_summaries(kc_ref, vc_ref, pe_ref, phi_ref, crt_ref, nc):
    acc_k = jnp.zeros((nc, LANES), F32)
    acc_v = jnp.zeros((nc, LANES), F32)
    for l in range(CMP_BLOCK):
        rk = kc_ref[pl.ds(l, nc, stride=CMP_BLOCK), :] + pe_ref[0, l:l + 1, :]
        acc_k = acc_k + _dot(rk.astype(BF16), phi_ref[0, l])
        rv = vc_ref[pl.ds(l, nc, stride=CMP_BLOCK), :] + pe_ref[1, l:l + 1, :]
        acc_v = acc_v + _dot(rv.astype(BF16), phi_ref[1, l])
    ck = _rope128(acc_k, crt_ref[:, 0:128], crt_ref[:, 128:256], crt_ref[:, 256:384])
    return ck, acc_v


def _nsa_cmp_select(q4, ck_e, ck_o, cv_e, cv_o, qpos, *, tq, nc, ns, topn):
    qpos4 = _tile_rows(qpos, NSA_GROUP)
    blk = _iota((1, SEL_W), 1)
    n_even, n_odd = (nc + 1) // 2, nc // 2
    ok_e = ((2 * blk + 1) * CMP_BLOCK - 1 <= qpos4) & (blk < n_even)
    ok_o = ((2 * blk + 2) * CMP_BLOCK - 1 <= qpos4) & (blk < n_odd)
    s_e = jnp.where(ok_e, _dot_nt(q4, ck_e), NEG_INF)
    s_o = jnp.where(ok_o, _dot_nt(q4, ck_o), NEG_INF)
    m = jnp.maximum(jnp.max(s_e, axis=-1, keepdims=True), jnp.max(s_o, axis=-1, keepdims=True))
    m = jnp.where(m == NEG_INF, 0.0, m)
    p_e = jnp.exp(s_e - m)
    p_o = jnp.exp(s_o - m)
    den = jnp.sum(p_e, axis=-1, keepdims=True) + jnp.sum(p_o, axis=-1, keepdims=True)
    inv = 1.0 / jnp.maximum(den, 1e-30)
    p_e = p_e * inv
    p_o = p_o * inv
    o_cmp = _dot(p_e.astype(BF16), cv_e) + _dot(p_o.astype(BF16), cv_o)
    pp = p_e + p_o
    imp = pp[0:tq]
    for r in range(1, NSA_GROUP):
        imp = imp + pp[r * tq:(r + 1) * tq]
    valid = blk * SLC_BLOCK <= qpos
    forced = (blk == 0) | (blk == (qpos >> 6))
    score = jnp.where(forced, jnp.inf, jnp.where(valid, imp, NEG_INF))
    score = jnp.where(blk < ns, score, NEG_INF)
    rank = jnp.zeros((tq, SEL_W), F32)
    for b2 in range(ns):
        col = score[:, b2:b2 + 1]
        beats = (col > score) | ((col == score) & (blk > b2))
        rank = rank + jnp.where(beats, 1.0, 0.0)
    sel = jnp.where(rank < topn, 1.0, 0.0)
    return o_cmp, sel


def _block_expand(pos0, n):
    return jnp.where(_iota((SEL_W, n), 0) == ((pos0 + _iota((SEL_W, n), 1)) >> 6), 1.0, 0.0).astype(BF16)


def _nsa_q4(hq_ref, g):
    q4 = jnp.concatenate([hq_ref[:, (NSA_GROUP * g + r) * 64:(NSA_GROUP * g + r + 1) * 64]
                          for r in range(NSA_GROUP)], axis=0)
    return (q4 * 0.125).astype(BF16)


def _nsa_write(o_ref, gates, g, tq, o_cmp, o_slc, o_win):
    for r in range(NSA_GROUP):
        hh = NSA_GROUP * g + r
        rs = slice(r * tq, (r + 1) * tq)
        c0 = MISC_GATES + 3 * hh
        o_ref[:, hh * 64:(hh + 1) * 64] = (gates[:, c0:c0 + 1] * o_cmp[rs] + gates[:, c0 + 1:c0 + 2] * o_slc[rs]
                                           + gates[:, c0 + 2:c0 + 3] * o_win[rs])


def _kth_largest_key(key_ref, n_keep, count, row_shape, bits):
    kk = float(n_keep)

    def descend(step, thr):
        shift = 32 - bits * (step + 1)
        digit = jnp.zeros(row_shape, jnp.int32)
        for d in range(1, 1 << bits):
            cand = thr + (jnp.int32(d) << shift)
            digit = digit + jnp.where(count(key_ref[...] >= cand) >= kk, 1, 0)
        return thr + (digit << shift)

    return lax.fori_loop(0, 32 // bits, descend, jnp.full(row_shape, -2147483648, jnp.int32))


def _topk_bias_store(out_ref, key_ref, n_keep, idx, count, row_shape, bits=1):
    kk = float(n_keep)
    thr = _kth_largest_key(key_ref, n_keep, count, row_shape, bits)
    has_ties = jnp.max(count(key_ref[...] >= thr)) > kk

    @pl.when(jnp.logical_not(has_ties))
    def _():
        out_ref[...] = jnp.where(key_ref[...] >= thr, 0.0, NEG_INF)

    @pl.when(has_ties)
    def _():
        keys = key_ref[...]
        gt = keys > thr
        eq = keys == thr
        need = kk - count(gt)

        def widen(b, bound):
            cand = bound + (jnp.int32(1) << (15 - b))
            return jnp.where(count(eq & (idx < cand)) <= need, cand, bound)

        bound = lax.fori_loop(0, 16, widen, jnp.zeros(row_shape, jnp.int32))
        out_ref[...] = jnp.where(gt | (eq & (idx < bound)), 0.0, NEG_INF)


def _sortable_key(score):
    bits = pltpu.bitcast(score, jnp.int32)
    return bits ^ ((bits >> 31) & jnp.int32(0x7FFFFFFF))


def _cum_tri(lf, carry):
    rows = lf.shape[0]
    tri = jnp.where(_iota((rows, rows), 0) >= _iota((rows, rows), 1), 1.0, 0.0).astype(BF16)
    hi, mid, lo = _split3(lf)
    return _dot(tri, hi.astype(BF16)) + _dot(tri, mid.astype(BF16)) + _dot(tri, lo.astype(BF16)) + carry


def _fox_aug_cols(cum3, hh):
    chi, cmid, clo = [part[:, MISC_FF + hh:MISC_FF + hh + 1] for part in cum3]
    l64 = _iota((1, HEAD_DIM), 1)
    qx = jnp.where(l64 == 0, chi, jnp.where(l64 == 1, cmid, jnp.where(l64 == 2, clo, jnp.where(l64 < 6, 1.0, 0.0))))
    kx = jnp.where(l64 < 3, 1.0,
                   jnp.where(l64 == 3, -chi, jnp.where(l64 == 4, -cmid, jnp.where(l64 == 5, -clo, 0.0))))
    return qx, kx


def _fox_prep_kernel(fq_ref, fk_ref, misc_ref, fb_ref, qa_ref, ka_ref, lf_ref, carry_ref):
    j = pl.program_id(1)

    @pl.when(j == 0)
    def _():
        carry_ref[...] = jnp.zeros_like(carry_ref)

    lf = _forget_lanes(_log_sigmoid(misc_ref[...] + fb_ref[...]))
    lf_ref[...] = lf
    cum = _cum_tri(lf, carry_ref[...])
    rows = lf.shape[0]
    carry_ref[...] = cum[rows - 1:rows, :]
    cum3 = _split3(cum)
    for hh in range(FOX_HEADS):
        qx, kx = _fox_aug_cols(cum3, hh)
        qa_ref[:, hh * 128:hh * 128 + 64] = (fq_ref[:, hh * 64:(hh + 1) * 64] * 0.125).astype(BF16)
        qa_ref[:, hh * 128 + 64:(hh + 1) * 128] = qx.astype(BF16)
        ka_ref[:, hh * 128:hh * 128 + 64] = fk_ref[:, hh * 64:(hh + 1) * 64].astype(BF16)
        ka_ref[:, hh * 128 + 64:(hh + 1) * 128] = kx.astype(BF16)


def _fox_prep(h, fb, *, n_seq, t, rows):
    nr = t // rows
    n = n_seq * t
    return pl.pallas_call(
        _fox_prep_kernel,
        grid=(n_seq, nr),
        in_specs=[pl.BlockSpec((rows, 512), lambda s, j: (s * nr + j, E_FQ // 512)),
                  pl.BlockSpec((rows, 512), lambda s, j: (s * nr + j, E_FK // 512)),
                  pl.BlockSpec((rows, LANES), lambda s, j: (s * nr + j, E_MISC // LANES)),
                  pl.BlockSpec((1, LANES), lambda s, j: (0, 0))],
        out_specs=[pl.BlockSpec((rows, FOX_HEADS * 128), lambda s, j: (s * nr + j, 0)),
                   pl.BlockSpec((rows, FOX_HEADS * 128), lambda s, j: (s * nr + j, 0)),
                   pl.BlockSpec((rows, LANES), lambda s, j: (s * nr + j, 0))],
        out_shape=[jax.ShapeDtypeStruct((n, FOX_HEADS * 128), BF16),
                   jax.ShapeDtypeStruct((n, FOX_HEADS * 128), BF16),
                   jax.ShapeDtypeStruct((n, LANES), F32)],
        scratch_shapes=[pltpu.VMEM((1, LANES), F32)],
        compiler_params=pltpu.CompilerParams(dimension_semantics=("arbitrary", "arbitrary")),
        name="fox_prep",
    )(h, h, h, fb)


def _fox_attn_kernel(qa_ref, ka_ref, v_ref, o_ref, m_s, l_s, acc_s, *, tq, t, hps):
    i = pl.program_id(2)
    qpos = i * tq + _iota((tq, 1), 0)
    q_end = i * tq + tq - 1
    for hh in range(hps):
        _st_reset(m_s.at[hh], l_s.at[hh], acc_s.at[hh])
    for r0, n in _key_chunks(t, KEY_CHUNK):
        @pl.when(r0 <= q_end)
        def _():
            bias = jnp.where((r0 + _iota((1, n), 1)) <= qpos, 0.0, NEG_INF)
            for hh in range(hps):
                k = ka_ref[r0:r0 + n, hh * 128:(hh + 1) * 128]
                v = v_ref[r0:r0 + n, hh * 64:(hh + 1) * 64].astype(BF16)
                s = _dot_nt(qa_ref[:, hh * 128:(hh + 1) * 128], k) + bias
                _st_update(m_s.at[hh], l_s.at[hh], acc_s.at[hh], s, lambda pb, v=v: _dot(pb, v))
    for hh in range(hps):
        o_ref[:, hh * 64:(hh + 1) * 64] = _st_finish(l_s.at[hh], acc_s.at[hh])


def _fox_attn(qa, ka, h, *, n_seq, t, tq, hps):
    nq = t // tq
    n = n_seq * t
    return pl.pallas_call(
        functools.partial(_fox_attn_kernel, tq=tq, t=t, hps=hps),
        grid=(n_seq, FOX_HEADS // hps, nq),
        in_specs=[pl.BlockSpec((tq, hps * 128), lambda s, p, i: (s * nq + i, p)),
                  pl.BlockSpec((t, hps * 128), lambda s, p, i: (s, p)),
                  pl.BlockSpec((t, hps * 64), lambda s, p, i: (s, E_FV // (hps * 64) + p))],
        out_specs=pl.BlockSpec((tq, hps * 64), lambda s, p, i: (s * nq + i, p)),
        out_shape=jax.ShapeDtypeStruct((n, FOX_HEADS * HEAD_DIM), F32),
        scratch_shapes=[pltpu.VMEM((hps, tq, 1), F32), pltpu.VMEM((hps, tq, 1), F32),
                        pltpu.VMEM((hps, tq, HEAD_DIM), F32)],
        compiler_params=pltpu.CompilerParams(dimension_semantics=("arbitrary", "arbitrary", "arbitrary"),
                                             vmem_limit_bytes=VMEM_LIMIT),
        name="fox_attn",
    )(qa, ka, h)


def _nsa_kernel(hq_ref, misc_ref, kc_ref, vc_ref, slc_ref, win_ref, pe_ref, phi_ref, crt_ref, o_ref,
                ck_s, cv_s, m_s, l_s, acc_s, *, tq, t, nc, ns, topn):
    i = pl.program_id(1)

    @pl.when(i == 0)
    def _():
        ck, cv = _cmp_summaries(kc_ref, vc_ref, pe_ref, phi_ref, crt_ref, nc)
        ck_s[...] = jnp.zeros_like(ck_s)
        cv_s[...] = jnp.zeros_like(cv_s)
        ck_s[0:nc, :] = ck
        cv_s[0:nc, :] = cv

    qs = i * tq
    q_end = qs + tq - 1
    qpos = qs + _iota((tq, 1), 0)
    gates = jax.nn.sigmoid(misc_ref[...])
    n_win = min(WINDOW + tq, t)
    w0 = pl.multiple_of(jnp.maximum(qs + tq - n_win, 0), 8)

    groups = range(NSA_KV_HEADS)
    kcols = [slice(g * 64, (g + 1) * 64) for g in groups]
    vcols = [slice(128 + g * 64, 128 + (g + 1) * 64) for g in groups]
    qh = [(hq_ref[:, hh * 64:(hh + 1) * 64] * 0.125).astype(BF16) for hh in range(NSA_HEADS)]
    o_cmps, sels = [], []
    for g in groups:
        q4 = jnp.concatenate(qh[NSA_GROUP * g:NSA_GROUP * (g + 1)], axis=0)
        ck_e = ck_s[pl.ds(0, SEL_W, stride=2), :][:, kcols[g]].astype(BF16)
        ck_o = ck_s[pl.ds(1, SEL_W, stride=2), :][:, kcols[g]].astype(BF16)
        cv_e = cv_s[pl.ds(0, SEL_W, stride=2), :][:, kcols[g]].astype(BF16)
        cv_o = cv_s[pl.ds(1, SEL_W, stride=2), :][:, kcols[g]].astype(BF16)
        o_cmp, sel = _nsa_cmp_select(q4, ck_e, ck_o, cv_e, cv_o, qpos, tq=tq, nc=nc, ns=ns, topn=topn)
        o_cmps.append(o_cmp)
        sels.append(sel.astype(BF16))
    for hh in range(NSA_HEADS):
        _st_reset(m_s.at[hh], l_s.at[hh], acc_s.at[hh])

    for r0, n in _key_chunks(t, KEY_CHUNK):
        @pl.when(r0 <= q_end)
        def _():
            expand = _block_expand(r0, n)
            causal = (r0 + _iota((1, n), 1)) <= qpos
            for g in groups:
                k = slc_ref[r0:r0 + n, kcols[g]].astype(BF16)
                v = slc_ref[r0:r0 + n, vcols[g]].astype(BF16)
                bias = jnp.where((_dot(sels[g], expand) > 0.5) & causal, 0.0, NEG_INF)
                for r in range(NSA_GROUP):
                    hh = NSA_GROUP * g + r
                    _st_update(m_s.at[hh], l_s.at[hh], acc_s.at[hh], _dot_nt(qh[hh], k) + bias,
                               lambda pb, v=v: _dot(pb, v))

    kpos = w0 + _iota((1, n_win), 1)
    wbias = jnp.where((kpos <= qpos) & (kpos > qpos - WINDOW), 0.0, NEG_INF)
    for g in groups:
        k = win_ref[pl.ds(w0, n_win), kcols[g]].astype(BF16)
        v = win_ref[pl.ds(w0, n_win), vcols[g]].astype(BF16)
        for r in range(NSA_GROUP):
            hh = NSA_GROUP * g + r
            o_slc = _st_finish(l_s.at[hh], acc_s.at[hh])
            s = _dot_nt(qh[hh], k) + wbias
            p = jnp.exp(s - jnp.max(s, axis=-1, keepdims=True))
            o_win = _dot(p.astype(BF16), v) / jnp.maximum(jnp.sum(p, axis=-1, keepdims=True), 1e-30)
            c0 = MISC_GATES + 3 * hh
            o_ref[:, hh * 64:(hh + 1) * 64] = (gates[:, c0:c0 + 1] * o_cmps[g][r * tq:(r + 1) * tq]
                                               + gates[:, c0 + 1:c0 + 2] * o_slc + gates[:, c0 + 2:c0 + 3] * o_win)


def _nsa_attn(h, pe, phi, crt, *, n_seq, t, tq, topn):
    nq = t // tq
    n = n_seq * t
    nc = t // CMP_BLOCK
    ns = -(-t // SLC_BLOCK)
    assert ns <= SEL_W and nc <= 2 * SEL_W
    rows4 = NSA_GROUP * tq

    def qblk(col):
        return lambda s, i: (s * nq + i, col)

    def kblk(col):
        return lambda s, i: (s, col)

    return pl.pallas_call(
        functools.partial(_nsa_kernel, tq=tq, t=t, nc=nc, ns=ns, topn=min(topn, ns)),
        grid=(n_seq, nq),
        in_specs=[pl.BlockSpec((tq, 512), qblk(E_NQ // 512)),
                  pl.BlockSpec((tq, LANES), qblk(E_MISC // LANES)),
                  pl.BlockSpec((t, LANES), kblk(E_CMP // LANES)),
                  pl.BlockSpec((t, LANES), kblk(E_CMP // LANES + 1)),
                  pl.BlockSpec((t, 256), kblk(E_SLC // 256)),
                  pl.BlockSpec((t, 256), kblk(E_WIN // 256)),
                  pl.BlockSpec(pe.shape, lambda s, i: (0, 0, 0)),
                  pl.BlockSpec(phi.shape, lambda s, i: (0, 0, 0, 0)),
                  pl.BlockSpec(crt.shape, lambda s, i: (0, 0))],
        out_specs=pl.BlockSpec((tq, NSA_HEADS * HEAD_DIM), lambda s, i: (s * nq + i, 0)),
        out_shape=jax.ShapeDtypeStruct((n, NSA_HEADS * HEAD_DIM), F32),
        scratch_shapes=[pltpu.VMEM((2 * SEL_W, LANES), F32), pltpu.VMEM((2 * SEL_W, LANES), F32),
                        pltpu.VMEM((NSA_HEADS, tq, 1), F32), pltpu.VMEM((NSA_HEADS, tq, 1), F32),
                        pltpu.VMEM((NSA_HEADS, tq, HEAD_DIM), F32)],
        compiler_params=pltpu.CompilerParams(dimension_semantics=("arbitrary", "arbitrary"),
                                             vmem_limit_bytes=VMEM_LIMIT),
        name="nsa_attn",
    )(h, h, h, h, h, h, pe, phi, crt)


def _dsa_kernel(q_ref, qi_ref, kw_ref, kv_ref, ki_ref, o_ref, sc_s, key_s, m_s, l_s, acc_s, *, tq, t, n_keep):
    i = pl.program_id(1)
    qpos = i * tq + _iota((tq, 1), 0)
    q_end = i * tq + tq - 1
    chunks = _key_chunks(t, KEY_CHUNK)

    sc_s[...] = jnp.full(sc_s.shape, NEG_INF, F32)
    wi = kw_ref[:, IDX_DIM:IDX_DIM + IDX_HEADS] * (IDX_HEADS ** -0.5)
    qi = [(qi_ref[:, hh * IDX_DIM:(hh + 1) * IDX_DIM] * (IDX_DIM ** -0.5)).astype(BF16) for hh in range(IDX_HEADS)]
    for r0, n in chunks:
        @pl.when(r0 <= q_end)
        def _():
            ki = ki_ref[r0:r0 + n, 0:IDX_DIM].astype(BF16)
            sc = jnp.zeros((tq, n), F32)
            for hh in range(IDX_HEADS):
                sc = sc + wi[:, hh:hh + 1] * jnp.maximum(_dot_nt(qi[hh], ki), 0.0)
            ok = (r0 + _iota((1, n), 1)) <= qpos
            sc_s[:, r0:r0 + n] = jnp.where(ok, sc + 0.0, NEG_INF)

    key_s[...] = _sortable_key(sc_s[...])
    _topk_bias_store(sc_s, key_s, n_keep, _iota((1, t), 1),
                     lambda mask: jnp.sum(jnp.where(mask, 1.0, 0.0), axis=-1, keepdims=True), (tq, 1))

    qs = [(q_ref[:, hh * 64:(hh + 1) * 64] * 0.125).astype(BF16) for hh in range(DSA_HEADS)]
    for hh in range(DSA_HEADS):
        _st_reset(m_s.at[hh], l_s.at[hh], acc_s.at[hh])
    for r0, n in chunks:
        @pl.when(r0 <= q_end)
        def _():
            ok = (r0 + _iota((1, n), 1)) <= qpos
            bias = jnp.where(ok, sc_s[:, r0:r0 + n], NEG_INF)
            for kh in range(DSA_KV_HEADS):
                k = kv_ref[r0:r0 + n, kh * 64:(kh + 1) * 64].astype(BF16)
                v = kv_ref[r0:r0 + n, 256 + kh * 64:256 + (kh + 1) * 64].astype(BF16)
                for r in range(DSA_GROUP):
                    hh = DSA_GROUP * kh + r
                    _st_update(m_s.at[hh], l_s.at[hh], acc_s.at[hh], _dot_nt(qs[hh], k) + bias,
                               lambda pb, v=v: _dot(pb, v))
    for hh in range(DSA_HEADS):
        o_ref[:, hh * 64:(hh + 1) * 64] = _st_finish(l_s.at[hh], acc_s.at[hh])


def _dsa_attn(h, *, n_seq, t, tq, n_keep):
    nq = t // tq
    n = n_seq * t
    rows4 = DSA_GROUP * tq
    assert t % LANES == 0 and t <= 1 << 16

    def qblk(w, col):
        return pl.BlockSpec((tq, w), lambda s, i: (s * nq + i, col))

    return pl.pallas_call(
        functools.partial(_dsa_kernel, tq=tq, t=t, n_keep=n_keep),
        grid=(n_seq, nq),
        in_specs=[qblk(1024, O_Q // 1024), qblk(512, O_QI // 512), qblk(LANES, O_KI // LANES),
                  pl.BlockSpec((t, 512), lambda s, i: (s, O_KV // 512)),
                  pl.BlockSpec((t, LANES), lambda s, i: (s, O_KI // LANES))],
        out_specs=pl.BlockSpec((tq, DSA_HEADS * HEAD_DIM), lambda s, i: (s * nq + i, 0)),
        out_shape=jax.ShapeDtypeStruct((n, DSA_HEADS * HEAD_DIM), F32),
        scratch_shapes=[pltpu.VMEM((tq, t), F32), pltpu.VMEM((tq, t), jnp.int32),
                        pltpu.VMEM((DSA_HEADS, tq, 1), F32), pltpu.VMEM((DSA_HEADS, tq, 1), F32),
                        pltpu.VMEM((DSA_HEADS, tq, HEAD_DIM), F32)],
        compiler_params=pltpu.CompilerParams(dimension_semantics=("arbitrary", "arbitrary"),
                                             vmem_limit_bytes=VMEM_LIMIT),
        name="dsa_attn",
    )(h, h, h, h, h)


def _cmp_sample_kernel(pt_ref, *refs, pp, n_steps, nc):
    slabs = refs[:pp]
    pe_ref, phi_ref, crt_ref, o_ref, k_s, v_s, c_s = refs[pp:]
    p = pl.program_id(1)
    for k in range(pp):
        row0 = pl.multiple_of((p * pp + k) * PAGE_SIZE, PAGE_SIZE)
        for g in range(NSA_KV_HEADS):
            k_s[pl.ds(row0, PAGE_SIZE), g * 64:(g + 1) * 64] = slabs[k][0, g].T
            v_s[pl.ds(row0, PAGE_SIZE), g * 64:(g + 1) * 64] = slabs[k][1, g].T

    @pl.when(p == n_steps - 1)
    def _():
        ck, cv = _cmp_summaries(k_s, v_s, pe_ref, phi_ref, crt_ref, nc)
        for idx, val in ((0, ck), (2, cv)):
            c_s[...] = jnp.zeros_like(c_s)
            c_s[0:nc, :] = val
            o_ref[idx] = c_s[pl.ds(0, SEL_W, stride=2), :]
            o_ref[idx + 1] = c_s[pl.ds(1, SEL_W, stride=2), :]


def _cmp_sample(cmp_v, pt_flat, pe, phi, crt, *, layer, n_seq, n_pages, pp, nc):
    lp = n_pages * PAGE_SIZE
    n_steps = n_pages // pp
    in_specs = _page_specs((None, None, 2, NSA_KV_HEADS, HEAD_DIM, PAGE_SIZE), n_pages, pp, layer)
    in_specs += [pl.BlockSpec(pe.shape, lambda s, p, pt: (0, 0, 0)),
                 pl.BlockSpec(phi.shape, lambda s, p, pt: (0, 0, 0, 0)),
                 pl.BlockSpec(crt.shape, lambda s, p, pt: (0, 0))]
    return pl.pallas_call(
        functools.partial(_cmp_sample_kernel, pp=pp, n_steps=n_steps, nc=nc),
        grid_spec=pltpu.PrefetchScalarGridSpec(
            num_scalar_prefetch=1, grid=(n_seq, n_steps), in_specs=in_specs,
            out_specs=pl.BlockSpec((None, 4, SEL_W, LANES), lambda s, p, pt: (s, 0, 0, 0)),
            scratch_shapes=[pltpu.VMEM((lp, LANES), F32), pltpu.VMEM((lp, LANES), F32),
                            pltpu.VMEM((2 * SEL_W, LANES), F32)]),
        out_shape=jax.ShapeDtypeStruct((n_seq, 4, SEL_W, LANES), F32),
        compiler_params=pltpu.CompilerParams(dimension_semantics=("arbitrary", "arbitrary"),
                                             vmem_limit_bytes=VMEM_LIMIT),
        name="cmp_sample",
    )(pt_flat, *([cmp_v] * pp), pe, phi, crt)


def _nsa_sample_kernel(pt_ref, *refs, pp, n_steps, dt, l_past, w_past, nc, ns, topn):
    hq_ref, misc_ref, slc_ref, win_ref, ckv_ref, winp_ref = refs[:6]
    slabs = refs[6:6 + pp]
    o_ref, sel_s, ocmp_s, m_s, l_s, acc_s = refs[6 + pp:]
    p = pl.program_id(1)
    qpos = l_past + _iota((dt, 1), 0)
    new_pos = l_past + _iota((1, LANES), 1)
    new_ok = (new_pos <= qpos) & (new_pos < l_past + dt)

    @pl.when(p == 0)
    def _():
        for g in range(NSA_KV_HEADS):
            kcol = slice(g * 64, (g + 1) * 64)
            o_cmp, sel = _nsa_cmp_select(
                _nsa_q4(hq_ref, g), ckv_ref[0][:, kcol].astype(BF16), ckv_ref[1][:, kcol].astype(BF16),
                ckv_ref[2][:, kcol].astype(BF16), ckv_ref[3][:, kcol].astype(BF16), qpos,
                tq=dt, nc=nc, ns=ns, topn=topn)
            ocmp_s[g] = o_cmp
            sel_s[g] = sel
            _st_reset(m_s.at[g], l_s.at[g], acc_s.at[g])

    expand = _block_expand(p * pp * PAGE_SIZE, pp * PAGE_SIZE)
    for g in range(NSA_KV_HEADS):
        kt = jnp.concatenate([slabs[k][0, g].astype(BF16) for k in range(pp)], axis=1)
        vt = jnp.concatenate([slabs[k][1, g].astype(BF16) for k in range(pp)], axis=1)
        bias = jnp.where(_dot(sel_s[g].astype(BF16), expand) > 0.5, 0.0, NEG_INF)
        s = _dot(_nsa_q4(hq_ref, g), kt) + _tile_rows(bias, NSA_GROUP)
        _st_update(m_s.at[g], l_s.at[g], acc_s.at[g], s, lambda pb, vt=vt: _dot_nt(pb, vt))

    @pl.when(p == n_steps - 1)
    def _():
        gates = jax.nn.sigmoid(misc_ref[...])
        for g in range(NSA_KV_HEADS):
            kcol = slice(g * 64, (g + 1) * 64)
            vcol = slice(128 + g * 64, 128 + (g + 1) * 64)
            q4 = _nsa_q4(hq_ref, g)
            st = (m_s.at[g], l_s.at[g], acc_s.at[g])
            k = _pad_rows(slc_ref[:, kcol].astype(BF16), LANES)
            v = _pad_rows(slc_ref[:, vcol].astype(BF16), LANES)
            selx = _dot(sel_s[g].astype(BF16), _block_expand(l_past, LANES))
            bias = jnp.where((selx > 0.5) & new_ok, 0.0, NEG_INF)
            _st_update(*st, _dot_nt(q4, k) + _tile_rows(bias, NSA_GROUP), lambda pb: _dot(pb, v))
            o_slc = _st_finish(st[1], st[2])
            _st_reset(*st)
            kt = winp_ref[0, g].astype(BF16)
            vt = winp_ref[1, g].astype(BF16)
            wpos = l_past - w_past + _iota((1, w_past), 1)
            bias = jnp.where(wpos > qpos - WINDOW, 0.0, NEG_INF)
            _st_update(*st, _dot(q4, kt) + _tile_rows(bias, NSA_GROUP), lambda pb: _dot_nt(pb, vt))
            k = _pad_rows(win_ref[:, kcol].astype(BF16), LANES)
            v = _pad_rows(win_ref[:, vcol].astype(BF16), LANES)
            bias = jnp.where(new_ok & (new_pos > qpos - WINDOW), 0.0, NEG_INF)
            _st_update(*st, _dot_nt(q4, k) + _tile_rows(bias, NSA_GROUP), lambda pb: _dot(pb, v))
            o_win = _st_finish(st[1], st[2])
            _nsa_write(o_ref, gates, g, dt, ocmp_s[g], o_slc, o_win)


def _nsa_sample(h, ckv, slc_v, win_v, pt_flat, *, layer, row0, n_seq, dt, n_pages, pp, w_past, topn):
    l_past = n_pages * PAGE_SIZE
    l_tot = l_past + dt
    nc = l_tot // CMP_BLOCK
    ns = -(-l_tot // SLC_BLOCK)
    assert ns <= SEL_W and nc <= 2 * SEL_W and dt <= LANES
    n_steps = n_pages // pp
    rb0 = row0 // dt
    rows4 = NSA_GROUP * dt

    def hblk(w, col):
        return pl.BlockSpec((dt, w), lambda s, p, pt: (rb0 + s, col))

    in_specs = [hblk(512, E_NQ // 512), hblk(LANES, E_MISC // LANES), hblk(256, E_SLC // 256), hblk(256, E_WIN // 256),
                pl.BlockSpec((None, 4, SEL_W, LANES), lambda s, p, pt: (s, 0, 0, 0)),
                pl.BlockSpec((None, None, 2, NSA_KV_HEADS, HEAD_DIM, w_past), lambda s, p, pt: (s, layer, 0, 0, 0, 0))]
    in_specs += _page_specs((None, None, 2, NSA_KV_HEADS, HEAD_DIM, PAGE_SIZE), n_pages, pp, layer)
    return pl.pallas_call(
        functools.partial(_nsa_sample_kernel, pp=pp, n_steps=n_steps, dt=dt, l_past=l_past, w_past=w_past,
                          nc=nc, ns=ns, topn=min(topn, ns)),
        grid_spec=pltpu.PrefetchScalarGridSpec(
            num_scalar_prefetch=1, grid=(n_seq, n_steps), in_specs=in_specs,
            out_specs=pl.BlockSpec((dt, NSA_HEADS * HEAD_DIM), lambda s, p, pt: (s, 0)),
            scratch_shapes=[pltpu.VMEM((NSA_KV_HEADS, dt, SEL_W), F32), pltpu.VMEM((NSA_KV_HEADS, rows4, HEAD_DIM), F32),
                            pltpu.VMEM((NSA_KV_HEADS, rows4, 1), F32), pltpu.VMEM((NSA_KV_HEADS, rows4, 1), F32),
                            pltpu.VMEM((NSA_KV_HEADS, rows4, HEAD_DIM), F32)]),
        out_shape=jax.ShapeDtypeStruct((n_seq * dt, NSA_HEADS * HEAD_DIM), F32),
        compiler_params=pltpu.CompilerParams(dimension_semantics=("arbitrary", "arbitrary"),
                                             vmem_limit_bytes=VMEM_LIMIT),
        name="nsa_sample",
    )(pt_flat, h, h, h, h, ckv, win_v, *([slc_v] * pp))


def _fox_sample_kernel(pt_ref, *refs, pp, n_steps, dt, l_past):
    fq_ref, fk_ref, fv_ref, misc_ref, fb_ref = refs[:5]
    kv_slabs = refs[5:5 + pp]
    lf_slabs = refs[5 + pp:5 + 2 * pp]
    o_ref, lf_ref, qbd_s, carry_s, m_s, l_s, acc_s = refs[5 + 2 * pp:]
    p = pl.program_id(1)
    rows = FOX_HEADS * dt
    width = FOX_HEADS * HEAD_DIM

    @pl.when(p == 0)
    def _():
        qbd_s[...] = jnp.zeros_like(qbd_s)
        for hh in range(FOX_HEADS):
            cs = slice(hh * 64, (hh + 1) * 64)
            qbd_s[hh * dt:(hh + 1) * dt, cs] = (fq_ref[:, cs] * 0.125).astype(BF16)
        carry_s[...] = jnp.zeros_like(carry_s)
        _st_reset(m_s, l_s, acc_s)

    def head_rows(x):
        return jnp.concatenate([jnp.broadcast_to(x[hh:hh + 1, :], (dt, LANES)) for hh in range(FOX_HEADS)], axis=0)

    upper = jnp.where(_iota((LANES, LANES), 0) <= _iota((LANES, LANES), 1), 1.0, 0.0).astype(BF16)
    offset = carry_s[...]
    cums = []
    for k in range(pp):
        hi, mid, lo = _split3(lf_slabs[k][...])
        local = _dot(hi.astype(BF16), upper) + _dot(mid.astype(BF16), upper) + _dot(lo.astype(BF16), upper)
        cums.append(local + offset)
        offset = jnp.broadcast_to(cums[-1][:, LANES - 1:LANES], (FOX_HEADS, LANES))
    carry_s[...] = offset
    kt = jnp.concatenate([kv_slabs[k][0].reshape(width, PAGE_SIZE).astype(BF16) for k in range(pp)], axis=1)
    vt = jnp.concatenate([kv_slabs[k][1].reshape(width, PAGE_SIZE).astype(BF16) for k in range(pp)], axis=1)
    cum = jnp.concatenate(cums, axis=1)
    wide_rows = jnp.concatenate([jnp.broadcast_to(cum[hh:hh + 1, :], (dt, pp * PAGE_SIZE)) for hh in range(FOX_HEADS)],
                                axis=0)
    _st_update(m_s, l_s, acc_s, _dot(qbd_s[...], kt) - wide_rows, lambda pb: _dot_nt(pb, vt))

    @pl.when(p == n_steps - 1)
    def _():
        lf = _forget_lanes(_log_sigmoid(misc_ref[...] + fb_ref[...]))
        lf_ref[...] = lf
        parts = _split3(_pad_rows(_cum_small(lf), LANES))
        lane = _iota((dt, LANES), 1)
        pick = jnp.concatenate([jnp.where(lane == MISC_FF + hh, 1.0, 0.0) for hh in range(FOX_HEADS)], axis=0)
        pick = pick.astype(BF16)
        cum_new = sum(_dot_nt(pick, part.astype(BF16)) for part in parts)
        kn = _pad_rows(fk_ref[...].astype(BF16), LANES)
        vn = _pad_rows(fv_ref[...].astype(BF16), LANES)
        col = _iota((1, LANES), 1)
        trow = _tile_rows(_iota((dt, 1), 0), FOX_HEADS)
        ok = (col <= trow) & (col < dt)
        s = _dot_nt(qbd_s[...], kn) - (head_rows(carry_s[...]) + cum_new)
        _st_update(m_s, l_s, acc_s, jnp.where(ok, s, NEG_INF), lambda pb: _dot(pb, vn))
        o = _st_finish(l_s, acc_s)
        for hh in range(FOX_HEADS):
            o_ref[:, hh * 64:(hh + 1) * 64] = o[hh * dt:(hh + 1) * dt, hh * 64:(hh + 1) * 64]


def _fox_sample(h, fb, fox_v, flf_v, pt_flat, *, layer, row0, n_seq, dt, n_pages, pp):
    l_past = n_pages * PAGE_SIZE
    n_steps = n_pages // pp
    rb0 = row0 // dt
    rows = FOX_HEADS * dt
    width = FOX_HEADS * HEAD_DIM
    assert dt <= LANES

    def hblk(w, col):
        return pl.BlockSpec((dt, w), lambda s, p, pt: (rb0 + s, col))

    in_specs = [hblk(512, E_FQ // 512), hblk(512, E_FK // 512), hblk(512, E_FV // 512), hblk(LANES, E_MISC // LANES),
                pl.BlockSpec((1, LANES), lambda s, p, pt: (0, 0))]
    in_specs += _page_specs((None, None, 2, FOX_HEADS, HEAD_DIM, PAGE_SIZE), n_pages, pp, layer)
    in_specs += _page_specs((None, None, FOX_HEADS, PAGE_SIZE), n_pages, pp, layer)
    return pl.pallas_call(
        functools.partial(_fox_sample_kernel, pp=pp, n_steps=n_steps, dt=dt, l_past=l_past),
        grid_spec=pltpu.PrefetchScalarGridSpec(
            num_scalar_prefetch=1, grid=(n_seq, n_steps), in_specs=in_specs,
            out_specs=[pl.BlockSpec((dt, width), lambda s, p, pt: (s, 0)),
                       pl.BlockSpec((dt, LANES), lambda s, p, pt: (s, 0))],
            scratch_shapes=[pltpu.VMEM((rows, width), BF16), pltpu.VMEM((FOX_HEADS, LANES), F32),
                            pltpu.VMEM((rows, 1), F32), pltpu.VMEM((rows, 1), F32), pltpu.VMEM((rows, width), F32)]),
        out_shape=[jax.ShapeDtypeStruct((n_seq * dt, width), F32), jax.ShapeDtypeStruct((n_seq * dt, LANES), F32)],
        compiler_params=pltpu.CompilerParams(dimension_semantics=("arbitrary", "arbitrary"),
                                             vmem_limit_bytes=VMEM_LIMIT),
        name="fox_sample",
    )(pt_flat, h, h, h, h, fb, *([fox_v] * pp), *([flf_v] * pp))


def _dsa_index_kernel(pt_ref, *refs, pp, n_steps, dt, n_pages, n_keep):
    qi_ref, kw_ref = refs[:2]
    slabs = refs[2:2 + pp]
    o_ref, sc_s, key_s = refs[2 + pp:]
    p = pl.program_id(1)
    wi = kw_ref[:, IDX_DIM:IDX_DIM + IDX_HEADS] * (IDX_HEADS ** -0.5)
    qst = jnp.concatenate([qi_ref[:, hh * IDX_DIM:(hh + 1) * IDX_DIM] for hh in range(IDX_HEADS)], axis=0)
    qst = (qst * (IDX_DIM ** -0.5)).astype(BF16)

    def scores(lg):
        sc = jnp.zeros((dt, lg.shape[1]), F32)
        for hh in range(IDX_HEADS):
            sc = sc + wi[:, hh:hh + 1] * jnp.maximum(lg[hh * dt:(hh + 1) * dt], 0.0)
        return sc + 0.0

    wide = scores(_dot(qst, jnp.concatenate([slabs[k][...].astype(BF16) for k in range(pp)], axis=1)))
    for k in range(pp):
        sc_s[p * pp + k] = wide[:, k * PAGE_SIZE:(k + 1) * PAGE_SIZE]

    @pl.when(p == n_steps - 1)
    def _():
        kin = _pad_rows(kw_ref[:, 0:IDX_DIM].astype(BF16), LANES)
        col = _iota((1, LANES), 1)
        ok = (col <= _iota((dt, 1), 0)) & (col < dt)
        sc_s[n_pages] = jnp.where(ok, scores(_dot_nt(qst, kin)), NEG_INF)
        key_s[...] = _sortable_key(sc_s[...])
        shape = (n_pages + 1, dt, LANES)
        idx = _iota(shape, 0) * LANES + _iota(shape, 2)
        _topk_bias_store(o_ref, key_s, n_keep, idx,
                         lambda mask: jnp.sum(jnp.sum(jnp.where(mask, 1.0, 0.0), axis=0), axis=-1, keepdims=True)[None],
                         (1, dt, 1), bits=4)


def _dsa_index(h, dik_v, pt_flat, *, layer, row0, n_seq, dt, n_pages, pp, n_keep):
    n_steps = n_pages // pp
    rb0 = row0 // dt
    shape = (n_pages + 1, dt, LANES)
    in_specs = [pl.BlockSpec((dt, 512), lambda s, p, pt: (rb0 + s, O_QI // 512)),
                pl.BlockSpec((dt, LANES), lambda s, p, pt: (rb0 + s, O_KI // LANES))]
    in_specs += _page_specs((None, None, IDX_DIM, PAGE_SIZE), n_pages, pp, layer)
    return pl.pallas_call(
        functools.partial(_dsa_index_kernel, pp=pp, n_steps=n_steps, dt=dt, n_pages=n_pages, n_keep=n_keep),
        grid_spec=pltpu.PrefetchScalarGridSpec(
            num_scalar_prefetch=1, grid=(n_seq, n_steps), in_specs=in_specs,
            out_specs=pl.BlockSpec((None,) + shape, lambda s, p, pt: (s, 0, 0, 0)),
            scratch_shapes=[pltpu.VMEM(shape, F32), pltpu.VMEM(shape, jnp.int32)]),
        out_shape=jax.ShapeDtypeStruct((n_seq,) + shape, F32),
        compiler_params=pltpu.CompilerParams(dimension_semantics=("arbitrary", "arbitrary")),
        name="dsa_index",
    )(pt_flat, h, h, *([dik_v] * pp))


def _dsa_sample_kernel(pt_ref, *refs, pp, n_steps, dt, n_pages):
    q_ref, kv_ref, bias_ref = refs[:3]
    slabs = refs[3:3 + pp]
    o_ref, qbd_s, m_s, l_s, acc_s = refs[3 + pp:]
    p = pl.program_id(1)
    width = DSA_KV_HEADS * HEAD_DIM

    @pl.when(p == 0)
    def _():
        qbd_s[...] = jnp.zeros_like(qbd_s)
        for hh in range(DSA_HEADS):
            kh = hh // DSA_GROUP
            qbd_s[hh * dt:(hh + 1) * dt, kh * 64:(kh + 1) * 64] = (q_ref[:, hh * 64:(hh + 1) * 64] * 0.125).astype(BF16)
        _st_reset(m_s, l_s, acc_s)

    kt = jnp.concatenate([slabs[k][0].reshape(width, PAGE_SIZE).astype(BF16) for k in range(pp)], axis=1)
    vt = jnp.concatenate([slabs[k][1].reshape(width, PAGE_SIZE).astype(BF16) for k in range(pp)], axis=1)
    bias = jnp.concatenate([bias_ref[p * pp + k] for k in range(pp)], axis=1)
    _st_update(m_s, l_s, acc_s, _dot(qbd_s[...], kt) + _tile_rows(bias, DSA_HEADS), lambda pb: _dot_nt(pb, vt))

    @pl.when(p == n_steps - 1)
    def _():
        kn = _pad_rows(kv_ref[:, 0:width].astype(BF16), LANES)
        vn = _pad_rows(kv_ref[:, width:2 * width].astype(BF16), LANES)
        s = _dot_nt(qbd_s[...], kn) + _tile_rows(bias_ref[n_pages], DSA_HEADS)
        _st_update(m_s, l_s, acc_s, s, lambda pb: _dot(pb, vn))
        o = _st_finish(l_s, acc_s)
        for hh in range(DSA_HEADS):
            kh = hh // DSA_GROUP
            o_ref[:, hh * 64:(hh + 1) * 64] = o[hh * dt:(hh + 1) * dt, kh * 64:(kh + 1) * 64]


def _dsa_sample(h, bias, dkv_v, pt_flat, *, layer, row0, n_seq, dt, n_pages, pp):
    n_steps = n_pages // pp
    rb0 = row0 // dt
    rows = DSA_HEADS * dt
    width = DSA_KV_HEADS * HEAD_DIM
    in_specs = [pl.BlockSpec((dt, 1024), lambda s, p, pt: (rb0 + s, O_Q // 1024)),
                pl.BlockSpec((dt, 512), lambda s, p, pt: (rb0 + s, O_KV // 512)),
                pl.BlockSpec((None, n_pages + 1, dt, LANES), lambda s, p, pt: (s, 0, 0, 0))]
    in_specs += _page_specs((None, None, 2, DSA_KV_HEADS, HEAD_DIM, PAGE_SIZE), n_pages, pp, layer)
    return pl.pallas_call(
        functools.partial(_dsa_sample_kernel, pp=pp, n_steps=n_steps, dt=dt, n_pages=n_pages),
        grid_spec=pltpu.PrefetchScalarGridSpec(
            num_scalar_prefetch=1, grid=(n_seq, n_steps), in_specs=in_specs,
            out_specs=pl.BlockSpec((dt, DSA_HEADS * HEAD_DIM), lambda s, p, pt: (s, 0)),
            scratch_shapes=[pltpu.VMEM((rows, width), BF16), pltpu.VMEM((rows, 1), F32), pltpu.VMEM((rows, 1), F32),
                            pltpu.VMEM((rows, width), F32)]),
        out_shape=jax.ShapeDtypeStruct((n_seq * dt, DSA_HEADS * HEAD_DIM), F32),
        compiler_params=pltpu.CompilerParams(dimension_semantics=("arbitrary", "arbitrary"),
                                             vmem_limit_bytes=VMEM_LIMIT),
        name="dsa_sample",
    )(pt_flat, h, h, bias, *([dkv_v] * pp))


def _seq_page_specs(block, n_pages, useq, layer):
    tail = (0,) * (len(block) - 2)
    return [pl.BlockSpec(block, lambda s, pt, u=u, k=k: (pt[(s * useq + u) * n_pages + k], layer) + tail)
            for u in range(useq) for k in range(n_pages)]


def _softmax_over(blocks):
    m = None
    for s, _ in blocks:
        mb = jnp.max(s, axis=-1, keepdims=True)
        m = mb if m is None else jnp.maximum(m, mb)
    m = jnp.where(m == NEG_INF, 0.0, m)
    num = den = None
    for s, pv in blocks:
        p = jnp.exp(s - m)
        d = jnp.sum(p, axis=-1, keepdims=True)
        n = pv(p.astype(BF16))
        num, den = (n, d) if num is None else (num + n, den + d)
    return num / jnp.maximum(den, 1e-30)


def _wide(slabs, pick):
    return jnp.concatenate([pick(slab).astype(BF16) for slab in slabs], axis=1)


def _cmp_seq_kernel(pt_ref, *refs, useq, n_pages, nc):
    slabs = refs[:useq * n_pages]
    pe_ref, phi_ref, crt_ref, o_ref, k_s, v_s, c_s = refs[useq * n_pages:]
    for u in range(useq):
        for k in range(n_pages):
            slab = slabs[u * n_pages + k]
            rows = slice(k * PAGE_SIZE, (k + 1) * PAGE_SIZE)
            for g in range(NSA_KV_HEADS):
                k_s[u, rows, g * 64:(g + 1) * 64] = slab[0, g].T
                v_s[u, rows, g * 64:(g + 1) * 64] = slab[1, g].T
        ck, cv = _cmp_summaries(k_s.at[u], v_s.at[u], pe_ref, phi_ref, crt_ref, nc)
        for idx, val in ((0, ck), (2, cv)):
            c_s[u] = jnp.zeros(c_s.shape[1:], F32)
            c_s[u, 0:nc, :] = val
            o_ref[u, idx] = c_s[u, pl.ds(0, SEL_W, stride=2), :]
            o_ref[u, idx + 1] = c_s[u, pl.ds(1, SEL_W, stride=2), :]


def _cmp_seq(cmp_v, pt_flat, pe, phi, crt, *, layer, n_seq, n_pages, useq, nc):
    lp = n_pages * PAGE_SIZE
    in_specs = _seq_page_specs((None, None, 2, NSA_KV_HEADS, HEAD_DIM, PAGE_SIZE), n_pages, useq, layer)
    in_specs += [pl.BlockSpec(pe.shape, lambda s, pt: (0, 0, 0)),
                 pl.BlockSpec(phi.shape, lambda s, pt: (0, 0, 0, 0)),
                 pl.BlockSpec(crt.shape, lambda s, pt: (0, 0))]
    return pl.pallas_call(
        functools.partial(_cmp_seq_kernel, useq=useq, n_pages=n_pages, nc=nc),
        grid_spec=pltpu.PrefetchScalarGridSpec(
            num_scalar_prefetch=1, grid=(n_seq // useq,), in_specs=in_specs,
            out_specs=pl.BlockSpec((useq, 4, SEL_W, LANES), lambda s, pt: (s, 0, 0, 0)),
            scratch_shapes=[pltpu.VMEM((useq, lp, LANES), F32), pltpu.VMEM((useq, lp, LANES), F32),
                            pltpu.VMEM((useq, 2 * SEL_W, LANES), F32)]),
        out_shape=jax.ShapeDtypeStruct((n_seq, 4, SEL_W, LANES), F32),
        compiler_params=pltpu.CompilerParams(dimension_semantics=("arbitrary",), vmem_limit_bytes=VMEM_LIMIT),
        name="cmp_sample",
    )(pt_flat, *([cmp_v] * (useq * n_pages)), pe, phi, crt)


def _nsa_seq_kernel(pt_ref, *refs, useq, n_pages, dt, w_past, nc, ns, topn):
    hq_ref, misc_ref, slc_ref, win_ref, ckv_ref, winp_ref = refs[:6]
    slabs = refs[6:6 + useq * n_pages]
    o_ref = refs[6 + useq * n_pages]
    l_past = n_pages * PAGE_SIZE
    qpos = l_past + _iota((dt, 1), 0)
    new_pos = l_past + _iota((1, LANES), 1)
    new_ok = (new_pos <= qpos) & (new_pos < l_past + dt)
    expand_past = _block_expand(0, l_past)
    expand_new = _block_expand(l_past, LANES)
    wpos = l_past - w_past + _iota((1, w_past), 1)
    win_bias = _tile_rows(jnp.where(wpos > qpos - WINDOW, 0.0, NEG_INF), NSA_GROUP)
    win_new_bias = _tile_rows(jnp.where(new_ok & (new_pos > qpos - WINDOW), 0.0, NEG_INF), NSA_GROUP)
    for u in range(useq):
        rows = slice(u * dt, (u + 1) * dt)
        pages = slabs[u * n_pages:(u + 1) * n_pages]
        gates = jax.nn.sigmoid(misc_ref[rows, :])
        for g in range(NSA_KV_HEADS):
            kcol = slice(g * 64, (g + 1) * 64)
            vcol = slice(128 + g * 64, 128 + (g + 1) * 64)
            q4 = jnp.concatenate([hq_ref[rows, (NSA_GROUP * g + r) * 64:(NSA_GROUP * g + r + 1) * 64]
                                  for r in range(NSA_GROUP)], axis=0)
            q4 = (q4 * 0.125).astype(BF16)
            o_cmp, sel = _nsa_cmp_select(
                q4, ckv_ref[u, 0][:, kcol].astype(BF16), ckv_ref[u, 1][:, kcol].astype(BF16),
                ckv_ref[u, 2][:, kcol].astype(BF16), ckv_ref[u, 3][:, kcol].astype(BF16), qpos,
                tq=dt, nc=nc, ns=ns, topn=topn)
            sel = sel.astype(BF16)
            kt = _wide(pages, lambda slab: slab[0, g])
            vt = _wide(pages, lambda slab: slab[1, g])
            kn = _pad_rows(slc_ref[rows, kcol].astype(BF16), LANES)
            vn = _pad_rows(slc_ref[rows, vcol].astype(BF16), LANES)
            bias = _tile_rows(jnp.where(_dot(sel, expand_past) > 0.5, 0.0, NEG_INF), NSA_GROUP)
            bias_n = _tile_rows(jnp.where((_dot(sel, expand_new) > 0.5) & new_ok, 0.0, NEG_INF), NSA_GROUP)
            o_slc = _softmax_over([(_dot(q4, kt) + bias, lambda pb, vt=vt: _dot_nt(pb, vt)),
                                   (_dot_nt(q4, kn) + bias_n, lambda pb, vn=vn: _dot(pb, vn))])
            ktw = winp_ref[u, 0, g].astype(BF16)
            vtw = winp_ref[u, 1, g].astype(BF16)
            kn = _pad_rows(win_ref[rows, kcol].astype(BF16), LANES)
            vn = _pad_rows(win_ref[rows, vcol].astype(BF16), LANES)
            o_win = _softmax_over([(_dot(q4, ktw) + win_bias, lambda pb, vtw=vtw: _dot_nt(pb, vtw)),
                                   (_dot_nt(q4, kn) + win_new_bias, lambda pb, vn=vn: _dot(pb, vn))])
            for r in range(NSA_GROUP):
                hh = NSA_GROUP * g + r
                rs = slice(r * dt, (r + 1) * dt)
                c0 = MISC_GATES + 3 * hh
                o_ref[rows, hh * 64:(hh + 1) * 64] = (gates[:, c0:c0 + 1] * o_cmp[rs] + gates[:, c0 + 1:c0 + 2] * o_slc[rs]
                                                      + gates[:, c0 + 2:c0 + 3] * o_win[rs])


def _nsa_seq(h, ckv, slc_v, win_v, pt_flat, *, layer, row0, n_seq, dt, n_pages, useq, w_past, topn):
    l_tot = n_pages * PAGE_SIZE + dt
    nc = l_tot // CMP_BLOCK
    ns = -(-l_tot // SLC_BLOCK)
    assert ns <= SEL_W and nc <= 2 * SEL_W and dt <= LANES
    rows = useq * dt
    rb0 = row0 // rows

    def hblk(w, col):
        return pl.BlockSpec((rows, w), lambda s, pt: (rb0 + s, col))

    in_specs = [hblk(512, E_NQ // 512), hblk(LANES, E_MISC // LANES), hblk(256, E_SLC // 256), hblk(256, E_WIN // 256),
                pl.BlockSpec((useq, 4, SEL_W, LANES), lambda s, pt: (s, 0, 0, 0)),
                pl.BlockSpec((useq, None, 2, NSA_KV_HEADS, HEAD_DIM, w_past), lambda s, pt: (s, layer, 0, 0, 0, 0))]
    in_specs += _seq_page_specs((None, None, 2, NSA_KV_HEADS, HEAD_DIM, PAGE_SIZE), n_pages, useq, layer)
    return pl.pallas_call(
        functools.partial(_nsa_seq_kernel, useq=useq, n_pages=n_pages, dt=dt, w_past=w_past, nc=nc, ns=ns,
                          topn=min(topn, ns)),
        grid_spec=pltpu.PrefetchScalarGridSpec(
            num_scalar_prefetch=1, grid=(n_seq // useq,), in_specs=in_specs,
            out_specs=pl.BlockSpec((rows, NSA_HEADS * HEAD_DIM), lambda s, pt: (s, 0))),
        out_shape=jax.ShapeDtypeStruct((n_seq * dt, NSA_HEADS * HEAD_DIM), F32),
        compiler_params=pltpu.CompilerParams(dimension_semantics=("arbitrary",), vmem_limit_bytes=VMEM_LIMIT),
        name="nsa_sample",
    )(pt_flat, h, h, h, h, ckv, win_v, *([slc_v] * (useq * n_pages)))


def _fox_seq_kernel(pt_ref, *refs, useq, n_pages, dt):
    fq_ref, fk_ref, fv_ref, misc_ref, fb_ref = refs[:5]
    kv_slabs = refs[5:5 + useq * n_pages]
    lf_slabs = refs[5 + useq * n_pages:5 + 2 * useq * n_pages]
    o_ref, lf_ref, qbd_s = refs[5 + 2 * useq * n_pages:]
    width = FOX_HEADS * HEAD_DIM
    l_past = n_pages * PAGE_SIZE

    def head_rows(x):
        return jnp.concatenate([jnp.broadcast_to(x[hh:hh + 1, :], (dt, x.shape[1])) for hh in range(FOX_HEADS)], axis=0)

    upper = jnp.where(_iota((LANES, LANES), 0) <= _iota((LANES, LANES), 1), 1.0, 0.0).astype(BF16)
    lane = _iota((dt, LANES), 1)
    pick = jnp.concatenate([jnp.where(lane == MISC_FF + hh, 1.0, 0.0) for hh in range(FOX_HEADS)], axis=0).astype(BF16)
    col = _iota((1, LANES), 1)
    new_ok = (col <= _tile_rows(_iota((dt, 1), 0), FOX_HEADS)) & (col < dt)
    for u in range(useq):
        rows = slice(u * dt, (u + 1) * dt)
        qbd_s[u] = jnp.zeros(qbd_s.shape[1:], BF16)
        for hh in range(FOX_HEADS):
            cs = slice(hh * 64, (hh + 1) * 64)
            qbd_s[u, hh * dt:(hh + 1) * dt, cs] = (fq_ref[rows, cs] * 0.125).astype(BF16)
        qbd = qbd_s[u]
        offset = jnp.zeros((FOX_HEADS, LANES), F32)
        cums = []
        for k in range(n_pages):
            hi, mid, lo = _split3(lf_slabs[u * n_pages + k][...])
            local = _dot(hi.astype(BF16), upper) + _dot(mid.astype(BF16), upper) + _dot(lo.astype(BF16), upper)
            cums.append(local + offset)
            offset = jnp.broadcast_to(cums[-1][:, LANES - 1:LANES], (FOX_HEADS, LANES))
        pages = kv_slabs[u * n_pages:(u + 1) * n_pages]
        kt = _wide(pages, lambda slab: slab[0].reshape(width, PAGE_SIZE))
        vt = _wide(pages, lambda slab: slab[1].reshape(width, PAGE_SIZE))
        s_past = _dot(qbd, kt) - head_rows(jnp.concatenate(cums, axis=1))
        lf = _forget_lanes(_log_sigmoid(misc_ref[rows, :] + fb_ref[...]))
        lf_ref[rows, :] = lf
        parts = _split3(_pad_rows(_cum_small(lf), LANES))
        cum_new = sum(_dot_nt(pick, part.astype(BF16)) for part in parts)
        kn = _pad_rows(fk_ref[rows, :].astype(BF16), LANES)
        vn = _pad_rows(fv_ref[rows, :].astype(BF16), LANES)
        s_new = jnp.where(new_ok, _dot_nt(qbd, kn) - (head_rows(offset) + cum_new), NEG_INF)
        o = _softmax_over([(s_past, lambda pb, vt=vt: _dot_nt(pb, vt)), (s_new, lambda pb, vn=vn: _dot(pb, vn))])
        for hh in range(FOX_HEADS):
            o_ref[rows, hh * 64:(hh + 1) * 64] = o[hh * dt:(hh + 1) * dt, hh * 64:(hh + 1) * 64]


def _fox_seq(h, fb, fox_v, flf_v, pt_flat, *, layer, row0, n_seq, dt, n_pages, useq):
    rows = useq * dt
    rb0 = row0 // rows
    width = FOX_HEADS * HEAD_DIM
    assert dt <= LANES

    def hblk(w, col):
        return pl.BlockSpec((rows, w), lambda s, pt: (rb0 + s, col))

    in_specs = [hblk(512, E_FQ // 512), hblk(512, E_FK // 512), hblk(512, E_FV // 512), hblk(LANES, E_MISC // LANES),
                pl.BlockSpec((1, LANES), lambda s, pt: (0, 0))]
    in_specs += _seq_page_specs((None, None, 2, FOX_HEADS, HEAD_DIM, PAGE_SIZE), n_pages, useq, layer)
    in_specs += _seq_page_specs((None, None, FOX_HEADS, PAGE_SIZE), n_pages, useq, layer)
    return pl.pallas_call(
        functools.partial(_fox_seq_kernel, useq=useq, n_pages=n_pages, dt=dt),
        grid_spec=pltpu.PrefetchScalarGridSpec(
            num_scalar_prefetch=1, grid=(n_seq // useq,), in_specs=in_specs,
            out_specs=[pl.BlockSpec((rows, width), lambda s, pt: (s, 0)),
                       pl.BlockSpec((rows, LANES), lambda s, pt: (s, 0))],
            scratch_shapes=[pltpu.VMEM((useq, FOX_HEADS * dt, width), BF16)]),
        out_shape=[jax.ShapeDtypeStruct((n_seq * dt, width), F32), jax.ShapeDtypeStruct((n_seq * dt, LANES), F32)],
        compiler_params=pltpu.CompilerParams(dimension_semantics=("arbitrary",), vmem_limit_bytes=VMEM_LIMIT),
        name="fox_sample",
    )(pt_flat, h, h, h, h, fb, *([fox_v] * (useq * n_pages)), *([flf_v] * (useq * n_pages)))


def _dsa_seq_kernel(pt_ref, *refs, useq, n_pages, dt, n_keep):
    q_ref, qi_ref, kw_ref, kv_ref = refs[:4]
    idx_slabs = refs[4:4 + useq * n_pages]
    kv_slabs = refs[4 + useq * n_pages:4 + 2 * useq * n_pages]
    o_ref, sc_s, key_s, qbd_s = refs[4 + 2 * useq * n_pages:]
    width = DSA_KV_HEADS * HEAD_DIM
    col = _iota((1, LANES), 1)
    new_ok = (col <= _iota((dt, 1), 0)) & (col < dt)

    for u in range(useq):
        rows = slice(u * dt, (u + 1) * dt)
        wi = kw_ref[rows, IDX_DIM:IDX_DIM + IDX_HEADS] * (IDX_HEADS ** -0.5)
        qst = jnp.concatenate([qi_ref[rows, hh * IDX_DIM:(hh + 1) * IDX_DIM] for hh in range(IDX_HEADS)], axis=0)
        qst = (qst * (IDX_DIM ** -0.5)).astype(BF16)

        def scores(lg, wi=wi):
            sc = jnp.zeros((dt, lg.shape[1]), F32)
            for hh in range(IDX_HEADS):
                sc = sc + wi[:, hh:hh + 1] * jnp.maximum(lg[hh * dt:(hh + 1) * dt], 0.0)
            return sc + 0.0

        wide = scores(_dot(qst, _wide(idx_slabs[u * n_pages:(u + 1) * n_pages], lambda slab: slab[...])))
        for k in range(n_pages):
            sc_s[k, rows, :] = wide[:, k * PAGE_SIZE:(k + 1) * PAGE_SIZE]
        kin = _pad_rows(kw_ref[rows, 0:IDX_DIM].astype(BF16), LANES)
        sc_s[n_pages, rows, :] = jnp.where(new_ok, scores(_dot_nt(qst, kin)), NEG_INF)
    key_s[...] = _sortable_key(sc_s[...])
    shape = (n_pages + 1, useq * dt, LANES)
    _topk_bias_store(sc_s, key_s, n_keep, _iota(shape, 0) * LANES + _iota(shape, 2),
                     lambda mask: jnp.sum(jnp.sum(jnp.where(mask, 1.0, 0.0), axis=0), axis=-1, keepdims=True)[None],
                     (1, useq * dt, 1), bits=4)

    for u in range(useq):
        rows = slice(u * dt, (u + 1) * dt)
        qbd_s[u] = jnp.zeros(qbd_s.shape[1:], BF16)
        for hh in range(DSA_HEADS):
            kh = hh // DSA_GROUP
            qbd_s[u, hh * dt:(hh + 1) * dt, kh * 64:(kh + 1) * 64] = (q_ref[rows, hh * 64:(hh + 1) * 64] * 0.125).astype(BF16)
        qbd = qbd_s[u]
        pages = kv_slabs[u * n_pages:(u + 1) * n_pages]
        kt = _wide(pages, lambda slab: slab[0].reshape(width, PAGE_SIZE))
        vt = _wide(pages, lambda slab: slab[1].reshape(width, PAGE_SIZE))
        bias = jnp.concatenate([sc_s[k, rows, :] for k in range(n_pages)], axis=1)
        kn = _pad_rows(kv_ref[rows, 0:width].astype(BF16), LANES)
        vn = _pad_rows(kv_ref[rows, width:2 * width].astype(BF16), LANES)
        o = _softmax_over([(_dot(qbd, kt) + _tile_rows(bias, DSA_HEADS), lambda pb, vt=vt: _dot_nt(pb, vt)),
                           (_dot_nt(qbd, kn) + _tile_rows(sc_s[n_pages, rows, :], DSA_HEADS),
                            lambda pb, vn=vn: _dot(pb, vn))])
        for hh in range(DSA_HEADS):
            kh = hh // DSA_GROUP
            o_ref[rows, hh * 64:(hh + 1) * 64] = o[hh * dt:(hh + 1) * dt, kh * 64:(kh + 1) * 64]


def _dsa_seq(h, dik_v, dkv_v, pt_flat, *, layer, row0, n_seq, dt, n_pages, useq, n_keep):
    rows = useq * dt
    rb0 = row0 // rows
    width = DSA_KV_HEADS * HEAD_DIM
    shape = (n_pages + 1, rows, LANES)

    def hblk(w, col):
        return pl.BlockSpec((rows, w), lambda s, pt: (rb0 + s, col))

    in_specs = [hblk(1024, O_Q // 1024), hblk(512, O_QI // 512), hblk(LANES, O_KI // LANES), hblk(512, O_KV // 512)]
    in_specs += _seq_page_specs((None, None, IDX_DIM, PAGE_SIZE), n_pages, useq, layer)
    in_specs += _seq_page_specs((None, None, 2, DSA_KV_HEADS, HEAD_DIM, PAGE_SIZE), n_pages, useq, layer)
    return pl.pallas_call(
        functools.partial(_dsa_seq_kernel, useq=useq, n_pages=n_pages, dt=dt, n_keep=n_keep),
        grid_spec=pltpu.PrefetchScalarGridSpec(
            num_scalar_prefetch=1, grid=(n_seq // useq,), in_specs=in_specs,
            out_specs=pl.BlockSpec((rows, DSA_HEADS * HEAD_DIM), lambda s, pt: (s, 0)),
            scratch_shapes=[pltpu.VMEM(shape, F32), pltpu.VMEM(shape, jnp.int32),
                            pltpu.VMEM((useq, DSA_HEADS * dt, width), BF16)]),
        out_shape=jax.ShapeDtypeStruct((n_seq * dt, DSA_HEADS * HEAD_DIM), F32),
        compiler_params=pltpu.CompilerParams(dimension_semantics=("arbitrary",), vmem_limit_bytes=VMEM_LIMIT),
        name="dsa_sample",
    )(pt_flat, h, h, h, h, *([dik_v] * (useq * n_pages)), *([dkv_v] * (useq * n_pages)))


def _rope_table(pos):
    half = ROT_DIM // 2
    inv = ROPE_THETA ** (-2.0 * jnp.arange(half, dtype=F32) / ROT_DIM)
    ang = pos.astype(F32)[:, None] * inv[None, :]
    cos, sin = jnp.cos(ang), jnp.sin(ang)
    n = pos.shape[0]
    one = jnp.ones((n, HEAD_DIM - ROT_DIM), F32)
    zero = jnp.zeros((n, HEAD_DIM - ROT_DIM), F32)
    z8 = jnp.zeros((n, half), F32)
    c = jnp.concatenate([cos, cos, one], axis=1)
    s1 = jnp.concatenate([z8, sin, zero], axis=1)
    s2 = jnp.concatenate([-sin, z8, zero], axis=1)
    return jnp.concatenate([c, c, s1, s1, s2, s2], axis=1)


def _largest_tile(limit, *sizes):
    t = limit
    while any(s % t for s in sizes):
        t //= 2
    return t


def _rows_last(pool):
    nd = pool.ndim
    return jnp.transpose(pool, (0, 1) + tuple(range(3, nd)) + (2,))


def kernel(x_prompt, x_sample, cache_nsa_cmp_kv, cache_nsa_slc_kv, state_nsa_win_kv, cache_fox_kv, cache_fox_logf, cache_dsa_kv, cache_dsa_idx_k, page_table, ln_g, ln_b, w_in_even, w_out_even, fox_f_bias, nsa_cmp_pos, nsa_cmp_phi, w_in_odd, w_out_odd, ffn_gu, ffn_down, moe_router, moe_gu, moe_down):
    nb, t, d = x_prompt.shape
    db, dt, _ = x_sample.shape
    n_pages = page_table.shape[1]
    lp = n_pages * PAGE_SIZE
    w_past = state_nsa_win_kv.shape[2]
    n_p, n_s = nb * t, db * dt
    depth = ln_g.shape[0]
    alpha = (2 * depth) ** 0.25
    tm = _largest_tile(512, n_p, n_s)
    tm_ff = _largest_tile(512, n_p, n_s)
    tq = _largest_tile(256, t)
    useq = math.gcd(db, SEQS_PER_STEP)
    pt_flat = page_table.reshape(-1).astype(jnp.int32)

    cmp_v, slc_v, win_v = _rows_last(cache_nsa_cmp_kv), _rows_last(cache_nsa_slc_kv), _rows_last(state_nsa_win_kv)
    fox_v, flf_v = _rows_last(cache_fox_kv), _rows_last(cache_fox_logf)
    dkv_v, dik_v = _rows_last(cache_dsa_kv), _rows_last(cache_dsa_idx_k)

    x = jnp.concatenate([x_prompt.reshape(n_p, d), x_sample.reshape(n_s, d)], axis=0)
    pos = jnp.concatenate([jnp.tile(jnp.arange(t), nb), jnp.tile(lp + jnp.arange(dt), db)])
    rt = _rope_table(pos)
    nc_p, nc_s = t // CMP_BLOCK, (lp + dt) // CMP_BLOCK
    assert lp % CMP_BLOCK == 0 and dt < CMP_BLOCK
    crt_p = _rope_table((jnp.arange(nc_p) + 1) * CMP_BLOCK - 1)
    crt_s = _rope_table((jnp.arange(nc_s) + 1) * CMP_BLOCK - 1)

    ev_p, ev_s, od_p, od_s = [], [], [], []
    for layer in range(depth):
        j = layer // 2
        g0, b0 = ln_g[layer, 0][None, :], ln_b[layer, 0][None, :]
        g1, b1 = ln_g[layer, 1][None, :], ln_b[layer, 1][None, :]
        if layer % 2 == 0:
            w = w_in_even[j]
            w = jnp.concatenate([w[:, 0:512], w[:, 1304:2840], w[:, 512:1280], w[:, 1280:1304], w[:, 2840:2848],
                                 jnp.zeros((d, E_W - 2848), w.dtype)], axis=1).astype(BF16)
            h = _proj(x, w, rt, width=E_W, rope_full=EVEN_ROPE, rope_lo=(), tm=tm)

            pe = jnp.concatenate([nsa_cmp_pos[j], nsa_cmp_pos[j]], axis=-1)
            ph = nsa_cmp_phi[j]
            zz = jnp.zeros_like(ph)
            phi = jnp.concatenate([jnp.concatenate([ph, zz], axis=-1),
                                   jnp.concatenate([zz, ph], axis=-1)], axis=-2).astype(BF16)
            fb = jnp.zeros((1, LANES), F32).at[0, MISC_FF:MISC_FF + FOX_HEADS].set(fox_f_bias[j])

            qa_p, ka_p, lf_p = _fox_prep(h, fb, n_seq=nb, t=t, rows=_largest_tile(256, t))
            o_nsa_p = _nsa_attn(h, pe, phi, crt_p, n_seq=nb, t=t, tq=tq, topn=SLC_TOPN)
            o_fox_p = _fox_attn(qa_p, ka_p, h, n_seq=nb, t=t, tq=tq, hps=FOX_HEADS)

            ckv = _cmp_seq(cmp_v, pt_flat, pe, phi, crt_s, layer=j, n_seq=db, n_pages=n_pages, useq=useq, nc=nc_s)
            o_nsa_s = _nsa_seq(h, ckv, slc_v, win_v, pt_flat, layer=j, row0=n_p, n_seq=db, dt=dt, n_pages=n_pages,
                               useq=useq, w_past=w_past, topn=SLC_TOPN)
            o_fox_s, lf_s = _fox_seq(h, fb, fox_v, flf_v, pt_flat, layer=j, row0=n_p, n_seq=db, dt=dt,
                                     n_pages=n_pages, useq=useq)

            wo = w_out_even[j].astype(BF16)
            x = _out_ln([o_nsa_p, o_fox_p], [o_nsa_s, o_fox_s], [wo[0:512], wo[512:1024]], x, g0, b0,
                        alpha=alpha, tm=tm)
            x = _ffn_ln(x, ffn_gu[j].astype(BF16), ffn_down[j].astype(BF16), g1, b1, alpha=alpha, tm=tm_ff,
                        fc=ffn_down.shape[1] // 2)

            hp, hs = h[:n_p].reshape(nb, t, E_W), h[n_p:].reshape(db, dt, E_W)
            kv2 = (2, NSA_KV_HEADS, HEAD_DIM)
            win_new = hs[:, :, E_WIN:E_WIN + 256].reshape((db, dt) + kv2)
            keep_p, keep_s = min(WINDOW, t), min(WINDOW, w_past + dt)
            win_s = jnp.concatenate([state_nsa_win_kv[:, j, w_past + dt - keep_s:], win_new], axis=1)
            lf_p3 = lf_p[:, MISC_FF:MISC_FF + FOX_HEADS].reshape(nb, t, FOX_HEADS)
            lf_s3 = lf_s[:, MISC_FF:MISC_FF + FOX_HEADS].reshape(db, dt, FOX_HEADS)
            ev_p.append((hp[:, :, E_CMP:E_CMP + 256], hp[:, :, E_SLC:E_SLC + 256],
                         hp[:, t - keep_p:, E_WIN:E_WIN + 256], hp[:, :, E_FK:E_FK + 1024], lf_p3))
            ev_s.append((hs[:, :, E_CMP:E_CMP + 256], hs[:, :, E_SLC:E_SLC + 256],
                         win_s, hs[:, :, E_FK:E_FK + 1024], lf_s3))
        else:
            w = jnp.concatenate([w_in_odd[j], jnp.zeros((d, O_W - w_in_odd.shape[2]), w_in_odd.dtype)],
                                axis=1).astype(BF16)
            h = _proj(x, w, rt, width=O_W, rope_full=ODD_ROPE, rope_lo=ODD_ROPE_LO, tm=tm)
            o_p = _dsa_attn(h, n_seq=nb, t=t, tq=tq, n_keep=min(IDX_TOPK, t // 4))
            o_s = _dsa_seq(h, dik_v, dkv_v, pt_flat, layer=j, row0=n_p, n_seq=db, dt=dt, n_pages=n_pages,
                           useq=useq, n_keep=min(IDX_TOPK, (lp + dt) // 4))
            x = _out_ln([o_p], [o_s], [w_out_odd[j].astype(BF16)], x, g0, b0, alpha=alpha, tm=tm)
            wr = jnp.concatenate([moe_router[j], jnp.zeros((d, LANES - N_EXPERTS), F32)], axis=1)
            wr_hi = wr.astype(BF16)
            wr_lo = (wr - wr_hi.astype(F32)).astype(BF16)
            x = _moe_ln(x, wr_hi, wr_lo, jnp.transpose(moe_gu[j], (0, 2, 1)).astype(BF16),
                        jnp.transpose(moe_down[j], (0, 2, 1)).astype(BF16), g1, b1,
                        alpha=alpha, tmm=MOE_TILE if n_p + n_s >= 8 * MOE_TILE else MOE_BLOCK,
                        fc=moe_down.shape[2] // 4, blk=MOE_BLOCK)
            hp, hs = h[:n_p].reshape(nb, t, O_W), h[n_p:].reshape(db, dt, O_W)
            od_p.append((hp[:, :, O_KV:O_KV + 512], hp[:, :, O_KI:O_KI + IDX_DIM]))
            od_s.append((hs[:, :, O_KV:O_KV + 512], hs[:, :, O_KI:O_KI + IDX_DIM]))

    def stk(lst, idx, tail):
        a = jnp.stack([s[idx] for s in lst], axis=1)
        return a.reshape(a.shape[:3] + tail)

    kv2 = (2, NSA_KV_HEADS, HEAD_DIM)
    fkv = (2, FOX_HEADS, HEAD_DIM)
    dkv = (2, DSA_KV_HEADS, HEAD_DIM)
    return (x[:n_p].reshape(nb, t, d), x[n_p:].reshape(db, dt, d),
            stk(ev_p, 0, kv2), stk(ev_s, 0, kv2), stk(ev_p, 1, kv2), stk(ev_s, 1, kv2),
            stk(ev_p, 2, kv2), stk(ev_s, 2, kv2), stk(ev_p, 3, fkv), stk(ev_s, 3, fkv),
            stk(ev_p, 4, (FOX_HEADS,)), stk(ev_s, 4, (FOX_HEADS,)),
            stk(od_p, 0, dkv), stk(od_s, 0, dkv), stk(od_p, 1, (IDX_DIM,)), stk(od_s, 1, (IDX_DIM,)))
```

```python
import functools
import math

import jax
import jax.numpy as jnp
from jax import lax
from jax.experimental import pallas as pl
from jax.experimental.pallas import tpu as pltpu

F32 = jnp.float32
BF16 = jnp.bfloat16
NEG_INF = float("-inf")

HEAD_DIM = 64
ROT_DIM = HEAD_DIM // 4
ROPE_THETA = 500000.0
NSA_HEADS = 8
NSA_KV_HEADS = 2
NSA_GROUP = NSA_HEADS // NSA_KV_HEADS
CMP_BLOCK = 32
SLC_BLOCK = 64
SLC_TOPN = 16
WINDOW = 512
FOX_HEADS = 8
DSA_HEADS = 16
DSA_KV_HEADS = 4
DSA_GROUP = DSA_HEADS // DSA_KV_HEADS
IDX_HEADS = 8
IDX_DIM = 64
IDX_TOPK = 256
N_EXPERTS = 8
LN_EPS = 1e-5
PAGE_SIZE = 128
LANES = 128

E_NQ, E_FQ, E_FK, E_FV, E_CMP, E_SLC, E_WIN, E_MISC, E_W = 0, 512, 1024, 1536, 2048, 2304, 2560, 2816, 2944
MISC_GATES = 0
MISC_FF = 24
EVEN_ROPE = (0, 1, 2, 3, E_SLC // LANES, E_WIN // LANES)
O_Q, O_KV, O_QI, O_KI, O_W = 0, 1024, 1536, 2048, 2176
ODD_ROPE = tuple(range(0, 10)) + tuple(range(12, 16))
ODD_ROPE_LO = (O_KI // LANES,)

SEL_W = 64
KEY_CHUNK = 1024
FFN_CHUNK = 256
SEQS_PER_STEP = 2
MOE_BLOCK = 256
MOE_TILE = 896
VMEM_LIMIT = 48 * 1024 * 1024


def _dot(a, b):
    return jnp.dot(a, b, preferred_element_type=F32)


def _dot_nt(a, b):
    return lax.dot_general(a, b, (((1,), (1,)), ((), ())), preferred_element_type=F32)


def _iota(shape, dim):
    return lax.broadcasted_iota(jnp.int32, shape, dim)


def _rope128(v, c, s1, s2):
    return v * c + pltpu.roll(v, 8, 1) * s1 + pltpu.roll(v, LANES - 8, 1) * s2


def _split3(x):
    hi = x.astype(BF16).astype(F32)
    r = x - hi
    mid = r.astype(BF16).astype(F32)
    lo = (r - mid).astype(BF16).astype(F32)
    return hi, mid, lo


def _st_reset(m_ref, l_ref, acc_ref):
    m_ref[...] = jnp.full(m_ref.shape, NEG_INF, F32)
    l_ref[...] = jnp.zeros(l_ref.shape, F32)
    acc_ref[...] = jnp.zeros(acc_ref.shape, F32)


def _st_update(m_ref, l_ref, acc_ref, s, pv):
    m_prev = m_ref[...]
    m_new = jnp.maximum(m_prev, jnp.max(s, axis=-1, keepdims=True))
    m_safe = jnp.where(m_new == NEG_INF, 0.0, m_new)
    alpha = jnp.exp(m_prev - m_safe)
    p = jnp.exp(s - m_safe)
    l_ref[...] = alpha * l_ref[...] + jnp.sum(p, axis=-1, keepdims=True)
    acc_ref[...] = alpha * acc_ref[...] + pv(p.astype(BF16))
    m_ref[...] = m_new


def _st_finish(l_ref, acc_ref):
    return acc_ref[...] / jnp.maximum(l_ref[...], 1e-30)


def _pad_rows(a, rows):
    if a.shape[0] == rows:
        return a
    return jnp.concatenate([a, jnp.zeros((rows - a.shape[0], a.shape[1]), a.dtype)], axis=0)


def _tile_rows(a, times):
    return jnp.concatenate([a] * times, axis=0)


def _layer_norm(z, g, b):
    mu = jnp.mean(z, axis=-1, keepdims=True)
    d = z - mu
    var = jnp.mean(d * d, axis=-1, keepdims=True)
    return d * lax.rsqrt(var + LN_EPS) * g + b


def _key_chunks(total, ch):
    return [(r0, min(ch, total - r0)) for r0 in range(0, total, ch)]


def _proj_kernel(x_ref, w_ref, rt_ref, o_ref, *, width, rope_full, rope_lo, cw):
    xb = x_ref[...].astype(BF16)
    c, s1, s2 = rt_ref[:, 0:128], rt_ref[:, 128:256], rt_ref[:, 256:384]
    lo = _iota((1, LANES), 1) < HEAD_DIM
    for c0 in range(0, width, cw):
        c1 = min(c0 + cw, width)
        acc = _dot(xb, w_ref[:, c0:c1])
        for blk in range(c0 // LANES, c1 // LANES):
            v = acc[:, blk * LANES - c0:(blk + 1) * LANES - c0]
            if blk in rope_full:
                v = _rope128(v, c, s1, s2)
            elif blk in rope_lo:
                v = _rope128(v, jnp.where(lo, c, 1.0), jnp.where(lo, s1, 0.0), jnp.where(lo, s2, 0.0))
            o_ref[:, blk * LANES:(blk + 1) * LANES] = v


def _proj(x, w, rt, *, width, rope_full, rope_lo, tm):
    n, d = x.shape
    return pl.pallas_call(
        functools.partial(_proj_kernel, width=width, rope_full=rope_full, rope_lo=rope_lo, cw=512),
        grid=(n // tm,),
        in_specs=[pl.BlockSpec((tm, d), lambda i: (i, 0)),
                  pl.BlockSpec((d, width), lambda i: (0, 0)),
                  pl.BlockSpec((tm, 3 * LANES), lambda i: (i, 0))],
        out_specs=pl.BlockSpec((tm, width), lambda i: (i, 0)),
        out_shape=jax.ShapeDtypeStruct((n, width), F32),
        compiler_params=pltpu.CompilerParams(dimension_semantics=("arbitrary",), vmem_limit_bytes=VMEM_LIMIT),
        name="proj",
    )(x, w, rt)


def _out_ln_kernel(*refs, n_parts, n_prompt_tiles, alpha):
    op = refs[0:n_parts]
    os_ = refs[n_parts:2 * n_parts]
    ws = refs[2 * n_parts:3 * n_parts]
    x_ref, g_ref, b_ref, o_ref = refs[3 * n_parts:]
    i = pl.program_id(0)

    def compute(parts):
        y = None
        for o, w in zip(parts, ws):
            t = _dot(o[...].astype(BF16), w[...])
            y = t if y is None else y + t
        o_ref[...] = _layer_norm(alpha * x_ref[...] + y, g_ref[...], b_ref[...])

    @pl.when(i < n_prompt_tiles)
    def _():
        compute(op)

    @pl.when(i >= n_prompt_tiles)
    def _():
        compute(os_)


def _out_ln(o_prompt, o_sample, w_parts, x, g, b, *, alpha, tm):
    n, d = x.shape
    npt = o_prompt[0].shape[0] // tm
    nst = o_sample[0].shape[0] // tm
    k = len(w_parts)
    in_specs = []
    for o in o_prompt:
        in_specs.append(pl.BlockSpec((tm, o.shape[1]), lambda i: (jnp.minimum(i, npt - 1), 0)))
    for o in o_sample:
        in_specs.append(pl.BlockSpec((tm, o.shape[1]), lambda i: (jnp.maximum(i - npt, 0), 0)))
    for w in w_parts:
        in_specs.append(pl.BlockSpec(w.shape, lambda i: (0, 0)))
    in_specs += [pl.BlockSpec((tm, d), lambda i: (i, 0)),
                 pl.BlockSpec((1, d), lambda i: (0, 0)),
                 pl.BlockSpec((1, d), lambda i: (0, 0))]
    assert npt + nst == n // tm
    return pl.pallas_call(
        functools.partial(_out_ln_kernel, n_parts=k, n_prompt_tiles=npt, alpha=alpha),
        grid=(n // tm,),
        in_specs=in_specs,
        out_specs=pl.BlockSpec((tm, d), lambda i: (i, 0)),
        out_shape=jax.ShapeDtypeStruct((n, d), F32),
        compiler_params=pltpu.CompilerParams(dimension_semantics=("arbitrary",), vmem_limit_bytes=VMEM_LIMIT),
        name="out_ln",
    )(*o_prompt, *o_sample, *w_parts, x, g, b)


def _ffn_ln_kernel(x_ref, wa_ref, wg_ref, wd_ref, g_ref, b_ref, o_ref, acc_ref, *, alpha, n_chunks):
    c = pl.program_id(1)

    @pl.when(c == 0)
    def _():
        acc_ref[...] = jnp.zeros_like(acc_ref)

    xb = x_ref[...].astype(BF16)
    a = _dot(xb, wa_ref[...])
    gg = _dot(xb, wg_ref[...])
    hmid = (jax.nn.silu(a) * gg).astype(BF16)
    acc_ref[...] += _dot(hmid, wd_ref[...])

    @pl.when(c == n_chunks - 1)
    def _():
        o_ref[...] = _layer_norm(alpha * x_ref[...] + acc_ref[...], g_ref[...], b_ref[...])


def _ffn_ln(x, w_gu, w_down, g, b, *, alpha, tm, fc):
    n, d = x.shape
    dff = w_down.shape[0]
    nch = dff // fc
    return pl.pallas_call(
        functools.partial(_ffn_ln_kernel, alpha=alpha, n_chunks=nch),
        grid=(n // tm, nch),
        in_specs=[pl.BlockSpec((tm, d), lambda i, c: (i, 0)),
                  pl.BlockSpec((d, fc), lambda i, c: (0, c)),
                  pl.BlockSpec((d, fc), lambda i, c: (0, c + nch)),
                  pl.BlockSpec((fc, d), lambda i, c: (c, 0)),
                  pl.BlockSpec((1, d), lambda i, c: (0, 0)),
                  pl.BlockSpec((1, d), lambda i, c: (0, 0))],
        out_specs=pl.BlockSpec((tm, d), lambda i, c: (i, 0)),
        out_shape=jax.ShapeDtypeStruct((n, d), F32),
        scratch_shapes=[pltpu.VMEM((tm, d), F32)],
        compiler_params=pltpu.CompilerParams(dimension_semantics=("arbitrary", "arbitrary"),
                                             vmem_limit_bytes=VMEM_LIMIT),
        name="ffn_ln",
    )(x, w_gu, w_gu, w_down, g, b)


def _moe_ln_kernel(x_ref, wrh_ref, wrl_ref, wat_ref, wgt_ref, wdt_ref, g_ref, b_ref, o_ref,
                   xt_s, acct_s, route_s, routet_s, xct_s, yct_s, *, alpha, n_chunks, n_valid, blk):
    i = pl.program_id(0)
    e = pl.program_id(1)
    c = pl.program_id(2)
    tmm = x_ref.shape[0]
    lane = _iota((1, LANES), 1).astype(F32)
    ef = e.astype(F32)

    @pl.when((e == 0) & (c == 0))
    def _():
        x = x_ref[...]
        xh = x.astype(BF16)
        xl = (x - xh.astype(F32)).astype(BF16)
        logits = _dot(xh, wrh_ref[...]) + _dot(xl, wrh_ref[...]) + _dot(xh, wrl_ref[...])
        lg = jnp.where(lane < N_EXPERTS, logits, NEG_INF)
        m1 = jnp.max(lg, axis=-1, keepdims=True)
        i1 = jnp.min(jnp.where(lg == m1, lane, float(LANES)), axis=-1, keepdims=True)
        lg2 = jnp.where(lane == i1, NEG_INF, lg)
        m2 = jnp.max(lg2, axis=-1, keepdims=True)
        i2 = jnp.min(jnp.where(lg2 == m2, lane, float(LANES)), axis=-1, keepdims=True)
        e2 = jnp.exp(m2 - m1)
        den = 1.0 + e2
        comb = jnp.where(lane == i1, 1.0 / den, 0.0) + jnp.where(lane == i2, e2 / den, 0.0)
        real = (i * tmm + _iota((tmm, 1), 0)) < n_valid
        sel = jnp.where(real & ((lane == i1) | (lane == i2)), 1.0, 0.0)
        before = jnp.where(_iota((tmm, tmm), 0) > _iota((tmm, tmm), 1), 1.0, 0.0).astype(BF16)
        rank = _dot(before, sel.astype(BF16))
        for idx, val in enumerate((comb, sel, rank)):
            route_s[idx] = val
            routet_s[idx] = val.T
        xt_s[...] = x.T.astype(BF16)
        acct_s[...] = jnp.zeros_like(acct_s)

    def pick(ref3, idx):
        return jnp.sum(jnp.where(lane == ef, ref3[idx], 0.0), axis=-1, keepdims=True)

    count = jnp.sum(jnp.where(lane == ef, route_s[1], 0.0))
    n_blocks = (count.astype(jnp.int32) + blk - 1) // blk

    @pl.when(c == 0)
    def _():
        sel_col, rank_col = pick(route_s, 1), pick(route_s, 2)

        def compact(k, carry):
            slot = (k * blk + _iota((1, blk), 1)).astype(F32)
            onehot = jnp.where(rank_col == slot, sel_col, 0.0).astype(BF16)
            xct_s[k] = _dot(xt_s[...], onehot).astype(BF16)
            yct_s[k] = jnp.zeros(yct_s.shape[1:], F32)
            return carry

        lax.fori_loop(0, n_blocks, compact, 0)

    def expert(k, carry):
        xc = xct_s[k]
        a = _dot(wat_ref[...], xc)
        gg = _dot(wgt_ref[...], xc)
        yct_s[k] += _dot(wdt_ref[...], (jax.nn.silu(a) * gg).astype(BF16))
        return carry

    lax.fori_loop(0, n_blocks, expert, 0)

    @pl.when(c == n_chunks - 1)
    def _():
        w_row = routet_s[0, pl.ds(e, 1), :]
        sel_row = routet_s[1, pl.ds(e, 1), :]
        rank_row = routet_s[2, pl.ds(e, 1), :]

        def scatter(k, carry):
            slot = (k * blk + _iota((blk, 1), 0)).astype(F32)
            onehot = jnp.where(rank_row == slot, sel_row, 0.0).astype(BF16)
            y = yct_s[k]
            yh = y.astype(BF16)
            yl = (y - yh.astype(F32)).astype(BF16)
            acct_s[...] += w_row * (_dot(yh, onehot) + _dot(yl, onehot))
            return carry

        lax.fori_loop(0, n_blocks, scatter, 0)

    @pl.when((e == N_EXPERTS - 1) & (c == n_chunks - 1))
    def _():
        o_ref[...] = _layer_norm(alpha * x_ref[...] + acct_s[...].T, g_ref[...], b_ref[...])


def _moe_ln(x, wr_hi, wr_lo, w_gu_t, w_down_t, g, b, *, alpha, tmm, fc, blk):
    n, d = x.shape
    dff = w_down_t.shape[2]
    nch = dff // fc
    n_tiles = -(-n // tmm)
    n_pad = n_tiles * tmm
    xp = jnp.pad(x, ((0, n_pad - n), (0, 0)))
    max_blocks = -(-tmm // blk)
    out = pl.pallas_call(
        functools.partial(_moe_ln_kernel, alpha=alpha, n_chunks=nch, n_valid=n, blk=blk),
        grid=(n_tiles, N_EXPERTS, nch),
        in_specs=[pl.BlockSpec((tmm, d), lambda i, e, c: (i, 0)),
                  pl.BlockSpec((d, LANES), lambda i, e, c: (0, 0)),
                  pl.BlockSpec((d, LANES), lambda i, e, c: (0, 0)),
                  pl.BlockSpec((None, fc, d), lambda i, e, c: (e, c, 0)),
                  pl.BlockSpec((None, fc, d), lambda i, e, c: (e, c + nch, 0)),
                  pl.BlockSpec((None, d, fc), lambda i, e, c: (e, 0, c)),
                  pl.BlockSpec((1, d), lambda i, e, c: (0, 0)),
                  pl.BlockSpec((1, d), lambda i, e, c: (0, 0))],
        out_specs=pl.BlockSpec((tmm, d), lambda i, e, c: (i, 0)),
        out_shape=jax.ShapeDtypeStruct((n_pad, d), F32),
        scratch_shapes=[pltpu.VMEM((d, tmm), BF16), pltpu.VMEM((d, tmm), F32),
                        pltpu.VMEM((3, tmm, LANES), F32), pltpu.VMEM((3, LANES, tmm), F32),
                        pltpu.VMEM((max_blocks, d, blk), BF16), pltpu.VMEM((max_blocks, d, blk), F32)],
        compiler_params=pltpu.CompilerParams(dimension_semantics=("arbitrary", "arbitrary", "arbitrary"),
                                             vmem_limit_bytes=VMEM_LIMIT),
        name="moe_ln",
    )(xp, wr_hi, wr_lo, w_gu_t, w_gu_t, w_down_t, g, b)
    return out[:n]


def _log_sigmoid(z):
    return jnp.minimum(z, 0.0) - jnp.log1p(jnp.exp(-jnp.abs(z)))


def _forget_lanes(x):
    lane = _iota((1, LANES), 1)
    return jnp.where((lane >= MISC_FF) & (lane < MISC_FF + FOX_HEADS), x, 0.0)


def _cum_small(lf):
    rows = lf.shape[0]
    ri = _iota((rows, LANES), 0)
    cum = jnp.zeros((rows, LANES), F32)
    for t in range(rows):
        cum = cum + jnp.where(ri >= t, lf[t:t + 1, :], 0.0)
    return cum


def _cmp_summaries(kc_ref, vc_ref, pe_ref, phi_ref, crt_ref, nc):
    acc_k = jnp.zeros((nc, LANES), F32)
    acc_v = jnp.zeros((nc, LANES), F32)
    for l in range(CMP_BLOCK):
        rk = kc_ref[pl.ds(l, nc, stride=CMP_BLOCK), :] + pe_ref[0, l:l + 1, :]
        acc_k = acc_k + _dot(rk.astype(BF16), phi_ref[0, l])
        rv = vc_ref[pl.ds(l, nc, stride=CMP_BLOCK), :] + pe_ref[1, l:l + 1, :]
        acc_v = acc_v + _dot(rv.astype(BF16), phi_ref[1, l])
    ck = _rope128(acc_k, crt_ref[:, 0:128], crt_ref[:, 128:256], crt_ref[:, 256:384])
    return ck, acc_v


def _nsa_cmp_select(q4, ck_e, ck_o, cv_e, cv_o, qpos, *, tq, nc, ns, topn):
    qpos4 = _tile_rows(qpos, NSA_GROUP)
    blk = _iota((1, SEL_W), 1)
    n_even, n_odd = (nc + 1) // 2, nc // 2
    ok_e = ((2 * blk + 1) * CMP_BLOCK - 1 <= qpos4) & (blk < n_even)
    ok_o = ((2 * blk + 2) * CMP_BLOCK - 1 <= qpos4) & (blk < n_odd)
    s_e = jnp.where(ok_e, _dot_nt(q4, ck_e), NEG_INF)
    s_o = jnp.where(ok_o, _dot_nt(q4, ck_o), NEG_INF)
    m = jnp.maximum(jnp.max(s_e, axis=-1, keepdims=True), jnp.max(s_o, axis=-1, keepdims=True))
    m = jnp.where(m == NEG_INF, 0.0, m)
    p_e = jnp.exp(s_e - m)
    p_o = jnp.exp(s_o - m)
    den = jnp.sum(p_e, axis=-1, keepdims=True) + jnp.sum(p_o, axis=-1, keepdims=True)
    inv = 1.0 / jnp.maximum(den, 1e-30)
    p_e = p_e * inv
    p_o = p_o * inv
    o_cmp = _dot(p_e.astype(BF16), cv_e) + _dot(p_o.astype(BF16), cv_o)
    pp = p_e + p_o
    imp = pp[0:tq]
    for r in range(1, NSA_GROUP):
        imp = imp + pp[r * tq:(r + 1) * tq]
    valid = blk * SLC_BLOCK <= qpos
    forced = (blk == 0) | (blk == (qpos >> 6))
    score = jnp.where(forced, jnp.inf, jnp.where(valid, imp, NEG_INF))
    score = jnp.where(blk < ns, score, NEG_INF)
    rank = jnp.zeros((tq, SEL_W), F32)
    for b2 in range(ns):
        col = score[:, b2:b2 + 1]
        beats = (col > score) | ((col == score) & (blk > b2))
        rank = rank + jnp.where(beats, 1.0, 0.0)
    sel = jnp.where(rank < topn, 1.0, 0.0)
    return o_cmp, sel


def _block_expand(pos0, n):
    return jnp.where(_iota((SEL_W, n), 0) == ((pos0 + _iota((SEL_W, n), 1)) >> 6), 1.0, 0.0).astype(BF16)


def _topk_bias_store(parts, n_keep, reduce_rows, row_shape, bits=1):
    kk = float(n_keep)

    def count(pred):
        total = jnp.zeros(row_shape, F32)
        for load, _, idx, active in parts:
            def one(load=load, idx=idx):
                return reduce_rows(jnp.where(pred(load(), idx), 1.0, 0.0))
            total = total + (one() if active is None else lax.cond(active, one, lambda: jnp.zeros(row_shape, F32)))
        return total

    def descend(step, thr):
        shift = 32 - bits * (step + 1)
        digit = jnp.zeros(row_shape, jnp.int32)
        for d in range(1, 1 << bits):
            cand = thr + (jnp.int32(d) << shift)
            digit = digit + jnp.where(count(lambda k, _, cand=cand: k >= cand) >= kk, 1, 0)
        return thr + (digit << shift)

    thr = lax.fori_loop(0, 32 // bits, descend, jnp.full(row_shape, -2147483648, jnp.int32))
    has_ties = jnp.max(count(lambda k, _: k >= thr)) > kk

    @pl.when(jnp.logical_not(has_ties))
    def _():
        for load, store, _, _ in parts:
            store(jnp.where(load() >= thr, 0.0, NEG_INF))

    @pl.when(has_ties)
    def _():
        need = kk - count(lambda k, _: k > thr)

        def widen(b, bound):
            cand = bound + (jnp.int32(1) << (15 - b))
            return jnp.where(count(lambda k, i: (k == thr) & (i < cand)) <= need, cand, bound)

        bound = lax.fori_loop(0, 16, widen, jnp.zeros(row_shape, jnp.int32))
        for load, store, idx, _ in parts:
            k = load()
            store(jnp.where((k > thr) | ((k == thr) & (idx < bound)), 0.0, NEG_INF))


def _sortable_key(score):
    bits = pltpu.bitcast(score, jnp.int32)
    return bits ^ ((bits >> 31) & jnp.int32(0x7FFFFFFF))


def _cum_tri(lf, carry):
    rows = lf.shape[0]
    tri = jnp.where(_iota((rows, rows), 0) >= _iota((rows, rows), 1), 1.0, 0.0).astype(BF16)
    hi, mid, lo = _split3(lf)
    return _dot(tri, hi.astype(BF16)) + _dot(tri, mid.astype(BF16)) + _dot(tri, lo.astype(BF16)) + carry


def _fox_aug_cols(cum3, hh):
    chi, cmid, clo = [part[:, MISC_FF + hh:MISC_FF + hh + 1] for part in cum3]
    l64 = _iota((1, HEAD_DIM), 1)
    qx = jnp.where(l64 == 0, chi, jnp.where(l64 == 1, cmid, jnp.where(l64 == 2, clo, jnp.where(l64 < 6, 1.0, 0.0))))
    kx = jnp.where(l64 < 3, 1.0,
                   jnp.where(l64 == 3, -chi, jnp.where(l64 == 4, -cmid, jnp.where(l64 == 5, -clo, 0.0))))
    return qx, kx


def _fox_prep_kernel(fq_ref, fk_ref, misc_ref, fb_ref, qa_ref, ka_ref, lf_ref, carry_ref):
    j = pl.program_id(1)

    @pl.when(j == 0)
    def _():
        carry_ref[...] = jnp.zeros_like(carry_ref)

    lf = _forget_lanes(_log_sigmoid(misc_ref[...] + fb_ref[...]))
    lf_ref[...] = lf
    cum = _cum_tri(lf, carry_ref[...])
    rows = lf.shape[0]
    carry_ref[...] = cum[rows - 1:rows, :]
    cum3 = _split3(cum)
    for hh in range(FOX_HEADS):
        qx, kx = _fox_aug_cols(cum3, hh)
        qa_ref[:, hh * 128:hh * 128 + 64] = (fq_ref[:, hh * 64:(hh + 1) * 64] * 0.125).astype(BF16)
        qa_ref[:, hh * 128 + 64:(hh + 1) * 128] = qx.astype(BF16)
        ka_ref[:, hh * 128:hh * 128 + 64] = fk_ref[:, hh * 64:(hh + 1) * 64].astype(BF16)
        ka_ref[:, hh * 128 + 64:(hh + 1) * 128] = kx.astype(BF16)


def _fox_prep(h, fb, *, n_seq, t, rows):
    nr = t // rows
    n = n_seq * t
    return pl.pallas_call(
        _fox_prep_kernel,
        grid=(n_seq, nr),
        in_specs=[pl.BlockSpec((rows, 512), lambda s, j: (s * nr + j, E_FQ // 512)),
                  pl.BlockSpec((rows, 512), lambda s, j: (s * nr + j, E_FK // 512)),
                  pl.BlockSpec((rows, LANES), lambda s, j: (s * nr + j, E_MISC // LANES)),
                  pl.BlockSpec((1, LANES), lambda s, j: (0, 0))],
        out_specs=[pl.BlockSpec((rows, FOX_HEADS * 128), lambda s, j: (s * nr + j, 0)),
                   pl.BlockSpec((rows, FOX_HEADS * 128), lambda s, j: (s * nr + j, 0)),
                   pl.BlockSpec((rows, LANES), lambda s, j: (s * nr + j, 0))],
        out_shape=[jax.ShapeDtypeStruct((n, FOX_HEADS * 128), BF16),
                   jax.ShapeDtypeStruct((n, FOX_HEADS * 128), BF16),
                   jax.ShapeDtypeStruct((n, LANES), F32)],
        scratch_shapes=[pltpu.VMEM((1, LANES), F32)],
        compiler_params=pltpu.CompilerParams(dimension_semantics=("arbitrary", "arbitrary")),
        name="fox_prep",
    )(h, h, h, fb)


def _fox_attn_kernel(qa_ref, ka_ref, v_ref, o_ref, m_s, l_s, acc_s, *, tq, t, hps):
    i = pl.program_id(2)
    qpos = i * tq + _iota((tq, 1), 0)
    q_end = i * tq + tq - 1
    for hh in range(hps):
        _st_reset(m_s.at[hh], l_s.at[hh], acc_s.at[hh])
    for r0, n in _key_chunks(t, KEY_CHUNK):
        @pl.when(r0 <= q_end)
        def _():
            bias = jnp.where((r0 + _iota((1, n), 1)) <= qpos, 0.0, NEG_INF)
            for hh in range(hps):
                k = ka_ref[r0:r0 + n, hh * 128:(hh + 1) * 128]
                v = v_ref[r0:r0 + n, hh * 64:(hh + 1) * 64].astype(BF16)
                s = _dot_nt(qa_ref[:, hh * 128:(hh + 1) * 128], k) + bias
                _st_update(m_s.at[hh], l_s.at[hh], acc_s.at[hh], s, lambda pb, v=v: _dot(pb, v))
    for hh in range(hps):
        o_ref[:, hh * 64:(hh + 1) * 64] = _st_finish(l_s.at[hh], acc_s.at[hh])


def _fox_attn(qa, ka, h, *, n_seq, t, tq, hps):
    nq = t // tq
    n = n_seq * t
    return pl.pallas_call(
        functools.partial(_fox_attn_kernel, tq=tq, t=t, hps=hps),
        grid=(n_seq, FOX_HEADS // hps, nq),
        in_specs=[pl.BlockSpec((tq, hps * 128), lambda s, p, i: (s * nq + i, p)),
                  pl.BlockSpec((t, hps * 128), lambda s, p, i: (s, p)),
                  pl.BlockSpec((t, hps * 64), lambda s, p, i: (s, E_FV // (hps * 64) + p))],
        out_specs=pl.BlockSpec((tq, hps * 64), lambda s, p, i: (s * nq + i, p)),
        out_shape=jax.ShapeDtypeStruct((n, FOX_HEADS * HEAD_DIM), F32),
        scratch_shapes=[pltpu.VMEM((hps, tq, 1), F32), pltpu.VMEM((hps, tq, 1), F32),
                        pltpu.VMEM((hps, tq, HEAD_DIM), F32)],
        compiler_params=pltpu.CompilerParams(dimension_semantics=("arbitrary", "arbitrary", "arbitrary"),
                                             vmem_limit_bytes=VMEM_LIMIT),
        name="fox_attn",
    )(qa, ka, h)


def _nsa_kernel(hq_ref, misc_ref, kc_ref, vc_ref, slc_ref, win_ref, pe_ref, phi_ref, crt_ref, o_ref,
                ck_s, cv_s, m_s, l_s, acc_s, *, tq, t, nc, ns, topn):
    i = pl.program_id(1)

    @pl.when(i == 0)
    def _():
        ck, cv = _cmp_summaries(kc_ref, vc_ref, pe_ref, phi_ref, crt_ref, nc)
        ck_s[...] = jnp.zeros_like(ck_s)
        cv_s[...] = jnp.zeros_like(cv_s)
        ck_s[0:nc, :] = ck
        cv_s[0:nc, :] = cv

    qs = i * tq
    q_end = qs + tq - 1
    qpos = qs + _iota((tq, 1), 0)
    gates = jax.nn.sigmoid(misc_ref[...])
    n_win = min(WINDOW + tq, t)
    w0 = pl.multiple_of(jnp.maximum(qs + tq - n_win, 0), 8)

    groups = range(NSA_KV_HEADS)
    kcols = [slice(g * 64, (g + 1) * 64) for g in groups]
    vcols = [slice(128 + g * 64, 128 + (g + 1) * 64) for g in groups]
    qh = [(hq_ref[:, hh * 64:(hh + 1) * 64] * 0.125).astype(BF16) for hh in range(NSA_HEADS)]
    o_cmps, sels = [], []
    for g in groups:
        q4 = jnp.concatenate(qh[NSA_GROUP * g:NSA_GROUP * (g + 1)], axis=0)
        ck_e = ck_s[pl.ds(0, SEL_W, stride=2), :][:, kcols[g]].astype(BF16)
        ck_o = ck_s[pl.ds(1, SEL_W, stride=2), :][:, kcols[g]].astype(BF16)
        cv_e = cv_s[pl.ds(0, SEL_W, stride=2), :][:, kcols[g]].astype(BF16)
        cv_o = cv_s[pl.ds(1, SEL_W, stride=2), :][:, kcols[g]].astype(BF16)
        o_cmp, sel = _nsa_cmp_select(q4, ck_e, ck_o, cv_e, cv_o, qpos, tq=tq, nc=nc, ns=ns, topn=topn)
        o_cmps.append(o_cmp)
        sels.append(sel.astype(BF16))
    for hh in range(NSA_HEADS):
        _st_reset(m_s.at[hh], l_s.at[hh], acc_s.at[hh])

    for r0, n in _key_chunks(t, KEY_CHUNK):
        @pl.when(r0 <= q_end)
        def _():
            expand = _block_expand(r0, n)
            causal = (r0 + _iota((1, n), 1)) <= qpos
            for g in groups:
                k = slc_ref[r0:r0 + n, kcols[g]].astype(BF16)
                v = slc_ref[r0:r0 + n, vcols[g]].astype(BF16)
                bias = jnp.where((_dot(sels[g], expand) > 0.5) & causal, 0.0, NEG_INF)
                for r in range(NSA_GROUP):
                    hh = NSA_GROUP * g + r
                    _st_update(m_s.at[hh], l_s.at[hh], acc_s.at[hh], _dot_nt(qh[hh], k) + bias,
                               lambda pb, v=v: _dot(pb, v))

    kpos = w0 + _iota((1, n_win), 1)
    wbias = jnp.where((kpos <= qpos) & (kpos > qpos - WINDOW), 0.0, NEG_INF)
    for g in groups:
        k = win_ref[pl.ds(w0, n_win), kcols[g]].astype(BF16)
        v = win_ref[pl.ds(w0, n_win), vcols[g]].astype(BF16)
        for r in range(NSA_GROUP):
            hh = NSA_GROUP * g + r
            o_slc = _st_finish(l_s.at[hh], acc_s.at[hh])
            s = _dot_nt(qh[hh], k) + wbias
            p = jnp.exp(s - jnp.max(s, axis=-1, keepdims=True))
            o_win = _dot(p.astype(BF16), v) / jnp.maximum(jnp.sum(p, axis=-1, keepdims=True), 1e-30)
            c0 = MISC_GATES + 3 * hh
            o_ref[:, hh * 64:(hh + 1) * 64] = (gates[:, c0:c0 + 1] * o_cmps[g][r * tq:(r + 1) * tq]
                                               + gates[:, c0 + 1:c0 + 2] * o_slc + gates[:, c0 + 2:c0 + 3] * o_win)


def _nsa_attn(h, pe, phi, crt, *, n_seq, t, tq, topn):
    nq = t // tq
    n = n_seq * t
    nc = t // CMP_BLOCK
    ns = -(-t // SLC_BLOCK)
    assert ns <= SEL_W and nc <= 2 * SEL_W

    def qblk(col):
        return lambda s, i: (s * nq + i, col)

    def kblk(col):
        return lambda s, i: (s, col)

    return pl.pallas_call(
        functools.partial(_nsa_kernel, tq=tq, t=t, nc=nc, ns=ns, topn=min(topn, ns)),
        grid=(n_seq, nq),
        in_specs=[pl.BlockSpec((tq, 512), qblk(E_NQ // 512)),
                  pl.BlockSpec((tq, LANES), qblk(E_MISC // LANES)),
                  pl.BlockSpec((t, LANES), kblk(E_CMP // LANES)),
                  pl.BlockSpec((t, LANES), kblk(E_CMP // LANES + 1)),
                  pl.BlockSpec((t, 256), kblk(E_SLC // 256)),
                  pl.BlockSpec((t, 256), kblk(E_WIN // 256)),
                  pl.BlockSpec(pe.shape, lambda s, i: (0, 0, 0)),
                  pl.BlockSpec(phi.shape, lambda s, i: (0, 0, 0, 0)),
                  pl.BlockSpec(crt.shape, lambda s, i: (0, 0))],
        out_specs=pl.BlockSpec((tq, NSA_HEADS * HEAD_DIM), lambda s, i: (s * nq + i, 0)),
        out_shape=jax.ShapeDtypeStruct((n, NSA_HEADS * HEAD_DIM), F32),
        scratch_shapes=[pltpu.VMEM((2 * SEL_W, LANES), F32), pltpu.VMEM((2 * SEL_W, LANES), F32),
                        pltpu.VMEM((NSA_HEADS, tq, 1), F32), pltpu.VMEM((NSA_HEADS, tq, 1), F32),
                        pltpu.VMEM((NSA_HEADS, tq, HEAD_DIM), F32)],
        compiler_params=pltpu.CompilerParams(dimension_semantics=("arbitrary", "arbitrary"),
                                             vmem_limit_bytes=VMEM_LIMIT),
        name="nsa_attn",
    )(h, h, h, h, h, h, pe, phi, crt)


def _dsa_kernel(q_ref, qi_ref, kw_ref, kv_ref, ki_ref, o_ref, sc_s, key_s, m_s, l_s, acc_s, *, tq, t, n_keep):
    i = pl.program_id(1)
    qpos = i * tq + _iota((tq, 1), 0)
    q_end = i * tq + tq - 1
    chunks = _key_chunks(t, KEY_CHUNK)

    sc_s[...] = jnp.full(sc_s.shape, NEG_INF, F32)
    wi = kw_ref[:, IDX_DIM:IDX_DIM + IDX_HEADS] * (IDX_HEADS ** -0.5)
    qi = [(qi_ref[:, hh * IDX_DIM:(hh + 1) * IDX_DIM] * (IDX_DIM ** -0.5)).astype(BF16) for hh in range(IDX_HEADS)]
    for r0, n in chunks:
        @pl.when(r0 <= q_end)
        def _():
            ki = ki_ref[r0:r0 + n, 0:IDX_DIM].astype(BF16)
            sc = jnp.zeros((tq, n), F32)
            for hh in range(IDX_HEADS):
                sc = sc + wi[:, hh:hh + 1] * jnp.maximum(_dot_nt(qi[hh], ki), 0.0)
            ok = (r0 + _iota((1, n), 1)) <= qpos
            sc_s[:, r0:r0 + n] = jnp.where(ok, sc + 0.0, NEG_INF)

    key_s[...] = _sortable_key(sc_s[...])

    def key_part(r0, n):
        def load():
            return key_s[:, r0:r0 + n]

        def store(bias):
            sc_s[:, r0:r0 + n] = bias

        return load, store, r0 + _iota((1, n), 1), (None if r0 == 0 else r0 <= q_end)

    _topk_bias_store([key_part(r0, n) for r0, n in chunks], n_keep,
                     lambda ones: jnp.sum(ones, axis=-1, keepdims=True), (tq, 1))

    qs = [(q_ref[:, hh * 64:(hh + 1) * 64] * 0.125).astype(BF16) for hh in range(DSA_HEADS)]
    for hh in range(DSA_HEADS):
        _st_reset(m_s.at[hh], l_s.at[hh], acc_s.at[hh])
    for r0, n in chunks:
        @pl.when(r0 <= q_end)
        def _():
            ok = (r0 + _iota((1, n), 1)) <= qpos
            bias = jnp.where(ok, sc_s[:, r0:r0 + n], NEG_INF)
            for kh in range(DSA_KV_HEADS):
                k = kv_ref[r0:r0 + n, kh * 64:(kh + 1) * 64].astype(BF16)
                v = kv_ref[r0:r0 + n, 256 + kh * 64:256 + (kh + 1) * 64].astype(BF16)
                for r in range(DSA_GROUP):
                    hh = DSA_GROUP * kh + r
                    _st_update(m_s.at[hh], l_s.at[hh], acc_s.at[hh], _dot_nt(qs[hh], k) + bias,
                               lambda pb, v=v: _dot(pb, v))
    for hh in range(DSA_HEADS):
        o_ref[:, hh * 64:(hh + 1) * 64] = _st_finish(l_s.at[hh], acc_s.at[hh])


def _dsa_attn(h, *, n_seq, t, tq, n_keep):
    nq = t // tq
    n = n_seq * t
    assert t % LANES == 0 and t <= 1 << 16 and n_keep <= KEY_CHUNK

    def qblk(w, col):
        return pl.BlockSpec((tq, w), lambda s, i: (s * nq + i, col))

    return pl.pallas_call(
        functools.partial(_dsa_kernel, tq=tq, t=t, n_keep=n_keep),
        grid=(n_seq, nq),
        in_specs=[qblk(1024, O_Q // 1024), qblk(512, O_QI // 512), qblk(LANES, O_KI // LANES),
                  pl.BlockSpec((t, 512), lambda s, i: (s, O_KV // 512)),
                  pl.BlockSpec((t, LANES), lambda s, i: (s, O_KI // LANES))],
        out_specs=pl.BlockSpec((tq, DSA_HEADS * HEAD_DIM), lambda s, i: (s * nq + i, 0)),
        out_shape=jax.ShapeDtypeStruct((n, DSA_HEADS * HEAD_DIM), F32),
        scratch_shapes=[pltpu.VMEM((tq, t), F32), pltpu.VMEM((tq, t), jnp.int32),
                        pltpu.VMEM((DSA_HEADS, tq, 1), F32), pltpu.VMEM((DSA_HEADS, tq, 1), F32),
                        pltpu.VMEM((DSA_HEADS, tq, HEAD_DIM), F32)],
        compiler_params=pltpu.CompilerParams(dimension_semantics=("arbitrary", "arbitrary"),
                                             vmem_limit_bytes=VMEM_LIMIT),
        name="dsa_attn",
    )(h, h, h, h, h)


def _seq_page_specs(block, n_pages, useq, layer):
    tail = (0,) * (len(block) - 2)
    return [pl.BlockSpec(block, lambda s, pt, u=u, k=k: (pt[(s * useq + u) * n_pages + k], layer) + tail)
            for u in range(useq) for k in range(n_pages)]


def _softmax_over(blocks):
    m = None
    for s, _ in blocks:
        mb = jnp.max(s, axis=-1, keepdims=True)
        m = mb if m is None else jnp.maximum(m, mb)
    m = jnp.where(m == NEG_INF, 0.0, m)
    num = den = None
    for s, pv in blocks:
        p = jnp.exp(s - m)
        d = jnp.sum(p, axis=-1, keepdims=True)
        n = pv(p.astype(BF16))
        num, den = (n, d) if num is None else (num + n, den + d)
    return num / jnp.maximum(den, 1e-30)


def _wide(slabs, pick):
    return jnp.concatenate([pick(slab).astype(BF16) for slab in slabs], axis=1)


def _cmp_seq_kernel(pt_ref, *refs, useq, n_pages, nc):
    slabs = refs[:useq * n_pages]
    pe_ref, phi_ref, crt_ref, o_ref, k_s, v_s, c_s = refs[useq * n_pages:]
    for u in range(useq):
        for k in range(n_pages):
            slab = slabs[u * n_pages + k]
            rows = slice(k * PAGE_SIZE, (k + 1) * PAGE_SIZE)
            for g in range(NSA_KV_HEADS):
                k_s[u, rows, g * 64:(g + 1) * 64] = slab[0, g].T
                v_s[u, rows, g * 64:(g + 1) * 64] = slab[1, g].T
        ck, cv = _cmp_summaries(k_s.at[u], v_s.at[u], pe_ref, phi_ref, crt_ref, nc)
        for idx, val in ((0, ck), (2, cv)):
            c_s[u] = jnp.zeros(c_s.shape[1:], F32)
            c_s[u, 0:nc, :] = val
            o_ref[u, idx] = c_s[u, pl.ds(0, SEL_W, stride=2), :]
            o_ref[u, idx + 1] = c_s[u, pl.ds(1, SEL_W, stride=2), :]


def _cmp_seq(cmp_v, pt_flat, pe, phi, crt, *, layer, n_seq, n_pages, useq, nc):
    lp = n_pages * PAGE_SIZE
    in_specs = _seq_page_specs((None, None, 2, NSA_KV_HEADS, HEAD_DIM, PAGE_SIZE), n_pages, useq, layer)
    in_specs += [pl.BlockSpec(pe.shape, lambda s, pt: (0, 0, 0)),
                 pl.BlockSpec(phi.shape, lambda s, pt: (0, 0, 0, 0)),
                 pl.BlockSpec(crt.shape, lambda s, pt: (0, 0))]
    return pl.pallas_call(
        functools.partial(_cmp_seq_kernel, useq=useq, n_pages=n_pages, nc=nc),
        grid_spec=pltpu.PrefetchScalarGridSpec(
            num_scalar_prefetch=1, grid=(n_seq // useq,), in_specs=in_specs,
            out_specs=pl.BlockSpec((useq, 4, SEL_W, LANES), lambda s, pt: (s, 0, 0, 0)),
            scratch_shapes=[pltpu.VMEM((useq, lp, LANES), F32), pltpu.VMEM((useq, lp, LANES), F32),
                            pltpu.VMEM((useq, 2 * SEL_W, LANES), F32)]),
        out_shape=jax.ShapeDtypeStruct((n_seq, 4, SEL_W, LANES), F32),
        compiler_params=pltpu.CompilerParams(dimension_semantics=("arbitrary",), vmem_limit_bytes=VMEM_LIMIT),
        name="cmp_sample",
    )(pt_flat, *([cmp_v] * (useq * n_pages)), pe, phi, crt)


def _nsa_seq_kernel(pt_ref, *refs, useq, n_pages, dt, w_past, nc, ns, topn):
    hq_ref, misc_ref, slc_ref, win_ref, ckv_ref, winp_ref = refs[:6]
    slabs = refs[6:6 + useq * n_pages]
    o_ref = refs[6 + useq * n_pages]
    l_past = n_pages * PAGE_SIZE
    qpos = l_past + _iota((dt, 1), 0)
    new_pos = l_past + _iota((1, LANES), 1)
    new_ok = (new_pos <= qpos) & (new_pos < l_past + dt)
    expand_past = _block_expand(0, l_past)
    expand_new = _block_expand(l_past, LANES)
    wpos = l_past - w_past + _iota((1, w_past), 1)
    win_bias = _tile_rows(jnp.where(wpos > qpos - WINDOW, 0.0, NEG_INF), NSA_GROUP)
    win_new_bias = _tile_rows(jnp.where(new_ok & (new_pos > qpos - WINDOW), 0.0, NEG_INF), NSA_GROUP)
    for u in range(useq):
        rows = slice(u * dt, (u + 1) * dt)
        pages = slabs[u * n_pages:(u + 1) * n_pages]
        gates = jax.nn.sigmoid(misc_ref[rows, :])
        for g in range(NSA_KV_HEADS):
            kcol = slice(g * 64, (g + 1) * 64)
            vcol = slice(128 + g * 64, 128 + (g + 1) * 64)
            q4 = jnp.concatenate([hq_ref[rows, (NSA_GROUP * g + r) * 64:(NSA_GROUP * g + r + 1) * 64]
                                  for r in range(NSA_GROUP)], axis=0)
            q4 = (q4 * 0.125).astype(BF16)
            o_cmp, sel = _nsa_cmp_select(
                q4, ckv_ref[u, 0][:, kcol].astype(BF16), ckv_ref[u, 1][:, kcol].astype(BF16),
                ckv_ref[u, 2][:, kcol].astype(BF16), ckv_ref[u, 3][:, kcol].astype(BF16), qpos,
                tq=dt, nc=nc, ns=ns, topn=topn)
            sel = sel.astype(BF16)
            kt = _wide(pages, lambda slab: slab[0, g])
            vt = _wide(pages, lambda slab: slab[1, g])
            kn = _pad_rows(slc_ref[rows, kcol].astype(BF16), LANES)
            vn = _pad_rows(slc_ref[rows, vcol].astype(BF16), LANES)
            bias = _tile_rows(jnp.where(_dot(sel, expand_past) > 0.5, 0.0, NEG_INF), NSA_GROUP)
            bias_n = _tile_rows(jnp.where((_dot(sel, expand_new) > 0.5) & new_ok, 0.0, NEG_INF), NSA_GROUP)
            o_slc = _softmax_over([(_dot(q4, kt) + bias, lambda pb, vt=vt: _dot_nt(pb, vt)),
                                   (_dot_nt(q4, kn) + bias_n, lambda pb, vn=vn: _dot(pb, vn))])
            ktw = winp_ref[u, 0, g].astype(BF16)
            vtw = winp_ref[u, 1, g].astype(BF16)
            kn = _pad_rows(win_ref[rows, kcol].astype(BF16), LANES)
            vn = _pad_rows(win_ref[rows, vcol].astype(BF16), LANES)
            o_win = _softmax_over([(_dot(q4, ktw) + win_bias, lambda pb, vtw=vtw: _dot_nt(pb, vtw)),
                                   (_dot_nt(q4, kn) + win_new_bias, lambda pb, vn=vn: _dot(pb, vn))])
            for r in range(NSA_GROUP):
                hh = NSA_GROUP * g + r
                rs = slice(r * dt, (r + 1) * dt)
                c0 = MISC_GATES + 3 * hh
                o_ref[rows, hh * 64:(hh + 1) * 64] = (gates[:, c0:c0 + 1] * o_cmp[rs] + gates[:, c0 + 1:c0 + 2] * o_slc[rs]
                                                      + gates[:, c0 + 2:c0 + 3] * o_win[rs])


def _nsa_seq(h, ckv, slc_v, win_v, pt_flat, *, layer, row0, n_seq, dt, n_pages, useq, w_past, topn):
    l_tot = n_pages * PAGE_SIZE + dt
    nc = l_tot // CMP_BLOCK
    ns = -(-l_tot // SLC_BLOCK)
    assert ns <= SEL_W and nc <= 2 * SEL_W and dt <= LANES
    rows = useq * dt
    rb0 = row0 // rows

    def hblk(w, col):
        return pl.BlockSpec((rows, w), lambda s, pt: (rb0 + s, col))

    in_specs = [hblk(512, E_NQ // 512), hblk(LANES, E_MISC // LANES), hblk(256, E_SLC // 256), hblk(256, E_WIN // 256),
                pl.BlockSpec((useq, 4, SEL_W, LANES), lambda s, pt: (s, 0, 0, 0)),
                pl.BlockSpec((useq, None, 2, NSA_KV_HEADS, HEAD_DIM, w_past), lambda s, pt: (s, layer, 0, 0, 0, 0))]
    in_specs += _seq_page_specs((None, None, 2, NSA_KV_HEADS, HEAD_DIM, PAGE_SIZE), n_pages, useq, layer)
    return pl.pallas_call(
        functools.partial(_nsa_seq_kernel, useq=useq, n_pages=n_pages, dt=dt, w_past=w_past, nc=nc, ns=ns,
                          topn=min(topn, ns)),
        grid_spec=pltpu.PrefetchScalarGridSpec(
            num_scalar_prefetch=1, grid=(n_seq // useq,), in_specs=in_specs,
            out_specs=pl.BlockSpec((rows, NSA_HEADS * HEAD_DIM), lambda s, pt: (s, 0))),
        out_shape=jax.ShapeDtypeStruct((n_seq * dt, NSA_HEADS * HEAD_DIM), F32),
        compiler_params=pltpu.CompilerParams(dimension_semantics=("arbitrary",), vmem_limit_bytes=VMEM_LIMIT),
        name="nsa_sample",
    )(pt_flat, h, h, h, h, ckv, win_v, *([slc_v] * (useq * n_pages)))


def _fox_seq_kernel(pt_ref, *refs, useq, n_pages, dt):
    fq_ref, fk_ref, fv_ref, misc_ref, fb_ref = refs[:5]
    kv_slabs = refs[5:5 + useq * n_pages]
    lf_slabs = refs[5 + useq * n_pages:5 + 2 * useq * n_pages]
    o_ref, lf_ref, qbd_s = refs[5 + 2 * useq * n_pages:]
    width = FOX_HEADS * HEAD_DIM
    l_past = n_pages * PAGE_SIZE

    def head_rows(x):
        return jnp.concatenate([jnp.broadcast_to(x[hh:hh + 1, :], (dt, x.shape[1])) for hh in range(FOX_HEADS)], axis=0)

    upper = jnp.where(_iota((LANES, LANES), 0) <= _iota((LANES, LANES), 1), 1.0, 0.0).astype(BF16)
    lane = _iota((dt, LANES), 1)
    pick = jnp.concatenate([jnp.where(lane == MISC_FF + hh, 1.0, 0.0) for hh in range(FOX_HEADS)], axis=0).astype(BF16)
    col = _iota((1, LANES), 1)
    new_ok = (col <= _tile_rows(_iota((dt, 1), 0), FOX_HEADS)) & (col < dt)
    for u in range(useq):
        rows = slice(u * dt, (u + 1) * dt)
        qbd_s[u] = jnp.zeros(qbd_s.shape[1:], BF16)
        for hh in range(FOX_HEADS):
            cs = slice(hh * 64, (hh + 1) * 64)
            qbd_s[u, hh * dt:(hh + 1) * dt, cs] = (fq_ref[rows, cs] * 0.125).astype(BF16)
        qbd = qbd_s[u]
        offset = jnp.zeros((FOX_HEADS, LANES), F32)
        cums = []
        for k in range(n_pages):
            hi, mid, lo = _split3(lf_slabs[u * n_pages + k][...])
            local = _dot(hi.astype(BF16), upper) + _dot(mid.astype(BF16), upper) + _dot(lo.astype(BF16), upper)
            cums.append(local + offset)
            offset = jnp.broadcast_to(cums[-1][:, LANES - 1:LANES], (FOX_HEADS, LANES))
        pages = kv_slabs[u * n_pages:(u + 1) * n_pages]
        kt = _wide(pages, lambda slab: slab[0].reshape(width, PAGE_SIZE))
        vt = _wide(pages, lambda slab: slab[1].reshape(width, PAGE_SIZE))
        s_past = _dot(qbd, kt) - head_rows(jnp.concatenate(cums, axis=1))
        lf = _forget_lanes(_log_sigmoid(misc_ref[rows, :] + fb_ref[...]))
        lf_ref[rows, :] = lf
        parts = _split3(_pad_rows(_cum_small(lf), LANES))
        cum_new = sum(_dot_nt(pick, part.astype(BF16)) for part in parts)
        kn = _pad_rows(fk_ref[rows, :].astype(BF16), LANES)
        vn = _pad_rows(fv_ref[rows, :].astype(BF16), LANES)
        s_new = jnp.where(new_ok, _dot_nt(qbd, kn) - (head_rows(offset) + cum_new), NEG_INF)
        o = _softmax_over([(s_past, lambda pb, vt=vt: _dot_nt(pb, vt)), (s_new, lambda pb, vn=vn: _dot(pb, vn))])
        for hh in range(FOX_HEADS):
            o_ref[rows, hh * 64:(hh + 1) * 64] = o[hh * dt:(hh + 1) * dt, hh * 64:(hh + 1) * 64]


def _fox_seq(h, fb, fox_v, flf_v, pt_flat, *, layer, row0, n_seq, dt, n_pages, useq):
    rows = useq * dt
    rb0 = row0 // rows
    width = FOX_HEADS * HEAD_DIM
    assert dt <= LANES

    def hblk(w, col):
        return pl.BlockSpec((rows, w), lambda s, pt: (rb0 + s, col))

    in_specs = [hblk(512, E_FQ // 512), hblk(512, E_FK // 512), hblk(512, E_FV // 512), hblk(LANES, E_MISC // LANES),
                pl.BlockSpec((1, LANES), lambda s, pt: (0, 0))]
    in_specs += _seq_page_specs((None, None, 2, FOX_HEADS, HEAD_DIM, PAGE_SIZE), n_pages, useq, layer)
    in_specs += _seq_page_specs((None, None, FOX_HEADS, PAGE_SIZE), n_pages, useq, layer)
    return pl.pallas_call(
        functools.partial(_fox_seq_kernel, useq=useq, n_pages=n_pages, dt=dt),
        grid_spec=pltpu.PrefetchScalarGridSpec(
            num_scalar_prefetch=1, grid=(n_seq // useq,), in_specs=in_specs,
            out_specs=[pl.BlockSpec((rows, width), lambda s, pt: (s, 0)),
                       pl.BlockSpec((rows, LANES), lambda s, pt: (s, 0))],
            scratch_shapes=[pltpu.VMEM((useq, FOX_HEADS * dt, width), BF16)]),
        out_shape=[jax.ShapeDtypeStruct((n_seq * dt, width), F32), jax.ShapeDtypeStruct((n_seq * dt, LANES), F32)],
        compiler_params=pltpu.CompilerParams(dimension_semantics=("arbitrary",), vmem_limit_bytes=VMEM_LIMIT),
        name="fox_sample",
    )(pt_flat, h, h, h, h, fb, *([fox_v] * (useq * n_pages)), *([flf_v] * (useq * n_pages)))


def _dsa_seq_kernel(pt_ref, *refs, useq, n_pages, dt, n_keep):
    q_ref, qi_ref, kw_ref, kv_ref = refs[:4]
    idx_slabs = refs[4:4 + useq * n_pages]
    kv_slabs = refs[4 + useq * n_pages:4 + 2 * useq * n_pages]
    o_ref, sc_s, key_s, qbd_s = refs[4 + 2 * useq * n_pages:]
    width = DSA_KV_HEADS * HEAD_DIM
    col = _iota((1, LANES), 1)
    new_ok = (col <= _iota((dt, 1), 0)) & (col < dt)

    for u in range(useq):
        rows = slice(u * dt, (u + 1) * dt)
        wi = kw_ref[rows, IDX_DIM:IDX_DIM + IDX_HEADS] * (IDX_HEADS ** -0.5)
        qst = jnp.concatenate([qi_ref[rows, hh * IDX_DIM:(hh + 1) * IDX_DIM] for hh in range(IDX_HEADS)], axis=0)
        qst = (qst * (IDX_DIM ** -0.5)).astype(BF16)

        def scores(lg, wi=wi):
            sc = jnp.zeros((dt, lg.shape[1]), F32)
            for hh in range(IDX_HEADS):
                sc = sc + wi[:, hh:hh + 1] * jnp.maximum(lg[hh * dt:(hh + 1) * dt], 0.0)
            return sc + 0.0

        wide = scores(_dot(qst, _wide(idx_slabs[u * n_pages:(u + 1) * n_pages], lambda slab: slab[...])))
        for k in range(n_pages):
            sc_s[k, rows, :] = wide[:, k * PAGE_SIZE:(k + 1) * PAGE_SIZE]
        kin = _pad_rows(kw_ref[rows, 0:IDX_DIM].astype(BF16), LANES)
        sc_s[n_pages, rows, :] = jnp.where(new_ok, scores(_dot_nt(qst, kin)), NEG_INF)
    key_s[...] = _sortable_key(sc_s[...])
    shape = (n_pages + 1, useq * dt, LANES)

    def store(bias):
        sc_s[...] = bias

    _topk_bias_store([(lambda: key_s[...], store, _iota(shape, 0) * LANES + _iota(shape, 2), None)], n_keep,
                     lambda ones: jnp.sum(jnp.sum(ones, axis=0), axis=-1, keepdims=True)[None],
                     (1, useq * dt, 1), bits=4)

    for u in range(useq):
        rows = slice(u * dt, (u + 1) * dt)
        qbd_s[u] = jnp.zeros(qbd_s.shape[1:], BF16)
        for hh in range(DSA_HEADS):
            kh = hh // DSA_GROUP
            qbd_s[u, hh * dt:(hh + 1) * dt, kh * 64:(kh + 1) * 64] = (q_ref[rows, hh * 64:(hh + 1) * 64] * 0.125).astype(BF16)
        qbd = qbd_s[u]
        pages = kv_slabs[u * n_pages:(u + 1) * n_pages]
        kt = _wide(pages, lambda slab: slab[0].reshape(width, PAGE_SIZE))
        vt = _wide(pages, lambda slab: slab[1].reshape(width, PAGE_SIZE))
        bias = jnp.concatenate([sc_s[k, rows, :] for k in range(n_pages)], axis=1)
        kn = _pad_rows(kv_ref[rows, 0:width].astype(BF16), LANES)
        vn = _pad_rows(kv_ref[rows, width:2 * width].astype(BF16), LANES)
        o = _softmax_over([(_dot(qbd, kt) + _tile_rows(bias, DSA_HEADS), lambda pb, vt=vt: _dot_nt(pb, vt)),
                           (_dot_nt(qbd, kn) + _tile_rows(sc_s[n_pages, rows, :], DSA_HEADS),
                            lambda pb, vn=vn: _dot(pb, vn))])
        for hh in range(DSA_HEADS):
            kh = hh // DSA_GROUP
            o_ref[rows, hh * 64:(hh + 1) * 64] = o[hh * dt:(hh + 1) * dt, kh * 64:(kh + 1) * 64]


def _dsa_seq(h, dik_v, dkv_v, pt_flat, *, layer, row0, n_seq, dt, n_pages, useq, n_keep):
    rows = useq * dt
    rb0 = row0 // rows
    width = DSA_KV_HEADS * HEAD_DIM
    shape = (n_pages + 1, rows, LANES)

    def hblk(w, col):
        return pl.BlockSpec((rows, w), lambda s, pt: (rb0 + s, col))

    in_specs = [hblk(1024, O_Q // 1024), hblk(512, O_QI // 512), hblk(LANES, O_KI // LANES), hblk(512, O_KV // 512)]
    in_specs += _seq_page_specs((None, None, IDX_DIM, PAGE_SIZE), n_pages, useq, layer)
    in_specs += _seq_page_specs((None, None, 2, DSA_KV_HEADS, HEAD_DIM, PAGE_SIZE), n_pages, useq, layer)
    return pl.pallas_call(
        functools.partial(_dsa_seq_kernel, useq=useq, n_pages=n_pages, dt=dt, n_keep=n_keep),
        grid_spec=pltpu.PrefetchScalarGridSpec(
            num_scalar_prefetch=1, grid=(n_seq // useq,), in_specs=in_specs,
            out_specs=pl.BlockSpec((rows, DSA_HEADS * HEAD_DIM), lambda s, pt: (s, 0)),
            scratch_shapes=[pltpu.VMEM(shape, F32), pltpu.VMEM(shape, jnp.int32),
                            pltpu.VMEM((useq, DSA_HEADS * dt, width), BF16)]),
        out_shape=jax.ShapeDtypeStruct((n_seq * dt, DSA_HEADS * HEAD_DIM), F32),
        compiler_params=pltpu.CompilerParams(dimension_semantics=("arbitrary",), vmem_limit_bytes=VMEM_LIMIT),
        name="dsa_sample",
    )(pt_flat, h, h, h, h, *([dik_v] * (useq * n_pages)), *([dkv_v] * (useq * n_pages)))


def _rope_table(pos):
    half = ROT_DIM // 2
    inv = ROPE_THETA ** (-2.0 * jnp.arange(half, dtype=F32) / ROT_DIM)
    ang = pos.astype(F32)[:, None] * inv[None, :]
    cos, sin = jnp.cos(ang), jnp.sin(ang)
    n = pos.shape[0]
    one = jnp.ones((n, HEAD_DIM - ROT_DIM), F32)
    zero = jnp.zeros((n, HEAD_DIM - ROT_DIM), F32)
    z8 = jnp.zeros((n, half), F32)
    c = jnp.concatenate([cos, cos, one], axis=1)
    s1 = jnp.concatenate([z8, sin, zero], axis=1)
    s2 = jnp.concatenate([-sin, z8, zero], axis=1)
    return jnp.concatenate([c, c, s1, s1, s2, s2], axis=1)


def _largest_tile(limit, *sizes):
    t = limit
    while any(s % t for s in sizes):
        t //= 2
    return t


def _rows_last(pool):
    nd = pool.ndim
    return jnp.transpose(pool, (0, 1) + tuple(range(3, nd)) + (2,))


def kernel(x_prompt, x_sample, cache_nsa_cmp_kv, cache_nsa_slc_kv, state_nsa_win_kv, cache_fox_kv, cache_fox_logf, cache_dsa_kv, cache_dsa_idx_k, page_table, ln_g, ln_b, w_in_even, w_out_even, fox_f_bias, nsa_cmp_pos, nsa_cmp_phi, w_in_odd, w_out_odd, ffn_gu, ffn_down, moe_router, moe_gu, moe_down):
    nb, t, d = x_prompt.shape
    db, dt, _ = x_sample.shape
    n_pages = page_table.shape[1]
    lp = n_pages * PAGE_SIZE
    w_past = state_nsa_win_kv.shape[2]
    n_p, n_s = nb * t, db * dt
    depth = ln_g.shape[0]
    alpha = (2 * depth) ** 0.25
    tm = _largest_tile(512, n_p, n_s)
    tm_ff = _largest_tile(1024, n_p, n_s)
    tq = _largest_tile(256, t)
    useq = math.gcd(db, SEQS_PER_STEP)
    pt_flat = page_table.reshape(-1).astype(jnp.int32)

    cmp_v, slc_v, win_v = _rows_last(cache_nsa_cmp_kv), _rows_last(cache_nsa_slc_kv), _rows_last(state_nsa_win_kv)
    fox_v, flf_v = _rows_last(cache_fox_kv), _rows_last(cache_fox_logf)
    dkv_v, dik_v = _rows_last(cache_dsa_kv), _rows_last(cache_dsa_idx_k)

    x = jnp.concatenate([x_prompt.reshape(n_p, d), x_sample.reshape(n_s, d)], axis=0)
    pos = jnp.concatenate([jnp.tile(jnp.arange(t), nb), jnp.tile(lp + jnp.arange(dt), db)])
    rt = _rope_table(pos)
    nc_p, nc_s = t // CMP_BLOCK, (lp + dt) // CMP_BLOCK
    assert lp % CMP_BLOCK == 0 and dt < CMP_BLOCK
    crt_p = _rope_table((jnp.arange(nc_p) + 1) * CMP_BLOCK - 1)
    crt_s = _rope_table((jnp.arange(nc_s) + 1) * CMP_BLOCK - 1)

    ev_p, ev_s, od_p, od_s = [], [], [], []
    for layer in range(depth):
        j = layer // 2
        g0, b0 = ln_g[layer, 0][None, :], ln_b[layer, 0][None, :]
        g1, b1 = ln_g[layer, 1][None, :], ln_b[layer, 1][None, :]
        if layer % 2 == 0:
            w = w_in_even[j]
            w = jnp.concatenate([w[:, 0:512], w[:, 1304:2840], w[:, 512:1280], w[:, 1280:1304], w[:, 2840:2848],
                                 jnp.zeros((d, E_W - 2848), w.dtype)], axis=1).astype(BF16)
            h = _proj(x, w, rt, width=E_W, rope_full=EVEN_ROPE, rope_lo=(), tm=tm)

            pe = jnp.concatenate([nsa_cmp_pos[j], nsa_cmp_pos[j]], axis=-1)
            ph = nsa_cmp_phi[j]
            zz = jnp.zeros_like(ph)
            phi = jnp.concatenate([jnp.concatenate([ph, zz], axis=-1),
                                   jnp.concatenate([zz, ph], axis=-1)], axis=-2).astype(BF16)
            fb = jnp.zeros((1, LANES), F32).at[0, MISC_FF:MISC_FF + FOX_HEADS].set(fox_f_bias[j])

            qa_p, ka_p, lf_p = _fox_prep(h, fb, n_seq=nb, t=t, rows=_largest_tile(256, t))
            o_nsa_p = _nsa_attn(h, pe, phi, crt_p, n_seq=nb, t=t, tq=tq, topn=SLC_TOPN)
            o_fox_p = _fox_attn(qa_p, ka_p, h, n_seq=nb, t=t, tq=tq, hps=FOX_HEADS)

            ckv = _cmp_seq(cmp_v, pt_flat, pe, phi, crt_s, layer=j, n_seq=db, n_pages=n_pages, useq=useq, nc=nc_s)
            o_nsa_s = _nsa_seq(h, ckv, slc_v, win_v, pt_flat, layer=j, row0=n_p, n_seq=db, dt=dt, n_pages=n_pages,
                               useq=useq, w_past=w_past, topn=SLC_TOPN)
            o_fox_s, lf_s = _fox_seq(h, fb, fox_v, flf_v, pt_flat, layer=j, row0=n_p, n_seq=db, dt=dt,
                                     n_pages=n_pages, useq=useq)

            wo = w_out_even[j].astype(BF16)
            x = _out_ln([o_nsa_p, o_fox_p], [o_nsa_s, o_fox_s], [wo[0:512], wo[512:1024]], x, g0, b0,
                        alpha=alpha, tm=tm)
            x = _ffn_ln(x, ffn_gu[j].astype(BF16), ffn_down[j].astype(BF16), g1, b1, alpha=alpha, tm=tm_ff,
                        fc=FFN_CHUNK)

            hp, hs = h[:n_p].reshape(nb, t, E_W), h[n_p:].reshape(db, dt, E_W)
            kv2 = (2, NSA_KV_HEADS, HEAD_DIM)
            win_new = hs[:, :, E_WIN:E_WIN + 256].reshape((db, dt) + kv2)
            keep_p, keep_s = min(WINDOW, t), min(WINDOW, w_past + dt)
            win_s = jnp.concatenate([state_nsa_win_kv[:, j, w_past + dt - keep_s:], win_new], axis=1)
            lf_p3 = lf_p[:, MISC_FF:MISC_FF + FOX_HEADS].reshape(nb, t, FOX_HEADS)
            lf_s3 = lf_s[:, MISC_FF:MISC_FF + FOX_HEADS].reshape(db, dt, FOX_HEADS)
            ev_p.append((hp[:, :, E_CMP:E_CMP + 256], hp[:, :, E_SLC:E_SLC + 256],
                         hp[:, t - keep_p:, E_WIN:E_WIN + 256], hp[:, :, E_FK:E_FK + 1024], lf_p3))
            ev_s.append((hs[:, :, E_CMP:E_CMP + 256], hs[:, :, E_SLC:E_SLC + 256],
                         win_s, hs[:, :, E_FK:E_FK + 1024], lf_s3))
        else:
            w = jnp.concatenate([w_in_odd[j], jnp.zeros((d, O_W - w_in_odd.shape[2]), w_in_odd.dtype)],
                                axis=1).astype(BF16)
            h = _proj(x, w, rt, width=O_W, rope_full=ODD_ROPE, rope_lo=ODD_ROPE_LO, tm=tm)
            o_p = _dsa_attn(h, n_seq=nb, t=t, tq=tq, n_keep=min(IDX_TOPK, t // 4))
            o_s = _dsa_seq(h, dik_v, dkv_v, pt_flat, layer=j, row0=n_p, n_seq=db, dt=dt, n_pages=n_pages,
                           useq=useq, n_keep=min(IDX_TOPK, (lp + dt) // 4))
            x = _out_ln([o_p], [o_s], [w_out_odd[j].astype(BF16)], x, g0, b0, alpha=alpha, tm=tm)
            wr = jnp.concatenate([moe_router[j], jnp.zeros((d, LANES - N_EXPERTS), F32)], axis=1)
            wr_hi = wr.astype(BF16)
            wr_lo = (wr - wr_hi.astype(F32)).astype(BF16)
            x = _moe_ln(x, wr_hi, wr_lo, jnp.transpose(moe_gu[j], (0, 2, 1)).astype(BF16),
                        jnp.transpose(moe_down[j], (0, 2, 1)).astype(BF16), g1, b1,
                        alpha=alpha, tmm=MOE_TILE if n_p + n_s >= 8 * MOE_TILE else MOE_BLOCK,
                        fc=moe_down.shape[2] // 4, blk=MOE_BLOCK)
            hp, hs = h[:n_p].reshape(nb, t, O_W), h[n_p:].reshape(db, dt, O_W)
            od_p.append((hp[:, :, O_KV:O_KV + 512], hp[:, :, O_KI:O_KI + IDX_DIM]))
            od_s.append((hs[:, :, O_KV:O_KV + 512], hs[:, :, O_KI:O_KI + IDX_DIM]))

    def stk(lst, idx, tail):
        a = jnp.stack([s[idx] for s in lst], axis=1)
        return a.reshape(a.shape[:3] + tail)

    kv2 = (2, NSA_KV_HEADS, HEAD_DIM)
    fkv = (2, FOX_HEADS, HEAD_DIM)
    dkv = (2, DSA_KV_HEADS, HEAD_DIM)
    return (x[:n_p].reshape(nb, t, d), x[n_p:].reshape(db, dt, d),
            stk(ev_p, 0, kv2), stk(ev_s, 0, kv2), stk(ev_p, 1, kv2), stk(ev_s, 1, kv2),
            stk(ev_p, 2, kv2), stk(ev_s, 2, kv2), stk(ev_p, 3, fkv), stk(ev_s, 3, fkv),
            stk(ev_p, 4, (FOX_HEADS,)), stk(ev_s, 4, (FOX_HEADS,)),
            stk(od_p, 0, dkv), stk(od_s, 0, dkv), stk(od_p, 1, (IDX_DIM,)), stk(od_s, 1, (IDX_DIM,)))
```

```python
import functools
import math

import jax
import jax.numpy as jnp
from jax import lax
from jax.experimental import pallas as pl
from jax.experimental.pallas import tpu as pltpu

F32 = jnp.float32
BF16 = jnp.bfloat16
NEG_INF = float("-inf")

HEAD_DIM = 64
ROT_DIM = HEAD_DIM // 4
ROPE_THETA = 500000.0
NSA_HEADS = 8
NSA_KV_HEADS = 2
NSA_GROUP = NSA_HEADS // NSA_KV_HEADS
CMP_BLOCK = 32
SLC_BLOCK = 64
SLC_TOPN = 16
WINDOW = 512
FOX_HEADS = 8
DSA_HEADS = 16
DSA_KV_HEADS = 4
DSA_GROUP = DSA_HEADS // DSA_KV_HEADS
IDX_HEADS = 8
IDX_DIM = 64
IDX_TOPK = 256
N_EXPERTS = 8
LN_EPS = 1e-5
PAGE_SIZE = 128
LANES = 128

E_NQ, E_FQ, E_FK, E_FV, E_CMP, E_SLC, E_WIN, E_MISC, E_W = 0, 512, 1024, 1536, 2048, 2304, 2560, 2816, 2944
MISC_GATES = 0
MISC_FF = 24
EVEN_ROPE = (0, 1, 2, 3, E_SLC // LANES, E_WIN // LANES)
O_Q, O_KV, O_QI, O_KI, O_W = 0, 1024, 1536, 2048, 2176
ODD_ROPE = tuple(range(0, 10)) + tuple(range(12, 16))
ODD_ROPE_LO = (O_KI // LANES,)

SEL_W = 64
KEY_CHUNK = 1024
SEQS_PER_STEP = 2
MOE_BLOCK = 256
MOE_TILE = 896
VMEM_LIMIT = 48 * 1024 * 1024


def _dot(a, b):
    return jnp.dot(a, b, preferred_element_type=F32)


def _dot_nt(a, b):
    return lax.dot_general(a, b, (((1,), (1,)), ((), ())), preferred_element_type=F32)


def _iota(shape, dim):
    return lax.broadcasted_iota(jnp.int32, shape, dim)


def _rope128(v, c, s1, s2):
    return v * c + pltpu.roll(v, 8, 1) * s1 + pltpu.roll(v, LANES - 8, 1) * s2


def _split3(x):
    hi = x.astype(BF16).astype(F32)
    r = x - hi
    mid = r.astype(BF16).astype(F32)
    lo = (r - mid).astype(BF16).astype(F32)
    return hi, mid, lo


def _st_reset(m_ref, l_ref, acc_ref):
    m_ref[...] = jnp.full(m_ref.shape, NEG_INF, F32)
    l_ref[...] = jnp.zeros(l_ref.shape, F32)
    acc_ref[...] = jnp.zeros(acc_ref.shape, F32)


def _st_update(m_ref, l_ref, acc_ref, s, pv):
    m_prev = m_ref[...]
    m_new = jnp.maximum(m_prev, jnp.max(s, axis=-1, keepdims=True))
    m_safe = jnp.where(m_new == NEG_INF, 0.0, m_new)
    alpha = jnp.exp(m_prev - m_safe)
    p = jnp.exp(s - m_safe)
    l_ref[...] = alpha * l_ref[...] + jnp.sum(p, axis=-1, keepdims=True)
    acc_ref[...] = alpha * acc_ref[...] + pv(p.astype(BF16))
    m_ref[...] = m_new


def _st_finish(l_ref, acc_ref):
    return acc_ref[...] / jnp.maximum(l_ref[...], 1e-30)


def _pad_rows(a, rows):
    if a.shape[0] == rows:
        return a
    return jnp.concatenate([a, jnp.zeros((rows - a.shape[0], a.shape[1]), a.dtype)], axis=0)


def _tile_rows(a, times):
    return jnp.concatenate([a] * times, axis=0)


def _layer_norm(z, g, b):
    mu = jnp.mean(z, axis=-1, keepdims=True)
    d = z - mu
    var = jnp.mean(d * d, axis=-1, keepdims=True)
    return d * lax.rsqrt(var + LN_EPS) * g + b


def _key_chunks(total, ch):
    return [(r0, min(ch, total - r0)) for r0 in range(0, total, ch)]


def _proj_kernel(x_ref, w_ref, rt_ref, o_ref, *, width, rope_full, rope_lo, cw):
    xb = x_ref[...].astype(BF16)
    c, s1, s2 = rt_ref[:, 0:128], rt_ref[:, 128:256], rt_ref[:, 256:384]
    lo = _iota((1, LANES), 1) < HEAD_DIM
    for c0 in range(0, width, cw):
        c1 = min(c0 + cw, width)
        acc = _dot(xb, w_ref[:, c0:c1])
        for blk in range(c0 // LANES, c1 // LANES):
            v = acc[:, blk * LANES - c0:(blk + 1) * LANES - c0]
            if blk in rope_full:
                v = _rope128(v, c, s1, s2)
            elif blk in rope_lo:
                v = _rope128(v, jnp.where(lo, c, 1.0), jnp.where(lo, s1, 0.0), jnp.where(lo, s2, 0.0))
            o_ref[:, blk * LANES:(blk + 1) * LANES] = v


def _proj(x, w, rt, *, width, rope_full, rope_lo, tm):
    n, d = x.shape
    return pl.pallas_call(
        functools.partial(_proj_kernel, width=width, rope_full=rope_full, rope_lo=rope_lo, cw=512),
        grid=(n // tm,),
        in_specs=[pl.BlockSpec((tm, d), lambda i: (i, 0)),
                  pl.BlockSpec((d, width), lambda i: (0, 0)),
                  pl.BlockSpec((tm, 3 * LANES), lambda i: (i, 0))],
        out_specs=pl.BlockSpec((tm, width), lambda i: (i, 0)),
        out_shape=jax.ShapeDtypeStruct((n, width), F32),
        compiler_params=pltpu.CompilerParams(dimension_semantics=("arbitrary",), vmem_limit_bytes=VMEM_LIMIT),
        name="proj",
    )(x, w, rt)


def _out_ln_kernel(*refs, n_parts, n_prompt_tiles, alpha):
    op = refs[0:n_parts]
    os_ = refs[n_parts:2 * n_parts]
    ws = refs[2 * n_parts:3 * n_parts]
    x_ref, g_ref, b_ref, o_ref = refs[3 * n_parts:]
    i = pl.program_id(0)

    def compute(parts):
        y = None
        for o, w in zip(parts, ws):
            t = _dot(o[...].astype(BF16), w[...])
            y = t if y is None else y + t
        o_ref[...] = _layer_norm(alpha * x_ref[...] + y, g_ref[...], b_ref[...])

    @pl.when(i < n_prompt_tiles)
    def _():
        compute(op)

    @pl.when(i >= n_prompt_tiles)
    def _():
        compute(os_)


def _out_ln(o_prompt, o_sample, w_parts, x, g, b, *, alpha, tm):
    n, d = x.shape
    npt = o_prompt[0].shape[0] // tm
    nst = o_sample[0].shape[0] // tm
    k = len(w_parts)
    in_specs = []
    for o in o_prompt:
        in_specs.append(pl.BlockSpec((tm, o.shape[1]), lambda i: (jnp.minimum(i, npt - 1), 0)))
    for o in o_sample:
        in_specs.append(pl.BlockSpec((tm, o.shape[1]), lambda i: (jnp.maximum(i - npt, 0), 0)))
    for w in w_parts:
        in_specs.append(pl.BlockSpec(w.shape, lambda i: (0, 0)))
    in_specs += [pl.BlockSpec((tm, d), lambda i: (i, 0)),
                 pl.BlockSpec((1, d), lambda i: (0, 0)),
                 pl.BlockSpec((1, d), lambda i: (0, 0))]
    assert npt + nst == n // tm
    return pl.pallas_call(
        functools.partial(_out_ln_kernel, n_parts=k, n_prompt_tiles=npt, alpha=alpha),
        grid=(n // tm,),
        in_specs=in_specs,
        out_specs=pl.BlockSpec((tm, d), lambda i: (i, 0)),
        out_shape=jax.ShapeDtypeStruct((n, d), F32),
        compiler_params=pltpu.CompilerParams(dimension_semantics=("arbitrary",), vmem_limit_bytes=VMEM_LIMIT),
        name="out_ln",
    )(*o_prompt, *o_sample, *w_parts, x, g, b)


def _ffn_ln_kernel(x_ref, wa_ref, wg_ref, wd_ref, g_ref, b_ref, o_ref, acc_ref, *, alpha, n_chunks):
    c = pl.program_id(1)

    @pl.when(c == 0)
    def _():
        acc_ref[...] = jnp.zeros_like(acc_ref)

    xb = x_ref[...].astype(BF16)
    a = _dot(xb, wa_ref[...])
    gg = _dot(xb, wg_ref[...])
    hmid = (jax.nn.silu(a) * gg).astype(BF16)
    acc_ref[...] += _dot(hmid, wd_ref[...])

    @pl.when(c == n_chunks - 1)
    def _():
        o_ref[...] = _layer_norm(alpha * x_ref[...] + acc_ref[...], g_ref[...], b_ref[...])


def _ffn_ln(x, w_gu, w_down, g, b, *, alpha, tm, fc):
    n, d = x.shape
    dff = w_down.shape[0]
    nch = dff // fc
    return pl.pallas_call(
        functools.partial(_ffn_ln_kernel, alpha=alpha, n_chunks=nch),
        grid=(n // tm, nch),
        in_specs=[pl.BlockSpec((tm, d), lambda i, c: (i, 0)),
                  pl.BlockSpec((d, fc), lambda i, c: (0, c)),
                  pl.BlockSpec((d, fc), lambda i, c: (0, c + nch)),
                  pl.BlockSpec((fc, d), lambda i, c: (c, 0)),
                  pl.BlockSpec((1, d), lambda i, c: (0, 0)),
                  pl.BlockSpec((1, d), lambda i, c: (0, 0))],
        out_specs=pl.BlockSpec((tm, d), lambda i, c: (i, 0)),
        out_shape=jax.ShapeDtypeStruct((n, d), F32),
        scratch_shapes=[pltpu.VMEM((tm, d), F32)],
        compiler_params=pltpu.CompilerParams(dimension_semantics=("arbitrary", "arbitrary"),
                                             vmem_limit_bytes=VMEM_LIMIT),
        name="ffn_ln",
    )(x, w_gu, w_gu, w_down, g, b)


def _moe_ln_kernel(x_ref, wrh_ref, wrl_ref, wat_ref, wgt_ref, wdt_ref, g_ref, b_ref, o_ref,
                   xt_s, acct_s, route_s, routet_s, xct_s, yct_s, *, alpha, n_chunks, n_valid, blk):
    i = pl.program_id(0)
    e = pl.program_id(1)
    c = pl.program_id(2)
    tmm = x_ref.shape[0]
    lane = _iota((1, LANES), 1).astype(F32)
    ef = e.astype(F32)

    @pl.when((e == 0) & (c == 0))
    def _():
        x = x_ref[...]
        xh = x.astype(BF16)
        xl = (x - xh.astype(F32)).astype(BF16)
        logits = _dot(xh, wrh_ref[...]) + _dot(xl, wrh_ref[...]) + _dot(xh, wrl_ref[...])
        lg = jnp.where(lane < N_EXPERTS, logits, NEG_INF)
        m1 = jnp.max(lg, axis=-1, keepdims=True)
        i1 = jnp.min(jnp.where(lg == m1, lane, float(LANES)), axis=-1, keepdims=True)
        lg2 = jnp.where(lane == i1, NEG_INF, lg)
        m2 = jnp.max(lg2, axis=-1, keepdims=True)
        i2 = jnp.min(jnp.where(lg2 == m2, lane, float(LANES)), axis=-1, keepdims=True)
        e2 = jnp.exp(m2 - m1)
        den = 1.0 + e2
        comb = jnp.where(lane == i1, 1.0 / den, 0.0) + jnp.where(lane == i2, e2 / den, 0.0)
        real = (i * tmm + _iota((tmm, 1), 0)) < n_valid
        sel = jnp.where(real & ((lane == i1) | (lane == i2)), 1.0, 0.0)
        before = jnp.where(_iota((tmm, tmm), 0) > _iota((tmm, tmm), 1), 1.0, 0.0).astype(BF16)
        rank = _dot(before, sel.astype(BF16))
        for idx, val in enumerate((comb, sel, rank)):
            route_s[idx] = val
            routet_s[idx] = val.T
        xt_s[...] = x.T.astype(BF16)
        acct_s[...] = jnp.zeros_like(acct_s)

    def pick(ref3, idx):
        return jnp.sum(jnp.where(lane == ef, ref3[idx], 0.0), axis=-1, keepdims=True)

    count = jnp.sum(jnp.where(lane == ef, route_s[1], 0.0))
    n_blocks = (count.astype(jnp.int32) + blk - 1) // blk

    @pl.when(c == 0)
    def _():
        sel_col, rank_col = pick(route_s, 1), pick(route_s, 2)

        def compact(k, carry):
            slot = (k * blk + _iota((1, blk), 1)).astype(F32)
            onehot = jnp.where(rank_col == slot, sel_col, 0.0).astype(BF16)
            xct_s[k] = _dot(xt_s[...], onehot).astype(BF16)
            yct_s[k] = jnp.zeros(yct_s.shape[1:], F32)
            return carry

        lax.fori_loop(0, n_blocks, compact, 0)

    def expert(k, carry):
        xc = xct_s[k]
        a = _dot(wat_ref[...], xc)
        gg = _dot(wgt_ref[...], xc)
        yct_s[k] += _dot(wdt_ref[...], (jax.nn.silu(a) * gg).astype(BF16))
        return carry

    lax.fori_loop(0, n_blocks, expert, 0)

    @pl.when(c == n_chunks - 1)
    def _():
        w_row = routet_s[0, pl.ds(e, 1), :]
        sel_row = routet_s[1, pl.ds(e, 1), :]
        rank_row = routet_s[2, pl.ds(e, 1), :]

        def scatter(k, carry):
            slot = (k * blk + _iota((blk, 1), 0)).astype(F32)
            onehot = jnp.where(rank_row == slot, sel_row, 0.0).astype(BF16)
            y = yct_s[k]
            yh = y.astype(BF16)
            yl = (y - yh.astype(F32)).astype(BF16)
            acct_s[...] += w_row * (_dot(yh, onehot) + _dot(yl, onehot))
            return carry

        lax.fori_loop(0, n_blocks, scatter, 0)

    @pl.when((e == N_EXPERTS - 1) & (c == n_chunks - 1))
    def _():
        o_ref[...] = _layer_norm(alpha * x_ref[...] + acct_s[...].T, g_ref[...], b_ref[...])


def _moe_ln(x, wr_hi, wr_lo, w_gu_t, w_down_t, g, b, *, alpha, tmm, fc, blk):
    n, d = x.shape
    dff = w_down_t.shape[2]
    nch = dff // fc
    n_tiles = -(-n // tmm)
    n_pad = n_tiles * tmm
    xp = jnp.pad(x, ((0, n_pad - n), (0, 0)))
    max_blocks = -(-tmm // blk)
    out = pl.pallas_call(
        functools.partial(_moe_ln_kernel, alpha=alpha, n_chunks=nch, n_valid=n, blk=blk),
        grid=(n_tiles, N_EXPERTS, nch),
        in_specs=[pl.BlockSpec((tmm, d), lambda i, e, c: (i, 0)),
                  pl.BlockSpec((d, LANES), lambda i, e, c: (0, 0)),
                  pl.BlockSpec((d, LANES), lambda i, e, c: (0, 0)),
                  pl.BlockSpec((None, fc, d), lambda i, e, c: (e, c, 0)),
                  pl.BlockSpec((None, fc, d), lambda i, e, c: (e, c + nch, 0)),
                  pl.BlockSpec((None, d, fc), lambda i, e, c: (e, 0, c)),
                  pl.BlockSpec((1, d), lambda i, e, c: (0, 0)),
                  pl.BlockSpec((1, d), lambda i, e, c: (0, 0))],
        out_specs=pl.BlockSpec((tmm, d), lambda i, e, c: (i, 0)),
        out_shape=jax.ShapeDtypeStruct((n_pad, d), F32),
        scratch_shapes=[pltpu.VMEM((d, tmm), BF16), pltpu.VMEM((d, tmm), F32),
                        pltpu.VMEM((3, tmm, LANES), F32), pltpu.VMEM((3, LANES, tmm), F32),
                        pltpu.VMEM((max_blocks, d, blk), BF16), pltpu.VMEM((max_blocks, d, blk), F32)],
        compiler_params=pltpu.CompilerParams(dimension_semantics=("arbitrary", "arbitrary", "arbitrary"),
                                             vmem_limit_bytes=VMEM_LIMIT),
        name="moe_ln",
    )(xp, wr_hi, wr_lo, w_gu_t, w_gu_t, w_down_t, g, b)
    return out[:n]


def _log_sigmoid(z):
    return jnp.minimum(z, 0.0) - jnp.log1p(jnp.exp(-jnp.abs(z)))


def _forget_lanes(x):
    lane = _iota((1, LANES), 1)
    return jnp.where((lane >= MISC_FF) & (lane < MISC_FF + FOX_HEADS), x, 0.0)


def _cum_small(lf):
    rows = lf.shape[0]
    ri = _iota((rows, LANES), 0)
    cum = jnp.zeros((rows, LANES), F32)
    for t in range(rows):
        cum = cum + jnp.where(ri >= t, lf[t:t + 1, :], 0.0)
    return cum


def _cmp_summaries(kc_ref, vc_ref, pe_ref, phi_ref, crt_ref, nc):
    acc_k = jnp.zeros((nc, LANES), F32)
    acc_v = jnp.zeros((nc, LANES), F32)
    for l in range(CMP_BLOCK):
        rk = kc_ref[pl.ds(l, nc, stride=CMP_BLOCK), :] + pe_ref[0, l:l + 1, :]
        acc_k = acc_k + _dot(rk.astype(BF16), phi_ref[0, l])
        rv = vc_ref[pl.ds(l, nc, stride=CMP_BLOCK), :] + pe_ref[1, l:l + 1, :]
        acc_v = acc_v + _dot(rv.astype(BF16), phi_ref[1, l])
    ck = _rope128(acc_k, crt_ref[:, 0:128], crt_ref[:, 128:256], crt_ref[:, 256:384])
    return ck, acc_v


def _nsa_cmp_select(q4, ck_e, ck_o, cv_e, cv_o, qpos, *, tq, nc, ns, topn):
    qpos4 = _tile_rows(qpos, NSA_GROUP)
    blk = _iota((1, SEL_W), 1)
    n_even, n_odd = (nc + 1) // 2, nc // 2
    ok_e = ((2 * blk + 1) * CMP_BLOCK - 1 <= qpos4) & (blk < n_even)
    ok_o = ((2 * blk + 2) * CMP_BLOCK - 1 <= qpos4) & (blk < n_odd)
    s_e = jnp.where(ok_e, _dot_nt(q4, ck_e), NEG_INF)
    s_o = jnp.where(ok_o, _dot_nt(q4, ck_o), NEG_INF)
    m = jnp.maximum(jnp.max(s_e, axis=-1, keepdims=True), jnp.max(s_o, axis=-1, keepdims=True))
    m = jnp.where(m == NEG_INF, 0.0, m)
    p_e = jnp.exp(s_e - m)
    p_o = jnp.exp(s_o - m)
    den = jnp.sum(p_e, axis=-1, keepdims=True) + jnp.sum(p_o, axis=-1, keepdims=True)
    inv = 1.0 / jnp.maximum(den, 1e-30)
    p_e = p_e * inv
    p_o = p_o * inv
    o_cmp = _dot(p_e.astype(BF16), cv_e) + _dot(p_o.astype(BF16), cv_o)
    pp = p_e + p_o
    imp = pp[0:tq]
    for r in range(1, NSA_GROUP):
        imp = imp + pp[r * tq:(r + 1) * tq]
    valid = blk * SLC_BLOCK <= qpos
    forced = (blk == 0) | (blk == (qpos >> 6))
    score = jnp.where(forced, jnp.inf, jnp.where(valid, imp, NEG_INF))
    score = jnp.where(blk < ns, score, NEG_INF)
    rank = jnp.zeros((tq, SEL_W), F32)
    for b2 in range(ns):
        col = score[:, b2:b2 + 1]
        beats = (col > score) | ((col == score) & (blk > b2))
        rank = rank + jnp.where(beats, 1.0, 0.0)
    sel = jnp.where(rank < topn, 1.0, 0.0)
    return o_cmp, sel


def _block_expand(pos0, n):
    return jnp.where(_iota((SEL_W, n), 0) == ((pos0 + _iota((SEL_W, n), 1)) >> 6), 1.0, 0.0).astype(BF16)


def _topk_bias_store(parts, n_keep, reduce_rows, row_shape, bits=1):
    kk = float(n_keep)

    def count(pred):
        total = jnp.zeros(row_shape, F32)
        for load, _, idx in parts:
            total = total + reduce_rows(jnp.where(pred(load(), idx), 1.0, 0.0))
        return total

    def descend(step, thr):
        shift = 32 - bits * (step + 1)
        digit = jnp.zeros(row_shape, jnp.int32)
        for d in range(1, 1 << bits):
            cand = thr + (jnp.int32(d) << shift)
            digit = digit + jnp.where(count(lambda k, _, cand=cand: k >= cand) >= kk, 1, 0)
        return thr + (digit << shift)

    thr = lax.fori_loop(0, 32 // bits, descend, jnp.full(row_shape, -2147483648, jnp.int32))
    has_ties = jnp.max(count(lambda k, _: k >= thr)) > kk

    @pl.when(jnp.logical_not(has_ties))
    def _():
        for load, store, _ in parts:
            store(jnp.where(load() >= thr, 0.0, NEG_INF))

    @pl.when(has_ties)
    def _():
        need = kk - count(lambda k, _: k > thr)

        def widen(b, bound):
            cand = bound + (jnp.int32(1) << (15 - b))
            return jnp.where(count(lambda k, i: (k == thr) & (i < cand)) <= need, cand, bound)

        bound = lax.fori_loop(0, 16, widen, jnp.zeros(row_shape, jnp.int32))
        for load, store, idx in parts:
            k = load()
            store(jnp.where((k > thr) | ((k == thr) & (idx < bound)), 0.0, NEG_INF))


def _sortable_key(score):
    bits = pltpu.bitcast(score, jnp.int32)
    return bits ^ ((bits >> 31) & jnp.int32(0x7FFFFFFF))


def _cum_tri(lf, carry):
    rows = lf.shape[0]
    tri = jnp.where(_iota((rows, rows), 0) >= _iota((rows, rows), 1), 1.0, 0.0).astype(BF16)
    hi, mid, lo = _split3(lf)
    return _dot(tri, hi.astype(BF16)) + _dot(tri, mid.astype(BF16)) + _dot(tri, lo.astype(BF16)) + carry


def _fox_aug_cols(cum3, hh):
    chi, cmid, clo = [part[:, MISC_FF + hh:MISC_FF + hh + 1] for part in cum3]
    l64 = _iota((1, HEAD_DIM), 1)
    qx = jnp.where(l64 == 0, chi, jnp.where(l64 == 1, cmid, jnp.where(l64 == 2, clo, jnp.where(l64 < 6, 1.0, 0.0))))
    kx = jnp.where(l64 < 3, 1.0,
                   jnp.where(l64 == 3, -chi, jnp.where(l64 == 4, -cmid, jnp.where(l64 == 5, -clo, 0.0))))
    return qx, kx


def _fox_prep_kernel(fq_ref, fk_ref, misc_ref, fb_ref, qa_ref, ka_ref, lf_ref, carry_ref):
    j = pl.program_id(1)

    @pl.when(j == 0)
    def _():
        carry_ref[...] = jnp.zeros_like(carry_ref)

    lf = _forget_lanes(_log_sigmoid(misc_ref[...] + fb_ref[...]))
    lf_ref[...] = lf
    cum = _cum_tri(lf, carry_ref[...])
    rows = lf.shape[0]
    carry_ref[...] = cum[rows - 1:rows, :]
    cum3 = _split3(cum)
    for hh in range(FOX_HEADS):
        qx, kx = _fox_aug_cols(cum3, hh)
        qa_ref[:, hh * 128:hh * 128 + 64] = (fq_ref[:, hh * 64:(hh + 1) * 64] * 0.125).astype(BF16)
        qa_ref[:, hh * 128 + 64:(hh + 1) * 128] = qx.astype(BF16)
        ka_ref[:, hh * 128:hh * 128 + 64] = fk_ref[:, hh * 64:(hh + 1) * 64].astype(BF16)
        ka_ref[:, hh * 128 + 64:(hh + 1) * 128] = kx.astype(BF16)


def _fox_prep(h, fb, *, n_seq, t, rows):
    nr = t // rows
    n = n_seq * t
    return pl.pallas_call(
        _fox_prep_kernel,
        grid=(n_seq, nr),
        in_specs=[pl.BlockSpec((rows, 512), lambda s, j: (s * nr + j, E_FQ // 512)),
                  pl.BlockSpec((rows, 512), lambda s, j: (s * nr + j, E_FK // 512)),
                  pl.BlockSpec((rows, LANES), lambda s, j: (s * nr + j, E_MISC // LANES)),
                  pl.BlockSpec((1, LANES), lambda s, j: (0, 0))],
        out_specs=[pl.BlockSpec((rows, FOX_HEADS * 128), lambda s, j: (s * nr + j, 0)),
                   pl.BlockSpec((rows, FOX_HEADS * 128), lambda s, j: (s * nr + j, 0)),
                   pl.BlockSpec((rows, LANES), lambda s, j: (s * nr + j, 0))],
        out_shape=[jax.ShapeDtypeStruct((n, FOX_HEADS * 128), BF16),
                   jax.ShapeDtypeStruct((n, FOX_HEADS * 128), BF16),
                   jax.ShapeDtypeStruct((n, LANES), F32)],
        scratch_shapes=[pltpu.VMEM((1, LANES), F32)],
        compiler_params=pltpu.CompilerParams(dimension_semantics=("arbitrary", "arbitrary")),
        name="fox_prep",
    )(h, h, h, fb)


def _fox_attn_kernel(qa_ref, ka_ref, v_ref, o_ref, m_s, l_s, acc_s, *, tq, t, hps):
    i = pl.program_id(2)
    qpos = i * tq + _iota((tq, 1), 0)
    q_end = i * tq + tq - 1
    for hh in range(hps):
        _st_reset(m_s.at[hh], l_s.at[hh], acc_s.at[hh])
    for r0, n in _key_chunks(t, KEY_CHUNK):
        @pl.when(r0 <= q_end)
        def _():
            bias = jnp.where((r0 + _iota((1, n), 1)) <= qpos, 0.0, NEG_INF)
            for hh in range(hps):
                k = ka_ref[r0:r0 + n, hh * 128:(hh + 1) * 128]
                v = v_ref[r0:r0 + n, hh * 64:(hh + 1) * 64].astype(BF16)
                s = _dot_nt(qa_ref[:, hh * 128:(hh + 1) * 128], k) + bias
                _st_update(m_s.at[hh], l_s.at[hh], acc_s.at[hh], s, lambda pb, v=v: _dot(pb, v))
    for hh in range(hps):
        o_ref[:, hh * 64:(hh + 1) * 64] = _st_finish(l_s.at[hh], acc_s.at[hh])


def _fox_attn(qa, ka, h, *, n_seq, t, tq, hps):
    nq = t // tq
    n = n_seq * t
    return pl.pallas_call(
        functools.partial(_fox_attn_kernel, tq=tq, t=t, hps=hps),
        grid=(n_seq, FOX_HEADS // hps, nq),
        in_specs=[pl.BlockSpec((tq, hps * 128), lambda s, p, i: (s * nq + i, p)),
                  pl.BlockSpec((t, hps * 128), lambda s, p, i: (s, p)),
                  pl.BlockSpec((t, hps * 64), lambda s, p, i: (s, E_FV // (hps * 64) + p))],
        out_specs=pl.BlockSpec((tq, hps * 64), lambda s, p, i: (s * nq + i, p)),
        out_shape=jax.ShapeDtypeStruct((n, FOX_HEADS * HEAD_DIM), F32),
        scratch_shapes=[pltpu.VMEM((hps, tq, 1), F32), pltpu.VMEM((hps, tq, 1), F32),
                        pltpu.VMEM((hps, tq, HEAD_DIM), F32)],
        compiler_params=pltpu.CompilerParams(dimension_semantics=("arbitrary", "arbitrary", "arbitrary"),
                                             vmem_limit_bytes=VMEM_LIMIT),
        name="fox_attn",
    )(qa, ka, h)


def _nsa_kernel(hq_ref, misc_ref, kc_ref, vc_ref, slc_ref, win_ref, pe_ref, phi_ref, crt_ref, o_ref,
                ck_s, cv_s, m_s, l_s, acc_s, *, tq, t, nc, ns, topn):
    i = pl.program_id(1)

    @pl.when(i == 0)
    def _():
        ck, cv = _cmp_summaries(kc_ref, vc_ref, pe_ref, phi_ref, crt_ref, nc)
        ck_s[...] = jnp.zeros_like(ck_s)
        cv_s[...] = jnp.zeros_like(cv_s)
        ck_s[0:nc, :] = ck
        cv_s[0:nc, :] = cv

    qs = i * tq
    q_end = qs + tq - 1
    qpos = qs + _iota((tq, 1), 0)
    gates = jax.nn.sigmoid(misc_ref[...])
    n_win = min(WINDOW + tq, t)
    w0 = pl.multiple_of(jnp.maximum(qs + tq - n_win, 0), 8)

    groups = range(NSA_KV_HEADS)
    kcols = [slice(g * 64, (g + 1) * 64) for g in groups]
    vcols = [slice(128 + g * 64, 128 + (g + 1) * 64) for g in groups]
    qh = [(hq_ref[:, hh * 64:(hh + 1) * 64] * 0.125).astype(BF16) for hh in range(NSA_HEADS)]
    o_cmps, sels = [], []
    for g in groups:
        q4 = jnp.concatenate(qh[NSA_GROUP * g:NSA_GROUP * (g + 1)], axis=0)
        ck_e = ck_s[pl.ds(0, SEL_W, stride=2), :][:, kcols[g]].astype(BF16)
        ck_o = ck_s[pl.ds(1, SEL_W, stride=2), :][:, kcols[g]].astype(BF16)
        cv_e = cv_s[pl.ds(0, SEL_W, stride=2), :][:, kcols[g]].astype(BF16)
        cv_o = cv_s[pl.ds(1, SEL_W, stride=2), :][:, kcols[g]].astype(BF16)
        o_cmp, sel = _nsa_cmp_select(q4, ck_e, ck_o, cv_e, cv_o, qpos, tq=tq, nc=nc, ns=ns, topn=topn)
        o_cmps.append(o_cmp)
        sels.append(sel.astype(BF16))
    for hh in range(NSA_HEADS):
        _st_reset(m_s.at[hh], l_s.at[hh], acc_s.at[hh])

    for r0, n in _key_chunks(t, KEY_CHUNK):
        @pl.when(r0 <= q_end)
        def _():
            expand = _block_expand(r0, n)
            causal = (r0 + _iota((1, n), 1)) <= qpos
            for g in groups:
                k = slc_ref[r0:r0 + n, kcols[g]].astype(BF16)
                v = slc_ref[r0:r0 + n, vcols[g]].astype(BF16)
                bias = jnp.where((_dot(sels[g], expand) > 0.5) & causal, 0.0, NEG_INF)
                for r in range(NSA_GROUP):
                    hh = NSA_GROUP * g + r
                    _st_update(m_s.at[hh], l_s.at[hh], acc_s.at[hh], _dot_nt(qh[hh], k) + bias,
                               lambda pb, v=v: _dot(pb, v))

    kpos = w0 + _iota((1, n_win), 1)
    wbias = jnp.where((kpos <= qpos) & (kpos > qpos - WINDOW), 0.0, NEG_INF)
    for g in groups:
        k = win_ref[pl.ds(w0, n_win), kcols[g]].astype(BF16)
        v = win_ref[pl.ds(w0, n_win), vcols[g]].astype(BF16)
        for r in range(NSA_GROUP):
            hh = NSA_GROUP * g + r
            o_slc = _st_finish(l_s.at[hh], acc_s.at[hh])
            s = _dot_nt(qh[hh], k) + wbias
            p = jnp.exp(s - jnp.max(s, axis=-1, keepdims=True))
            o_win = _dot(p.astype(BF16), v) / jnp.maximum(jnp.sum(p, axis=-1, keepdims=True), 1e-30)
            c0 = MISC_GATES + 3 * hh
            o_ref[:, hh * 64:(hh + 1) * 64] = (gates[:, c0:c0 + 1] * o_cmps[g][r * tq:(r + 1) * tq]
                                               + gates[:, c0 + 1:c0 + 2] * o_slc + gates[:, c0 + 2:c0 + 3] * o_win)


def _nsa_attn(h, pe, phi, crt, *, n_seq, t, tq, topn):
    nq = t // tq
    n = n_seq * t
    nc = t // CMP_BLOCK
    ns = -(-t // SLC_BLOCK)
    assert ns <= SEL_W and nc <= 2 * SEL_W

    def qblk(col):
        return lambda s, i: (s * nq + i, col)

    def kblk(col):
        return lambda s, i: (s, col)

    return pl.pallas_call(
        functools.partial(_nsa_kernel, tq=tq, t=t, nc=nc, ns=ns, topn=min(topn, ns)),
        grid=(n_seq, nq),
        in_specs=[pl.BlockSpec((tq, 512), qblk(E_NQ // 512)),
                  pl.BlockSpec((tq, LANES), qblk(E_MISC // LANES)),
                  pl.BlockSpec((t, LANES), kblk(E_CMP // LANES)),
                  pl.BlockSpec((t, LANES), kblk(E_CMP // LANES + 1)),
                  pl.BlockSpec((t, 256), kblk(E_SLC // 256)),
                  pl.BlockSpec((t, 256), kblk(E_WIN // 256)),
                  pl.BlockSpec(pe.shape, lambda s, i: (0, 0, 0)),
                  pl.BlockSpec(phi.shape, lambda s, i: (0, 0, 0, 0)),
                  pl.BlockSpec(crt.shape, lambda s, i: (0, 0))],
        out_specs=pl.BlockSpec((tq, NSA_HEADS * HEAD_DIM), lambda s, i: (s * nq + i, 0)),
        out_shape=jax.ShapeDtypeStruct((n, NSA_HEADS * HEAD_DIM), F32),
        scratch_shapes=[pltpu.VMEM((2 * SEL_W, LANES), F32), pltpu.VMEM((2 * SEL_W, LANES), F32),
                        pltpu.VMEM((NSA_HEADS, tq, 1), F32), pltpu.VMEM((NSA_HEADS, tq, 1), F32),
                        pltpu.VMEM((NSA_HEADS, tq, HEAD_DIM), F32)],
        compiler_params=pltpu.CompilerParams(dimension_semantics=("arbitrary", "arbitrary"),
                                             vmem_limit_bytes=VMEM_LIMIT),
        name="nsa_attn",
    )(h, h, h, h, h, h, pe, phi, crt)


def _dsa_kernel(q_ref, qi_ref, kw_ref, kv_ref, ki_ref, o_ref, sc_s, key_s, m_s, l_s, acc_s, *, tq, t, n_keep):
    i = pl.program_id(1)
    qpos = i * tq + _iota((tq, 1), 0)
    q_end = i * tq + tq - 1
    chunks = _key_chunks(t, KEY_CHUNK)

    sc_s[...] = jnp.full(sc_s.shape, NEG_INF, F32)
    wi = kw_ref[:, IDX_DIM:IDX_DIM + IDX_HEADS] * (IDX_HEADS ** -0.5)
    qi = [(qi_ref[:, hh * IDX_DIM:(hh + 1) * IDX_DIM] * (IDX_DIM ** -0.5)).astype(BF16) for hh in range(IDX_HEADS)]
    for r0, n in chunks:
        @pl.when(r0 <= q_end)
        def _():
            ki = ki_ref[r0:r0 + n, 0:IDX_DIM].astype(BF16)
            sc = jnp.zeros((tq, n), F32)
            for hh in range(IDX_HEADS):
                sc = sc + wi[:, hh:hh + 1] * jnp.maximum(_dot_nt(qi[hh], ki), 0.0)
            ok = (r0 + _iota((1, n), 1)) <= qpos
            sc_s[:, r0:r0 + n] = jnp.where(ok, sc + 0.0, NEG_INF)

    key_s[...] = _sortable_key(sc_s[...])

    def key_part(r0, n):
        def load():
            return key_s[:, r0:r0 + n]

        def store(bias):
            sc_s[:, r0:r0 + n] = bias

        return load, store, r0 + _iota((1, n), 1)

    _topk_bias_store([key_part(r0, n) for r0, n in chunks], n_keep,
                     lambda ones: jnp.sum(ones, axis=-1, keepdims=True), (tq, 1))

    qs = [(q_ref[:, hh * 64:(hh + 1) * 64] * 0.125).astype(BF16) for hh in range(DSA_HEADS)]
    for hh in range(DSA_HEADS):
        _st_reset(m_s.at[hh], l_s.at[hh], acc_s.at[hh])
    for r0, n in chunks:
        @pl.when(r0 <= q_end)
        def _():
            ok = (r0 + _iota((1, n), 1)) <= qpos
            bias = jnp.where(ok, sc_s[:, r0:r0 + n], NEG_INF)
            for kh in range(DSA_KV_HEADS):
                k = kv_ref[r0:r0 + n, kh * 64:(kh + 1) * 64].astype(BF16)
                v = kv_ref[r0:r0 + n, 256 + kh * 64:256 + (kh + 1) * 64].astype(BF16)
                for r in range(DSA_GROUP):
                    hh = DSA_GROUP * kh + r
                    _st_update(m_s.at[hh], l_s.at[hh], acc_s.at[hh], _dot_nt(qs[hh], k) + bias,
                               lambda pb, v=v: _dot(pb, v))
    for hh in range(DSA_HEADS):
        o_ref[:, hh * 64:(hh + 1) * 64] = _st_finish(l_s.at[hh], acc_s.at[hh])


def _dsa_attn(h, *, n_seq, t, tq, n_keep):
    nq = t // tq
    n = n_seq * t
    assert t % LANES == 0 and t <= 1 << 16 and n_keep <= KEY_CHUNK

    def qblk(w, col):
        return pl.BlockSpec((tq, w), lambda s, i: (s * nq + i, col))

    return pl.pallas_call(
        functools.partial(_dsa_kernel, tq=tq, t=t, n_keep=n_keep),
        grid=(n_seq, nq),
        in_specs=[qblk(1024, O_Q // 1024), qblk(512, O_QI // 512), qblk(LANES, O_KI // LANES),
                  pl.BlockSpec((t, 512), lambda s, i: (s, O_KV // 512)),
                  pl.BlockSpec((t, LANES), lambda s, i: (s, O_KI // LANES))],
        out_specs=pl.BlockSpec((tq, DSA_HEADS * HEAD_DIM), lambda s, i: (s * nq + i, 0)),
        out_shape=jax.ShapeDtypeStruct((n, DSA_HEADS * HEAD_DIM), F32),
        scratch_shapes=[pltpu.VMEM((tq, t), F32), pltpu.VMEM((tq, t), jnp.int32),
                        pltpu.VMEM((DSA_HEADS, tq, 1), F32), pltpu.VMEM((DSA_HEADS, tq, 1), F32),
                        pltpu.VMEM((DSA_HEADS, tq, HEAD_DIM), F32)],
        compiler_params=pltpu.CompilerParams(dimension_semantics=("arbitrary", "arbitrary"),
                                             vmem_limit_bytes=VMEM_LIMIT),
        name="dsa_attn",
    )(h, h, h, h, h)


def _seq_page_specs(block, n_pages, useq, layer):
    tail = (0,) * (len(block) - 2)
    return [pl.BlockSpec(block, lambda s, pt, u=u, k=k: (pt[(s * useq + u) * n_pages + k], layer) + tail)
            for u in range(useq) for k in range(n_pages)]


def _softmax_over(blocks):
    m = None
    for s, _ in blocks:
        mb = jnp.max(s, axis=-1, keepdims=True)
        m = mb if m is None else jnp.maximum(m, mb)
    m = jnp.where(m == NEG_INF, 0.0, m)
    num = den = None
    for s, pv in blocks:
        p = jnp.exp(s - m)
        d = jnp.sum(p, axis=-1, keepdims=True)
        n = pv(p.astype(BF16))
        num, den = (n, d) if num is None else (num + n, den + d)
    return num / jnp.maximum(den, 1e-30)


def _wide(slabs, pick):
    return jnp.concatenate([pick(slab).astype(BF16) for slab in slabs], axis=1)


def _cmp_seq_kernel(pt_ref, *refs, useq, n_pages, nc):
    slabs = refs[:useq * n_pages]
    pe_ref, phi_ref, crt_ref, o_ref, k_s, v_s, c_s = refs[useq * n_pages:]
    for u in range(useq):
        for k in range(n_pages):
            slab = slabs[u * n_pages + k]
            rows = slice(k * PAGE_SIZE, (k + 1) * PAGE_SIZE)
            for g in range(NSA_KV_HEADS):
                k_s[u, rows, g * 64:(g + 1) * 64] = slab[0, g].T
                v_s[u, rows, g * 64:(g + 1) * 64] = slab[1, g].T
        ck, cv = _cmp_summaries(k_s.at[u], v_s.at[u], pe_ref, phi_ref, crt_ref, nc)
        for idx, val in ((0, ck), (2, cv)):
            c_s[u] = jnp.zeros(c_s.shape[1:], F32)
            c_s[u, 0:nc, :] = val
            o_ref[u, idx] = c_s[u, pl.ds(0, SEL_W, stride=2), :]
            o_ref[u, idx + 1] = c_s[u, pl.ds(1, SEL_W, stride=2), :]


def _cmp_seq(cmp_v, pt_flat, pe, phi, crt, *, layer, n_seq, n_pages, useq, nc):
    lp = n_pages * PAGE_SIZE
    in_specs = _seq_page_specs((None, None, 2, NSA_KV_HEADS, HEAD_DIM, PAGE_SIZE), n_pages, useq, layer)
    in_specs += [pl.BlockSpec(pe.shape, lambda s, pt: (0, 0, 0)),
                 pl.BlockSpec(phi.shape, lambda s, pt: (0, 0, 0, 0)),
                 pl.BlockSpec(crt.shape, lambda s, pt: (0, 0))]
    return pl.pallas_call(
        functools.partial(_cmp_seq_kernel, useq=useq, n_pages=n_pages, nc=nc),
        grid_spec=pltpu.PrefetchScalarGridSpec(
            num_scalar_prefetch=1, grid=(n_seq // useq,), in_specs=in_specs,
            out_specs=pl.BlockSpec((useq, 4, SEL_W, LANES), lambda s, pt: (s, 0, 0, 0)),
            scratch_shapes=[pltpu.VMEM((useq, lp, LANES), F32), pltpu.VMEM((useq, lp, LANES), F32),
                            pltpu.VMEM((useq, 2 * SEL_W, LANES), F32)]),
        out_shape=jax.ShapeDtypeStruct((n_seq, 4, SEL_W, LANES), F32),
        compiler_params=pltpu.CompilerParams(dimension_semantics=("arbitrary",), vmem_limit_bytes=VMEM_LIMIT),
        name="cmp_sample",
    )(pt_flat, *([cmp_v] * (useq * n_pages)), pe, phi, crt)


def _nsa_seq_kernel(pt_ref, *refs, useq, n_pages, dt, w_past, nc, ns, topn):
    hq_ref, misc_ref, slc_ref, win_ref, ckv_ref, winp_ref = refs[:6]
    slabs = refs[6:6 + useq * n_pages]
    o_ref = refs[6 + useq * n_pages]
    l_past = n_pages * PAGE_SIZE
    qpos = l_past + _iota((dt, 1), 0)
    new_pos = l_past + _iota((1, LANES), 1)
    new_ok = (new_pos <= qpos) & (new_pos < l_past + dt)
    expand_past = _block_expand(0, l_past)
    expand_new = _block_expand(l_past, LANES)
    wpos = l_past - w_past + _iota((1, w_past), 1)
    win_bias = _tile_rows(jnp.where(wpos > qpos - WINDOW, 0.0, NEG_INF), NSA_GROUP)
    win_new_bias = _tile_rows(jnp.where(new_ok & (new_pos > qpos - WINDOW), 0.0, NEG_INF), NSA_GROUP)
    for u in range(useq):
        rows = slice(u * dt, (u + 1) * dt)
        pages = slabs[u * n_pages:(u + 1) * n_pages]
        gates = jax.nn.sigmoid(misc_ref[rows, :])
        for g in range(NSA_KV_HEADS):
            kcol = slice(g * 64, (g + 1) * 64)
            vcol = slice(128 + g * 64, 128 + (g + 1) * 64)
            q4 = jnp.concatenate([hq_ref[rows, (NSA_GROUP * g + r) * 64:(NSA_GROUP * g + r + 1) * 64]
                                  for r in range(NSA_GROUP)], axis=0)
            q4 = (q4 * 0.125).astype(BF16)
            o_cmp, sel = _nsa_cmp_select(
                q4, ckv_ref[u, 0][:, kcol].astype(BF16), ckv_ref[u, 1][:, kcol].astype(BF16),
                ckv_ref[u, 2][:, kcol].astype(BF16), ckv_ref[u, 3][:, kcol].astype(BF16), qpos,
                tq=dt, nc=nc, ns=ns, topn=topn)
            sel = sel.astype(BF16)
            kt = _wide(pages, lambda slab: slab[0, g])
            vt = _wide(pages, lambda slab: slab[1, g])
            kn = _pad_rows(slc_ref[rows, kcol].astype(BF16), LANES)
            vn = _pad_rows(slc_ref[rows, vcol].astype(BF16), LANES)
            bias = _tile_rows(jnp.where(_dot(sel, expand_past) > 0.5, 0.0, NEG_INF), NSA_GROUP)
            bias_n = _tile_rows(jnp.where((_dot(sel, expand_new) > 0.5) & new_ok, 0.0, NEG_INF), NSA_GROUP)
            o_slc = _softmax_over([(_dot(q4, kt) + bias, lambda pb, vt=vt: _dot_nt(pb, vt)),
                                   (_dot_nt(q4, kn) + bias_n, lambda pb, vn=vn: _dot(pb, vn))])
            ktw = winp_ref[u, 0, g].astype(BF16)
            vtw = winp_ref[u, 1, g].astype(BF16)
            kn = _pad_rows(win_ref[rows, kcol].astype(BF16), LANES)
            vn = _pad_rows(win_ref[rows, vcol].astype(BF16), LANES)
            o_win = _softmax_over([(_dot(q4, ktw) + win_bias, lambda pb, vtw=vtw: _dot_nt(pb, vtw)),
                                   (_dot_nt(q4, kn) + win_new_bias, lambda pb, vn=vn: _dot(pb, vn))])
            for r in range(NSA_GROUP):
                hh = NSA_GROUP * g + r
                rs = slice(r * dt, (r + 1) * dt)
                c0 = MISC_GATES + 3 * hh
                o_ref[rows, hh * 64:(hh + 1) * 64] = (gates[:, c0:c0 + 1] * o_cmp[rs] + gates[:, c0 + 1:c0 + 2] * o_slc[rs]
                                                      + gates[:, c0 + 2:c0 + 3] * o_win[rs])


def _nsa_seq(h, ckv, slc_v, win_v, pt_flat, *, layer, row0, n_seq, dt, n_pages, useq, w_past, topn):
    l_tot = n_pages * PAGE_SIZE + dt
    nc = l_tot // CMP_BLOCK
    ns = -(-l_tot // SLC_BLOCK)
    assert ns <= SEL_W and nc <= 2 * SEL_W and dt <= LANES
    rows = useq * dt
    rb0 = row0 // rows

    def hblk(w, col):
        return pl.BlockSpec((rows, w), lambda s, pt: (rb0 + s, col))

    in_specs = [hblk(512, E_NQ // 512), hblk(LANES, E_MISC // LANES), hblk(256, E_SLC // 256), hblk(256, E_WIN // 256),
                pl.BlockSpec((useq, 4, SEL_W, LANES), lambda s, pt: (s, 0, 0, 0)),
                pl.BlockSpec((useq, None, 2, NSA_KV_HEADS, HEAD_DIM, w_past), lambda s, pt: (s, layer, 0, 0, 0, 0))]
    in_specs += _seq_page_specs((None, None, 2, NSA_KV_HEADS, HEAD_DIM, PAGE_SIZE), n_pages, useq, layer)
    return pl.pallas_call(
        functools.partial(_nsa_seq_kernel, useq=useq, n_pages=n_pages, dt=dt, w_past=w_past, nc=nc, ns=ns,
                          topn=min(topn, ns)),
        grid_spec=pltpu.PrefetchScalarGridSpec(
            num_scalar_prefetch=1, grid=(n_seq // useq,), in_specs=in_specs,
            out_specs=pl.BlockSpec((rows, NSA_HEADS * HEAD_DIM), lambda s, pt: (s, 0))),
        out_shape=jax.ShapeDtypeStruct((n_seq * dt, NSA_HEADS * HEAD_DIM), F32),
        compiler_params=pltpu.CompilerParams(dimension_semantics=("arbitrary",), vmem_limit_bytes=VMEM_LIMIT),
        name="nsa_sample",
    )(pt_flat, h, h, h, h, ckv, win_v, *([slc_v] * (useq * n_pages)))


def _fox_seq_kernel(pt_ref, *refs, useq, n_pages, dt):
    fq_ref, fk_ref, fv_ref, misc_ref, fb_ref = refs[:5]
    kv_slabs = refs[5:5 + useq * n_pages]
    lf_slabs = refs[5 + useq * n_pages:5 + 2 * useq * n_pages]
    o_ref, lf_ref, qbd_s = refs[5 + 2 * useq * n_pages:]
    width = FOX_HEADS * HEAD_DIM
    l_past = n_pages * PAGE_SIZE

    def head_rows(x):
        return jnp.concatenate([jnp.broadcast_to(x[hh:hh + 1, :], (dt, x.shape[1])) for hh in range(FOX_HEADS)], axis=0)

    upper = jnp.where(_iota((LANES, LANES), 0) <= _iota((LANES, LANES), 1), 1.0, 0.0).astype(BF16)
    lane = _iota((dt, LANES), 1)
    pick = jnp.concatenate([jnp.where(lane == MISC_FF + hh, 1.0, 0.0) for hh in range(FOX_HEADS)], axis=0).astype(BF16)
    col = _iota((1, LANES), 1)
    new_ok = (col <= _tile_rows(_iota((dt, 1), 0), FOX_HEADS)) & (col < dt)
    for u in range(useq):
        rows = slice(u * dt, (u + 1) * dt)
        qbd_s[u] = jnp.zeros(qbd_s.shape[1:], BF16)
        for hh in range(FOX_HEADS):
            cs = slice(hh * 64, (hh + 1) * 64)
            qbd_s[u, hh * dt:(hh + 1) * dt, cs] = (fq_ref[rows, cs] * 0.125).astype(BF16)
        qbd = qbd_s[u]
        offset = jnp.zeros((FOX_HEADS, LANES), F32)
        cums = []
        for k in range(n_pages):
            hi, mid, lo = _split3(lf_slabs[u * n_pages + k][...])
            local = _dot(hi.astype(BF16), upper) + _dot(mid.astype(BF16), upper) + _dot(lo.astype(BF16), upper)
            cums.append(local + offset)
            offset = jnp.broadcast_to(cums[-1][:, LANES - 1:LANES], (FOX_HEADS, LANES))
        pages = kv_slabs[u * n_pages:(u + 1) * n_pages]
        kt = _wide(pages, lambda slab: slab[0].reshape(width, PAGE_SIZE))
        vt = _wide(pages, lambda slab: slab[1].reshape(width, PAGE_SIZE))
        s_past = _dot(qbd, kt) - head_rows(jnp.concatenate(cums, axis=1))
        lf = _forget_lanes(_log_sigmoid(misc_ref[rows, :] + fb_ref[...]))
        lf_ref[rows, :] = lf
        parts = _split3(_pad_rows(_cum_small(lf), LANES))
        cum_new = sum(_dot_nt(pick, part.astype(BF16)) for part in parts)
        kn = _pad_rows(fk_ref[rows, :].astype(BF16), LANES)
        vn = _pad_rows(fv_ref[rows, :].astype(BF16), LANES)
        s_new = jnp.where(new_ok, _dot_nt(qbd, kn) - (head_rows(offset) + cum_new), NEG_INF)
        o = _softmax_over([(s_past, lambda pb, vt=vt: _dot_nt(pb, vt)), (s_new, lambda pb, vn=vn: _dot(pb, vn))])
        for hh in range(FOX_HEADS):
            o_ref[rows, hh * 64:(hh + 1) * 64] = o[hh * dt:(hh + 1) * dt, hh * 64:(hh + 1) * 64]


def _fox_seq(h, fb, fox_v, flf_v, pt_flat, *, layer, row0, n_seq, dt, n_pages, useq):
    rows = useq * dt
    rb0 = row0 // rows
    width = FOX_HEADS * HEAD_DIM
    assert dt <= LANES

    def hblk(w, col):
        return pl.BlockSpec((rows, w), lambda s, pt: (rb0 + s, col))

    in_specs = [hblk(512, E_FQ // 512), hblk(512, E_FK // 512), hblk(512, E_FV // 512), hblk(LANES, E_MISC // LANES),
                pl.BlockSpec((1, LANES), lambda s, pt: (0, 0))]
    in_specs += _seq_page_specs((None, None, 2, FOX_HEADS, HEAD_DIM, PAGE_SIZE), n_pages, useq, layer)
    in_specs += _seq_page_specs((None, None, FOX_HEADS, PAGE_SIZE), n_pages, useq, layer)
    return pl.pallas_call(
        functools.partial(_fox_seq_kernel, useq=useq, n_pages=n_pages, dt=dt),
        grid_spec=pltpu.PrefetchScalarGridSpec(
            num_scalar_prefetch=1, grid=(n_seq // useq,), in_specs=in_specs,
            out_specs=[pl.BlockSpec((rows, width), lambda s, pt: (s, 0)),
                       pl.BlockSpec((rows, LANES), lambda s, pt: (s, 0))],
            scratch_shapes=[pltpu.VMEM((useq, FOX_HEADS * dt, width), BF16)]),
        out_shape=[jax.ShapeDtypeStruct((n_seq * dt, width), F32), jax.ShapeDtypeStruct((n_seq * dt, LANES), F32)],
        compiler_params=pltpu.CompilerParams(dimension_semantics=("arbitrary",), vmem_limit_bytes=VMEM_LIMIT),
        name="fox_sample",
    )(pt_flat, h, h, h, h, fb, *([fox_v] * (useq * n_pages)), *([flf_v] * (useq * n_pages)))


def _dsa_seq_kernel(pt_ref, *refs, useq, n_pages, dt, n_keep):
    q_ref, qi_ref, kw_ref, kv_ref = refs[:4]
    idx_slabs = refs[4:4 + useq * n_pages]
    kv_slabs = refs[4 + useq * n_pages:4 + 2 * useq * n_pages]
    o_ref, sc_s, key_s, qbd_s = refs[4 + 2 * useq * n_pages:]
    width = DSA_KV_HEADS * HEAD_DIM
    col = _iota((1, LANES), 1)
    new_ok = (col <= _iota((dt, 1), 0)) & (col < dt)

    for u in range(useq):
        rows = slice(u * dt, (u + 1) * dt)
        wi = kw_ref[rows, IDX_DIM:IDX_DIM + IDX_HEADS] * (IDX_HEADS ** -0.5)
        qst = jnp.concatenate([qi_ref[rows, hh * IDX_DIM:(hh + 1) * IDX_DIM] for hh in range(IDX_HEADS)], axis=0)
        qst = (qst * (IDX_DIM ** -0.5)).astype(BF16)

        def scores(lg, wi=wi):
            sc = jnp.zeros((dt, lg.shape[1]), F32)
            for hh in range(IDX_HEADS):
                sc = sc + wi[:, hh:hh + 1] * jnp.maximum(lg[hh * dt:(hh + 1) * dt], 0.0)
            return sc + 0.0

        wide = scores(_dot(qst, _wide(idx_slabs[u * n_pages:(u + 1) * n_pages], lambda slab: slab[...])))
        for k in range(n_pages):
            sc_s[k, rows, :] = wide[:, k * PAGE_SIZE:(k + 1) * PAGE_SIZE]
        kin = _pad_rows(kw_ref[rows, 0:IDX_DIM].astype(BF16), LANES)
        sc_s[n_pages, rows, :] = jnp.where(new_ok, scores(_dot_nt(qst, kin)), NEG_INF)
    key_s[...] = _sortable_key(sc_s[...])
    shape = (n_pages + 1, useq * dt, LANES)

    def store(bias):
        sc_s[...] = bias

    _topk_bias_store([(lambda: key_s[...], store, _iota(shape, 0) * LANES + _iota(shape, 2))], n_keep,
                     lambda ones: jnp.sum(jnp.sum(ones, axis=0), axis=-1, keepdims=True)[None],
                     (1, useq * dt, 1), bits=4)

    for u in range(useq):
        rows = slice(u * dt, (u + 1) * dt)
        qbd_s[u] = jnp.zeros(qbd_s.shape[1:], BF16)
        for hh in range(DSA_HEADS):
            kh = hh // DSA_GROUP
            qbd_s[u, hh * dt:(hh + 1) * dt, kh * 64:(kh + 1) * 64] = (q_ref[rows, hh * 64:(hh + 1) * 64] * 0.125).astype(BF16)
        qbd = qbd_s[u]
        pages = kv_slabs[u * n_pages:(u + 1) * n_pages]
        kt = _wide(pages, lambda slab: slab[0].reshape(width, PAGE_SIZE))
        vt = _wide(pages, lambda slab: slab[1].reshape(width, PAGE_SIZE))
        bias = jnp.concatenate([sc_s[k, rows, :] for k in range(n_pages)], axis=1)
        kn = _pad_rows(kv_ref[rows, 0:width].astype(BF16), LANES)
        vn = _pad_rows(kv_ref[rows, width:2 * width].astype(BF16), LANES)
        o = _softmax_over([(_dot(qbd, kt) + _tile_rows(bias, DSA_HEADS), lambda pb, vt=vt: _dot_nt(pb, vt)),
                           (_dot_nt(qbd, kn) + _tile_rows(sc_s[n_pages, rows, :], DSA_HEADS),
                            lambda pb, vn=vn: _dot(pb, vn))])
        for hh in range(DSA_HEADS):
            kh = hh // DSA_GROUP
            o_ref[rows, hh * 64:(hh + 1) * 64] = o[hh * dt:(hh + 1) * dt, kh * 64:(kh + 1) * 64]


def _dsa_seq(h, dik_v, dkv_v, pt_flat, *, layer, row0, n_seq, dt, n_pages, useq, n_keep):
    rows = useq * dt
    rb0 = row0 // rows
    width = DSA_KV_HEADS * HEAD_DIM
    shape = (n_pages + 1, rows, LANES)

    def hblk(w, col):
        return pl.BlockSpec((rows, w), lambda s, pt: (rb0 + s, col))

    in_specs = [hblk(1024, O_Q // 1024), hblk(512, O_QI // 512), hblk(LANES, O_KI // LANES), hblk(512, O_KV // 512)]
    in_specs += _seq_page_specs((None, None, IDX_DIM, PAGE_SIZE), n_pages, useq, layer)
    in_specs += _seq_page_specs((None, None, 2, DSA_KV_HEADS, HEAD_DIM, PAGE_SIZE), n_pages, useq, layer)
    return pl.pallas_call(
        functools.partial(_dsa_seq_kernel, useq=useq, n_pages=n_pages, dt=dt, n_keep=n_keep),
        grid_spec=pltpu.PrefetchScalarGridSpec(
            num_scalar_prefetch=1, grid=(n_seq // useq,), in_specs=in_specs,
            out_specs=pl.BlockSpec((rows, DSA_HEADS * HEAD_DIM), lambda s, pt: (s, 0)),
            scratch_shapes=[pltpu.VMEM(shape, F32), pltpu.VMEM(shape, jnp.int32),
                            pltpu.VMEM((useq, DSA_HEADS * dt, width), BF16)]),
        out_shape=jax.ShapeDtypeStruct((n_seq * dt, DSA_HEADS * HEAD_DIM), F32),
        compiler_params=pltpu.CompilerParams(dimension_semantics=("arbitrary",), vmem_limit_bytes=VMEM_LIMIT),
        name="dsa_sample",
    )(pt_flat, h, h, h, h, *([dik_v] * (useq * n_pages)), *([dkv_v] * (useq * n_pages)))


def _rope_table(pos):
    half = ROT_DIM // 2
    inv = ROPE_THETA ** (-2.0 * jnp.arange(half, dtype=F32) / ROT_DIM)
    ang = pos.astype(F32)[:, None] * inv[None, :]
    cos, sin = jnp.cos(ang), jnp.sin(ang)
    n = pos.shape[0]
    one = jnp.ones((n, HEAD_DIM - ROT_DIM), F32)
    zero = jnp.zeros((n, HEAD_DIM - ROT_DIM), F32)
    z8 = jnp.zeros((n, half), F32)
    c = jnp.concatenate([cos, cos, one], axis=1)
    s1 = jnp.concatenate([z8, sin, zero], axis=1)
    s2 = jnp.concatenate([-sin, z8, zero], axis=1)
    return jnp.concatenate([c, c, s1, s1, s2, s2], axis=1)


def _largest_tile(limit, *sizes):
    t = limit
    while any(s % t for s in sizes):
        t //= 2
    return t


def _rows_last(pool):
    nd = pool.ndim
    return jnp.transpose(pool, (0, 1) + tuple(range(3, nd)) + (2,))


def kernel(x_prompt, x_sample, cache_nsa_cmp_kv, cache_nsa_slc_kv, state_nsa_win_kv, cache_fox_kv, cache_fox_logf, cache_dsa_kv, cache_dsa_idx_k, page_table, ln_g, ln_b, w_in_even, w_out_even, fox_f_bias, nsa_cmp_pos, nsa_cmp_phi, w_in_odd, w_out_odd, ffn_gu, ffn_down, moe_router, moe_gu, moe_down):
    nb, t, d = x_prompt.shape
    db, dt, _ = x_sample.shape
    n_pages = page_table.shape[1]
    lp = n_pages * PAGE_SIZE
    w_past = state_nsa_win_kv.shape[2]
    n_p, n_s = nb * t, db * dt
    depth = ln_g.shape[0]
    alpha = (2 * depth) ** 0.25
    tm = _largest_tile(512, n_p, n_s)
    tm_ff = _largest_tile(512, n_p, n_s)
    tq = _largest_tile(256, t)
    useq = math.gcd(db, SEQS_PER_STEP)
    pt_flat = page_table.reshape(-1).astype(jnp.int32)

    cmp_v, slc_v, win_v = _rows_last(cache_nsa_cmp_kv), _rows_last(cache_nsa_slc_kv), _rows_last(state_nsa_win_kv)
    fox_v, flf_v = _rows_last(cache_fox_kv), _rows_last(cache_fox_logf)
    dkv_v, dik_v = _rows_last(cache_dsa_kv), _rows_last(cache_dsa_idx_k)

    x = jnp.concatenate([x_prompt.reshape(n_p, d), x_sample.reshape(n_s, d)], axis=0)
    pos = jnp.concatenate([jnp.tile(jnp.arange(t), nb), jnp.tile(lp + jnp.arange(dt), db)])
    rt = _rope_table(pos)
    nc_p, nc_s = t // CMP_BLOCK, (lp + dt) // CMP_BLOCK
    assert lp % CMP_BLOCK == 0 and dt < CMP_BLOCK
    crt_p = _rope_table((jnp.arange(nc_p) + 1) * CMP_BLOCK - 1)
    crt_s = _rope_table((jnp.arange(nc_s) + 1) * CMP_BLOCK - 1)

    ev_p, ev_s, od_p, od_s = [], [], [], []
    for layer in range(depth):
        j = layer // 2
        g0, b0 = ln_g[layer, 0][None, :], ln_b[layer, 0][None, :]
        g1, b1 = ln_g[layer, 1][None, :], ln_b[layer, 1][None, :]
        if layer % 2 == 0:
            w = w_in_even[j]
            w = jnp.concatenate([w[:, 0:512], w[:, 1304:2840], w[:, 512:1280], w[:, 1280:1304], w[:, 2840:2848],
                                 jnp.zeros((d, E_W - 2848), w.dtype)], axis=1).astype(BF16)
            h = _proj(x, w, rt, width=E_W, rope_full=EVEN_ROPE, rope_lo=(), tm=tm)

            pe = jnp.concatenate([nsa_cmp_pos[j], nsa_cmp_pos[j]], axis=-1)
            ph = nsa_cmp_phi[j]
            zz = jnp.zeros_like(ph)
            phi = jnp.concatenate([jnp.concatenate([ph, zz], axis=-1),
                                   jnp.concatenate([zz, ph], axis=-1)], axis=-2).astype(BF16)
            fb = jnp.zeros((1, LANES), F32).at[0, MISC_FF:MISC_FF + FOX_HEADS].set(fox_f_bias[j])

            qa_p, ka_p, lf_p = _fox_prep(h, fb, n_seq=nb, t=t, rows=_largest_tile(256, t))
            o_nsa_p = _nsa_attn(h, pe, phi, crt_p, n_seq=nb, t=t, tq=tq, topn=SLC_TOPN)
            o_fox_p = _fox_attn(qa_p, ka_p, h, n_seq=nb, t=t, tq=tq, hps=FOX_HEADS)

            ckv = _cmp_seq(cmp_v, pt_flat, pe, phi, crt_s, layer=j, n_seq=db, n_pages=n_pages, useq=useq, nc=nc_s)
            o_nsa_s = _nsa_seq(h, ckv, slc_v, win_v, pt_flat, layer=j, row0=n_p, n_seq=db, dt=dt, n_pages=n_pages,
                               useq=useq, w_past=w_past, topn=SLC_TOPN)
            o_fox_s, lf_s = _fox_seq(h, fb, fox_v, flf_v, pt_flat, layer=j, row0=n_p, n_seq=db, dt=dt,
                                     n_pages=n_pages, useq=useq)

            wo = w_out_even[j].astype(BF16)
            x = _out_ln([o_nsa_p, o_fox_p], [o_nsa_s, o_fox_s], [wo[0:512], wo[512:1024]], x, g0, b0,
                        alpha=alpha, tm=tm)
            x = _ffn_ln(x, ffn_gu[j].astype(BF16), ffn_down[j].astype(BF16), g1, b1, alpha=alpha, tm=tm_ff,
                        fc=ffn_down.shape[1] // 2)

            hp, hs = h[:n_p].reshape(nb, t, E_W), h[n_p:].reshape(db, dt, E_W)
            kv2 = (2, NSA_KV_HEADS, HEAD_DIM)
            win_new = hs[:, :, E_WIN:E_WIN + 256].reshape((db, dt) + kv2)
            keep_p, keep_s = min(WINDOW, t), min(WINDOW, w_past + dt)
            win_s = jnp.concatenate([state_nsa_win_kv[:, j, w_past + dt - keep_s:], win_new], axis=1)
            lf_p3 = lf_p[:, MISC_FF:MISC_FF + FOX_HEADS].reshape(nb, t, FOX_HEADS)
            lf_s3 = lf_s[:, MISC_FF:MISC_FF + FOX_HEADS].reshape(db, dt, FOX_HEADS)
            ev_p.append((hp[:, :, E_CMP:E_CMP + 256], hp[:, :, E_SLC:E_SLC + 256],
                         hp[:, t - keep_p:, E_WIN:E_WIN + 256], hp[:, :, E_FK:E_FK + 1024], lf_p3))
            ev_s.append((hs[:, :, E_CMP:E_CMP + 256], hs[:, :, E_SLC:E_SLC + 256],
                         win_s, hs[:, :, E_FK:E_FK + 1024], lf_s3))
        else:
            w = jnp.concatenate([w_in_odd[j], jnp.zeros((d, O_W - w_in_odd.shape[2]), w_in_odd.dtype)],
                                axis=1).astype(BF16)
            h = _proj(x, w, rt, width=O_W, rope_full=ODD_ROPE, rope_lo=ODD_ROPE_LO, tm=tm)
            o_p = _dsa_attn(h, n_seq=nb, t=t, tq=tq, n_keep=min(IDX_TOPK, t // 4))
            o_s = _dsa_seq(h, dik_v, dkv_v, pt_flat, layer=j, row0=n_p, n_seq=db, dt=dt, n_pages=n_pages,
                           useq=useq, n_keep=min(IDX_TOPK, (lp + dt) // 4))
            x = _out_ln([o_p], [o_s], [w_out_odd[j].astype(BF16)], x, g0, b0, alpha=alpha, tm=tm)
            wr = jnp.concatenate([moe_router[j], jnp.zeros((d, LANES - N_EXPERTS), F32)], axis=1)
            wr_hi = wr.astype(BF16)
            wr_lo = (wr - wr_hi.astype(F32)).astype(BF16)
            x = _moe_ln(x, wr_hi, wr_lo, jnp.transpose(moe_gu[j], (0, 2, 1)).astype(BF16),
                        jnp.transpose(moe_down[j], (0, 2, 1)).astype(BF16), g1, b1,
                        alpha=alpha, tmm=MOE_TILE if n_p + n_s >= 8 * MOE_TILE else MOE_BLOCK,
                        fc=moe_down.shape[2] // 4, blk=MOE_BLOCK)
            hp, hs = h[:n_p].reshape(nb, t, O_W), h[n_p:].reshape(db, dt, O_W)
            od_p.append((hp[:, :, O_KV:O_KV + 512], hp[:, :, O_KI:O_KI + IDX_DIM]))
            od_s.append((hs[:, :, O_KV:O_KV + 512], hs[:, :, O_KI:O_KI + IDX_DIM]))

    def stk(lst, idx, tail):
        a = jnp.stack([s[idx] for s in lst], axis=1)
        return a.reshape(a.shape[:3] + tail)

    kv2 = (2, NSA_KV_HEADS, HEAD_DIM)
    fkv = (2, FOX_HEADS, HEAD_DIM)
    dkv = (2, DSA_KV_HEADS, HEAD_DIM)
    return (x[:n_p].reshape(nb, t, d), x[n_p:].reshape(db, dt, d),
            stk(ev_p, 0, kv2), stk(ev_s, 0, kv2), stk(ev_p, 1, kv2), stk(ev_s, 1, kv2),
            stk(ev_p, 2, kv2), stk(ev_s, 2, kv2), stk(ev_p, 3, fkv), stk(ev_s, 3, fkv),
            stk(ev_p, 4, (FOX_HEADS,)), stk(ev_s, 4, (FOX_HEADS,)),
            stk(od_p, 0, dkv), stk(od_s, 0, dkv), stk(od_p, 1, (IDX_DIM,)), stk(od_s, 1, (IDX_DIM,)))
```
